```python
import math
import jax
import jax.numpy as jnp
from jax import lax
import numpy as np

D_MODEL = 1024
BATCH = 8
SEQ = 2048
DEPTH = 4
DEC_BATCH = 128
DEC_SEQ = 8
PAST_LEN = 8192
PAGE_SIZE = 128

HEAD_DIM = 64
ROT_DIM = HEAD_DIM // 4
ROPE_THETA = 500000.0
ATTN_SCALE = HEAD_DIM ** -0.5

N_MIXERS = 2
N_A = (DEPTH + 1) // 2
N_B = DEPTH // 2

A_HEADS = D_MODEL // HEAD_DIM
A_KV = 2
A_G = A_HEADS // A_KV
CMP_BLOCK = 32
CMP_STRIDE = 16
CMP_HIDDEN = 4 * HEAD_DIM
SEL_BLOCK = 64
SEL_TOP = 16
A_WINDOW = 512
A_Q = A_HEADS * HEAD_DIM
A_KVW = A_KV * HEAD_DIM
A_IN = A_Q + 6 * A_KVW + 3 * A_HEADS
FORCE_SCORE = 1e9

B_HEADS = D_MODEL // HEAD_DIM
B_KV = 2
B_G = B_HEADS // B_KV
B_WINDOW = 128
B_Q = B_HEADS * HEAD_DIM
B_KVW = B_KV * HEAD_DIM
B_IN = B_Q + 2 * B_KVW

N_EXPERTS = 16
N_GROUPS = 4
EXPERTS_PER_GROUP = N_EXPERTS // N_GROUPS
TOP_K = 2
D_FF_EXPERT = D_MODEL // 2

DN_ALPHA = (2.0 * DEPTH) ** 0.25
DN_BETA = (8.0 * DEPTH) ** -0.25
LN_EPS = 1e-5
Q_BLOCK = 128
SEL_Q_BLOCK = 64
NEG_INF = -1e30
TINY = 1e-30

kernel_name = 'nsa_swa_sink_grouped_moe_decoder_step'


def _layer_norm(x, g, b):
    xf = x.astype(jnp.float32)
    mu = jnp.mean(xf, axis=-1, keepdims=True)
    xc = xf - mu
    var = jnp.mean(xc * xc, axis=-1, keepdims=True)
    return (xc * lax.rsqrt(var + LN_EPS) * g + b).astype(x.dtype)


def _rope(x, pos):
    half = ROT_DIM // 2
    inv = jnp.exp(-math.log(ROPE_THETA) * jnp.arange(half, dtype=jnp.float32) / half)
    ang = pos.astype(jnp.float32)[:, None] * inv[None, :]
    cos = jnp.cos(ang)[None, :, None, :]
    sin = jnp.sin(ang)[None, :, None, :]
    xf = x[..., :ROT_DIM].astype(jnp.float32)
    x1, x2 = xf[..., :half], xf[..., half:]
    rot = jnp.concatenate([x1 * cos - x2 * sin, x2 * cos + x1 * sin], axis=-1).astype(x.dtype)
    return jnp.concatenate([rot, x[..., ROT_DIM:]], axis=-1)


def _masked_probs(s, mask, sink=None):
    s = jnp.where(mask, s, NEG_INF)
    m = jnp.max(s, axis=-1, keepdims=True)
    if sink is not None:
        m = jnp.maximum(m, sink)
    e = jnp.where(mask, jnp.exp(s - m), 0.0)
    den = jnp.sum(e, axis=-1, keepdims=True)
    if sink is not None:
        den = den + jnp.exp(sink - m)
    return e / jnp.maximum(den, TINY)


def _split_cols(z, sizes):
    return jnp.split(z, np.cumsum(sizes)[:-1].tolist(), axis=-1)


def _banded_attend(q, rows, window, sink=None):
    b, t, kv, g, hd = q.shape
    nb = t // Q_BLOCK
    pad = -(-window // Q_BLOCK) * Q_BLOCK
    span = pad + Q_BLOCK
    rp = jnp.pad(rows, ((0, 0), (pad, 0), (0, 0), (0, 0), (0, 0)))
    idx = jnp.arange(nb)[:, None] * Q_BLOCK + jnp.arange(span)[None, :]
    kpos = idx - pad
    qpos = jnp.arange(t).reshape(nb, Q_BLOCK)
    d = qpos[:, :, None] - kpos[:, None, :]
    mask = (d >= 0) & (d <= window) & (kpos >= 0)[:, None, :]
    kb = rp[:, idx].swapaxes(0, 1)
    qb = q.reshape(b, nb, Q_BLOCK, kv, g, hd).swapaxes(0, 1)

    def one(args):
        qi, ki, mi = args
        s = jnp.einsum('bqkgd,bskd->bkgqs', qi.astype(jnp.float32), ki[..., 0, :].astype(jnp.float32)) * ATTN_SCALE
        p = _masked_probs(s, mi, sink)
        return jnp.einsum('bkgqs,bskd->bqkgd', p, ki[..., 1, :].astype(jnp.float32))

    o = lax.map(one, (qb, kb, mask))
    return o.swapaxes(0, 1).reshape(b, t, kv, g, hd)


def _window_attend_cached(q, buf, new_rows, qpos, window, sink=None):
    rows = jnp.concatenate([buf, new_rows], axis=1)
    wb = buf.shape[1]
    kpos = PAST_LEN - wb + jnp.arange(rows.shape[1])
    d = qpos[:, None] - kpos[None, :]
    mask = (d >= 0) & (d <= window)
    s = jnp.einsum('nqkgd,nskd->nkgqs', q.astype(jnp.float32), rows[..., 0, :].astype(jnp.float32)) * ATTN_SCALE
    p = _masked_probs(s, mask, sink)
    o = jnp.einsum('nkgqs,nskd->nqkgd', p, rows[..., 1, :].astype(jnp.float32))
    return o, rows[:, -wb:]


def _compress(rows, pe, w1, b1, w2):
    n, l = rows.shape[:2]
    r_sh = CMP_BLOCK // CMP_STRIDE
    nch = l // CMP_STRIDE
    nc = nch - r_sh + 1
    ch = rows.reshape(n, nch, CMP_STRIDE, A_KV, 2, HEAD_DIM)
    w1r = w1.reshape(2, r_sh, CMP_STRIDE, HEAD_DIM, CMP_HIDDEN)
    part = jnp.einsum('ncskid,irsdh->rnckih', ch, w1r)
    pre = sum(part[r, :, r:r + nc] for r in range(r_sh))
    pe_term = jnp.einsum('ibd,ibdh->ih', pe, w1.reshape(2, CMP_BLOCK, HEAD_DIM, CMP_HIDDEN))
    hid = jax.nn.gelu(pre + pe_term + b1)
    return jnp.einsum('nckih,ihd->nckid', hid, w2)


def _nsa_project(h, w_in, pos):
    n, t, _ = h.shape
    z = h @ w_in
    q, kc, vc, ks, vs, kw, vw, g = _split_cols(z, [A_Q] + [A_KVW] * 6 + [3 * A_HEADS])
    q = q.reshape(n, t, A_HEADS, HEAD_DIM)
    q_rot = _rope(q, pos).reshape(n, t, A_KV, A_G, HEAD_DIM)
    q = q.reshape(n, t, A_KV, A_G, HEAD_DIM)

    def heads(a):
        return a.reshape(n, t, A_KV, HEAD_DIM)

    cmp_rows = jnp.stack([heads(kc), heads(vc)], axis=3)
    slc_rows = jnp.stack([_rope(heads(ks), pos), heads(vs)], axis=3)
    win_rows = jnp.stack([_rope(heads(kw), pos), heads(vw)], axis=3)
    gates = jax.nn.sigmoid(g.astype(jnp.float32)).reshape(n, t, 3, A_KV, A_G)
    return q, q_rot, cmp_rows, slc_rows, win_rows, gates


def _nsa_compressed(q, rows, qpos, pe, w1, b1, w2):
    blocks = _compress(rows, pe, w1, b1, w2)
    nc = blocks.shape[1]
    end = jnp.arange(nc) * CMP_STRIDE + CMP_BLOCK - 1
    mask = end[None, :] <= qpos[:, None]
    s = jnp.einsum('nqkgd,nckd->nkgqc', q.astype(jnp.float32), blocks[..., 0, :].astype(jnp.float32)) * ATTN_SCALE
    p = _masked_probs(s, mask)
    o = jnp.einsum('nkgqc,nckd->nqkgd', p, blocks[..., 1, :].astype(jnp.float32))
    return o, p


def _nsa_select(p, qpos, nsb):
    r_sh = CMP_BLOCK // CMP_STRIDE
    cps = SEL_BLOCK // CMP_STRIDE
    imp = jnp.sum(p, axis=2)
    pp = jnp.pad(imp, ((0, 0), (0, 0), (0, 0), (r_sh - 1, r_sh - 1)))
    nch = nsb * cps
    cs = sum(pp[..., r_sh - 1 - r:r_sh - 1 - r + nch] for r in range(r_sh))
    score = cs.reshape(cs.shape[:-1] + (nsb, cps)).sum(-1)
    j = jnp.arange(nsb)[None, :]
    cur = (qpos // SEL_BLOCK)[:, None]
    valid = j * SEL_BLOCK <= qpos[:, None]
    forced = (j == 0) | (j == cur) | (j == cur - 1)
    score = jnp.where(valid, jnp.where(forced, FORCE_SCORE, score), NEG_INF)
    _, idx = lax.top_k(score, min(SEL_TOP, nsb))
    return idx


def _nsa_selected_attend(q, kv_sel, idx, qpos):
    n, kv, tq, k = idx.shape
    kvs = kv_sel.reshape(n, kv, tq, k * SEL_BLOCK, 2, HEAD_DIM)
    kpos = (idx[..., None] * SEL_BLOCK + jnp.arange(SEL_BLOCK)).reshape(n, kv, tq, k * SEL_BLOCK)
    mask = (kpos <= qpos[:, None])[:, :, None]
    s = jnp.einsum('nqkgd,nkqsd->nkgqs', q.astype(jnp.float32), kvs[..., 0, :].astype(jnp.float32)) * ATTN_SCALE
    p = _masked_probs(s, mask)
    return jnp.einsum('nkgqs,nkqsd->nqkgd', p, kvs[..., 1, :].astype(jnp.float32))


def _nsa_merge(gates, o_cmp, o_slc, o_win, w_o, dtype):
    o = (gates[:, :, 0, ..., None] * o_cmp + gates[:, :, 1, ..., None] * o_slc
         + gates[:, :, 2, ..., None] * o_win)
    n, t = o.shape[:2]
    return o.reshape(n, t, A_Q).astype(dtype) @ w_o


def _nsa_prompt(h, pos, w_in, pe, w1, b1, w2, w_o):
    b, t, _ = h.shape
    q, q_rot, cmp_rows, slc_rows, win_rows, gates = _nsa_project(h, w_in, pos)
    o_cmp, p_cmp = _nsa_compressed(q, cmp_rows, pos, pe, w1, b1, w2)
    nsb = t // SEL_BLOCK
    idx = _nsa_select(p_cmp, pos, nsb)
    blocks = slc_rows.reshape(b, nsb, SEL_BLOCK, A_KV, 2, HEAD_DIM)
    bi = jnp.arange(b)[:, None, None, None]
    kvi = jnp.arange(A_KV)[None, :, None, None]

    def one(args):
        qi, ii, pi = args
        return _nsa_selected_attend(qi, blocks[bi, ii, :, kvi], ii, pi)

    nqb = t // SEL_Q_BLOCK
    qs = q_rot.reshape(b, nqb, SEL_Q_BLOCK, A_KV, A_G, HEAD_DIM).swapaxes(0, 1)
    iis = idx.reshape(b, A_KV, nqb, SEL_Q_BLOCK, idx.shape[-1]).transpose(2, 0, 1, 3, 4)
    o_slc = lax.map(one, (qs, iis, pos.reshape(nqb, SEL_Q_BLOCK)))
    o_slc = o_slc.swapaxes(0, 1).reshape(b, t, A_KV, A_G, HEAD_DIM)
    o_win = _banded_attend(q_rot, win_rows, A_WINDOW)
    y = _nsa_merge(gates, o_cmp, o_slc, o_win, w_o, h.dtype)
    return y, (cmp_rows, slc_rows, win_rows[:, -min(A_WINDOW, t):])


def _nsa_sample(h, pos, ia, cache_cmp, cache_slc, win_buf, page_table, w_in, pe, w1, b1, w2, w_o):
    n, t, _ = h.shape
    q, q_rot, cmp_new, slc_new, win_new, gates = _nsa_project(h, w_in, pos)
    nnb = -(-t // SEL_BLOCK)
    pad = ((0, 0), (0, nnb * SEL_BLOCK - t), (0, 0), (0, 0), (0, 0))
    past = cache_cmp[ia, page_table].reshape(n, PAST_LEN, A_KV, 2, HEAD_DIM)
    rows = jnp.concatenate([past, jnp.pad(cmp_new, pad)], axis=1)
    o_cmp, p_cmp = _nsa_compressed(q, rows, pos, pe, w1, b1, w2)
    npb = PAST_LEN // SEL_BLOCK
    idx = _nsa_select(p_cmp, pos, npb + nnb)
    bpp = PAGE_SIZE // SEL_BLOCK
    bi = jnp.arange(n)[:, None, None, None]
    kvi = jnp.arange(A_KV)[None, :, None, None]
    jp = jnp.minimum(idx, npb - 1)
    page = page_table[bi, jp // bpp]
    row = (jp % bpp)[..., None] * SEL_BLOCK + jnp.arange(SEL_BLOCK)
    from_pool = cache_slc[ia, page[..., None], row, kvi[..., None]]
    new_blocks = jnp.pad(slc_new, pad).reshape(n, nnb, SEL_BLOCK, A_KV, 2, HEAD_DIM)
    from_new = new_blocks[bi, jnp.clip(idx - npb, 0, nnb - 1), :, kvi]
    kv_sel = jnp.where((idx < npb)[..., None, None, None], from_pool, from_new)
    o_slc = _nsa_selected_attend(q_rot, kv_sel, idx, pos)
    o_win, new_buf = _window_attend_cached(q_rot, win_buf, win_new, pos, A_WINDOW)
    y = _nsa_merge(gates, o_cmp, o_slc, o_win, w_o, h.dtype)
    return y, (cmp_new, slc_new, new_buf)


def _swa(h, pos, buf, w_in, b_in, sinks, w_o, b_o):
    n, t, _ = h.shape
    z = h @ w_in + b_in
    q, k, v = _split_cols(z, [B_Q, B_KVW, B_KVW])
    q = _rope(q.reshape(n, t, B_HEADS, HEAD_DIM), pos).reshape(n, t, B_KV, B_G, HEAD_DIM)
    rows = jnp.stack([_rope(k.reshape(n, t, B_KV, HEAD_DIM), pos), v.reshape(n, t, B_KV, HEAD_DIM)], axis=3)
    sink = sinks.astype(jnp.float32).reshape(B_KV, B_G, 1, 1)
    if buf is None:
        o = _banded_attend(q, rows, B_WINDOW, sink)
        new = rows[:, -min(B_WINDOW, t):]
    else:
        o, new = _window_attend_cached(q, buf, rows, pos, B_WINDOW, sink)
    y = o.reshape(n, t, B_Q).astype(h.dtype) @ w_o + b_o
    return y, new


def _moe(h, router_w, router_b, w_gate, w_up, w_down):
    n, t, _ = h.shape
    probs = jax.nn.softmax(jnp.einsum('ntd,de->nte', h, router_w).astype(jnp.float32), axis=-1)
    biased = probs + router_b.astype(jnp.float32)
    grp = biased.reshape(n, t, N_GROUPS, EXPERTS_PER_GROUP)
    grp_score = jnp.sum(lax.top_k(grp, TOP_K)[0], axis=-1)
    g_sel = jnp.argmax(grp_score, axis=-1)
    in_grp = (jnp.arange(N_EXPERTS) // EXPERTS_PER_GROUP) == g_sel[..., None]
    _, top = lax.top_k(jnp.where(in_grp, biased, NEG_INF), TOP_K)
    w = jnp.take_along_axis(probs, top, axis=-1)
    w = w / jnp.sum(w, axis=-1, keepdims=True)
    gate = jnp.sum(jax.nn.one_hot(top, N_EXPERTS, dtype=jnp.float32) * w[..., None], axis=-2)
    a = jnp.einsum('ntd,edf->ntef', h, w_gate)
    u = jnp.einsum('ntd,edf->ntef', h, w_up)
    act = jax.nn.silu(a) * u * gate[..., None].astype(a.dtype)
    return jnp.einsum('ntef,efd->ntd', act, w_down)


def _trunk(x, c, past, nsa_w_in, nsa_cmp_pe, nsa_cmp_w1, nsa_cmp_b1, nsa_cmp_w2, nsa_w_o,
           swa_w_in, swa_b_in, swa_sinks, swa_w_o, swa_b_o, ada_w, ada_b, ln_g, ln_b,
           router_w, router_b, moe_w_gate, moe_w_up, moe_w_down):
    n, t, _ = x.shape
    pos = jnp.arange(t, dtype=jnp.int32) + (0 if past is None else PAST_LEN)
    cmp_out, slc_out, nwin_out, swin_out = [], [], [], []
    for l in range(DEPTH):
        ada = jax.nn.silu(c) @ ada_w[l] + ada_b[l]
        sh1, sc1, g1, sh2, sc2, g2 = jnp.split(ada, 6, axis=-1)
        h = x * (1.0 + sc1[:, None]) + sh1[:, None]
        i = l // N_MIXERS
        if l % N_MIXERS == 0:
            wts = (nsa_w_in[i], nsa_cmp_pe[i], nsa_cmp_w1[i], nsa_cmp_b1[i], nsa_cmp_w2[i], nsa_w_o[i])
            if past is None:
                y, (cr, sr, wr) = _nsa_prompt(h, pos, *wts)
            else:
                cache_cmp, cache_slc, cache_nwin, _, page_table = past
                y, (cr, sr, wr) = _nsa_sample(h, pos, i, cache_cmp, cache_slc, cache_nwin[i], page_table, *wts)
            cmp_out.append(cr)
            slc_out.append(sr)
            nwin_out.append(wr)
        else:
            buf = None if past is None else past[3][i]
            y, wr = _swa(h, pos, buf, swa_w_in[i], swa_b_in[i], swa_sinks[i], swa_w_o[i], swa_b_o[i])
            swin_out.append(wr)
        x = _layer_norm(DN_ALPHA * x + g1[:, None] * y, ln_g[l, 0], ln_b[l, 0])
        h = x * (1.0 + sc2[:, None]) + sh2[:, None]
        y = _moe(h, router_w, router_b, moe_w_gate[l], moe_w_up[l], moe_w_down[l])
        x = _layer_norm(DN_ALPHA * x + g2[:, None] * y, ln_g[l, 1], ln_b[l, 1])
    return x, (jnp.stack(cmp_out), jnp.stack(slc_out), jnp.stack(nwin_out), jnp.stack(swin_out))


def setup_inputs(seed: int = 0) -> dict:
    key = jax.random.key(seed)
    keys = iter(jax.random.split(key, 40))

    def nrm(shape, scale):
        return jax.random.normal(next(keys), shape, jnp.float32) * scale

    n_pages = PAST_LEN // PAGE_SIZE
    n_used = DEC_BATCH * n_pages
    n_pool = (5 * n_used + 3) // 4
    page_table = jax.random.permutation(next(keys), n_pool)[:n_used].reshape(DEC_BATCH, n_pages).astype(jnp.int32)
    wa = min(A_WINDOW, PAST_LEN)
    wb = min(B_WINDOW, PAST_LEN)
    return {
        'x_prompt': nrm((BATCH, SEQ, D_MODEL), 1.0),
        'x_sample': nrm((DEC_BATCH, DEC_SEQ, D_MODEL), 1.0),
        'cache_nsa_cmp': nrm((N_A, n_pool, PAGE_SIZE, A_KV, 2, HEAD_DIM), 1.0),
        'cache_nsa_slc': nrm((N_A, n_pool, PAGE_SIZE, A_KV, 2, HEAD_DIM), 1.0),
        'cache_nsa_win': nrm((N_A, DEC_BATCH, wa, A_KV, 2, HEAD_DIM), 1.0),
        'cache_swa_win': nrm((N_B, DEC_BATCH, wb, B_KV, 2, HEAD_DIM), 1.0),
        'page_table': page_table,
        'c_prompt': nrm((BATCH, D_MODEL), 1.0),
        'c_sample': nrm((DEC_BATCH, D_MODEL), 1.0),
        'nsa_w_in': nrm((N_A, D_MODEL, A_IN), D_MODEL ** -0.5),
        'nsa_cmp_pe': nrm((N_A, 2, CMP_BLOCK, HEAD_DIM), 0.1),
        'nsa_cmp_w1': nrm((N_A, 2, CMP_BLOCK * HEAD_DIM, CMP_HIDDEN), (CMP_BLOCK * HEAD_DIM) ** -0.5),
        'nsa_cmp_b1': nrm((N_A, 2, CMP_HIDDEN), 0.01),
        'nsa_cmp_w2': nrm((N_A, 2, CMP_HIDDEN, HEAD_DIM), CMP_HIDDEN ** -0.5),
        'nsa_w_o': nrm((N_A, A_Q, D_MODEL), A_Q ** -0.5 * DN_BETA),
        'swa_w_in': nrm((N_B, D_MODEL, B_IN), D_MODEL ** -0.5),
        'swa_b_in': nrm((N_B, B_IN), 0.01),
        'swa_sinks': nrm((N_B, B_HEADS), 0.5),
        'swa_w_o': nrm((N_B, B_Q, D_MODEL), B_Q ** -0.5 * DN_BETA),
        'swa_b_o': nrm((N_B, D_MODEL), 0.01),
        'ada_w': nrm((DEPTH, D_MODEL, 6 * D_MODEL), 0.5 * D_MODEL ** -0.5),
        'ada_b': nrm((DEPTH, 6 * D_MODEL), 0.01),
        'ln_g': 1.0 + nrm((DEPTH, 2, D_MODEL), 0.01),
        'ln_b': nrm((DEPTH, 2, D_MODEL), 0.01),
        'router_w': nrm((D_MODEL, N_EXPERTS), D_MODEL ** -0.5),
        'router_b': nrm((N_EXPERTS,), 0.01),
        'moe_w_gate': nrm((DEPTH, N_EXPERTS, D_MODEL, D_FF_EXPERT), D_MODEL ** -0.5),
        'moe_w_up': nrm((DEPTH, N_EXPERTS, D_MODEL, D_FF_EXPERT), D_MODEL ** -0.5),
        'moe_w_down': nrm((DEPTH, N_EXPERTS, D_FF_EXPERT, D_MODEL), D_FF_EXPERT ** -0.5 * DN_BETA),
    }


def reference(x_prompt, x_sample, cache_nsa_cmp, cache_nsa_slc, cache_nsa_win, cache_swa_win, page_table,
              c_prompt, c_sample, nsa_w_in, nsa_cmp_pe, nsa_cmp_w1, nsa_cmp_b1, nsa_cmp_w2, nsa_w_o,
              swa_w_in, swa_b_in, swa_sinks, swa_w_o, swa_b_o, ada_w, ada_b, ln_g, ln_b,
              router_w, router_b, moe_w_gate, moe_w_up, moe_w_down):
    weights = (nsa_w_in, nsa_cmp_pe, nsa_cmp_w1, nsa_cmp_b1, nsa_cmp_w2, nsa_w_o,
               swa_w_in, swa_b_in, swa_sinks, swa_w_o, swa_b_o, ada_w, ada_b, ln_g, ln_b,
               router_w, router_b, moe_w_gate, moe_w_up, moe_w_down)
    y_prompt, (cmp_p, slc_p, nwin_p, swin_p) = _trunk(x_prompt, c_prompt, None, *weights)
    past = (cache_nsa_cmp, cache_nsa_slc, cache_nsa_win, cache_swa_win, page_table)
    y_sample, (cmp_s, slc_s, nwin_s, swin_s) = _trunk(x_sample, c_sample, past, *weights)
    return (y_prompt, y_sample, cmp_p, slc_p, nwin_p, swin_p, cmp_s, slc_s, nwin_s, swin_s)
```

```python
import functools
import math

import numpy as np
import jax
import jax.numpy as jnp
from jax import lax
from jax.experimental import pallas as pl
from jax.experimental.pallas import tpu as pltpu

D_MODEL = 1024
SEQ = 2048
DEPTH = 4
DEC_SEQ = 8
PAST_LEN = 8192
PAGE_SIZE = 128
N_PAGES = PAST_LEN // PAGE_SIZE

HEAD_DIM = 64
ROT_DIM = HEAD_DIM // 4
ROT_HALF = ROT_DIM // 2
ROPE_THETA = 500000.0
ATTN_SCALE = HEAD_DIM ** -0.5

N_HEADS = D_MODEL // HEAD_DIM
N_KV = 2
N_G = N_HEADS // N_KV
KV_ROW = N_KV * 2 * HEAD_DIM
CMP_BLOCK = 32
CMP_STRIDE = 16
CMP_HIDDEN = 4 * HEAD_DIM
CMP_SHIFTS = CMP_BLOCK // CMP_STRIDE
CHUNK_W = CMP_STRIDE * HEAD_DIM
SEL_BLOCK = 64
SEL_TOP = 16
CHUNKS_PER_SEL = SEL_BLOCK // CMP_STRIDE
A_WINDOW = 512
B_WINDOW = 128
FORCE_SCORE = 1e9

N_EXPERTS = 16
N_GROUPS = 4
EXPERTS_PER_GROUP = N_EXPERTS // N_GROUPS
D_FF_EXPERT = D_MODEL // 2

DN_ALPHA = (2.0 * DEPTH) ** 0.25
LN_EPS = 1e-5
NEG_INF = -1e30
TINY = 1e-30

LANES = 128
A_COLS = 2048
A_CMP0, A_SLC0, A_WIN0, A_GATE0 = 1024, 1280, 1536, 1792
B_COLS = D_MODEL + KV_ROW
VMEM_LIMIT = 56 * 1024 * 1024

F32 = jnp.float32
BF16 = jnp.bfloat16


def _cparams(sem):
    return pltpu.CompilerParams(dimension_semantics=sem, vmem_limit_bytes=VMEM_LIMIT)


def _dot(a, b):
    return jnp.dot(a, b, preferred_element_type=F32)


def _dot_nt(a, b, precision=None):
    return lax.dot_general(a, b, (((1,), (1,)), ((), ())), precision=precision,
                           preferred_element_type=F32)


def _masked_probs(s, mask):
    s = jnp.where(mask, s, NEG_INF)
    m = jnp.max(s, axis=-1, keepdims=True)
    e = jnp.where(mask, jnp.exp(s - m), 0.0)
    den = jnp.sum(e, axis=-1, keepdims=True)
    return e / jnp.maximum(den, TINY)


def _layer_norm(r, g, b):
    mu = jnp.mean(r, axis=-1, keepdims=True)
    rc = r - mu
    var = jnp.mean(rc * rc, axis=-1, keepdims=True)
    return rc * lax.rsqrt(var + LN_EPS) * g + b


def _ada_kernel(c_ref, w_ref, b_ref, o_ref):
    c = c_ref[...]
    a = (c * jax.nn.sigmoid(c)).astype(BF16)
    o_ref[0] = _dot(a, w_ref[0].astype(BF16)) + b_ref[0]


def _ada(c_all, ada_w, ada_b):
    m = c_all.shape[0]
    n = ada_w.shape[-1]
    tn = 1536
    return pl.pallas_call(
        _ada_kernel,
        grid=(DEPTH, n // tn),
        in_specs=[pl.BlockSpec((m, D_MODEL), lambda l, j: (0, 0)),
                  pl.BlockSpec((1, D_MODEL, tn), lambda l, j: (l, 0, j)),
                  pl.BlockSpec((1, 1, tn), lambda l, j: (l, 0, j))],
        out_specs=pl.BlockSpec((1, m, tn), lambda l, j: (l, 0, j)),
        out_shape=jax.ShapeDtypeStruct((DEPTH, m, n), F32),
        compiler_params=_cparams(("arbitrary", "arbitrary")),
        name="ada",
    )(c_all, ada_w, ada_b.reshape(DEPTH, 1, n))


def _pe_term_kernel(pe_ref, w_ref, o_ref):
    o_ref[0] = _dot(pe_ref[0].astype(BF16), w_ref[0].astype(BF16))


def _pe_term(pe, w1):
    k = CMP_BLOCK * HEAD_DIM
    pe8 = jnp.broadcast_to(pe.reshape(2, 1, k), (2, 8, k))
    out = pl.pallas_call(
        _pe_term_kernel,
        grid=(2,),
        in_specs=[pl.BlockSpec((1, 8, k), lambda i: (i, 0, 0)),
                  pl.BlockSpec((1, k, CMP_HIDDEN), lambda i: (i, 0, 0))],
        out_specs=pl.BlockSpec((1, 8, CMP_HIDDEN), lambda i: (i, 0, 0)),
        out_shape=jax.ShapeDtypeStruct((2, 8, CMP_HIDDEN), F32),
        compiler_params=_cparams(("arbitrary",)),
        name="pe_term",
    )(pe8, w1)
    return out[:, 0]


def _rope_tables(pos):
    inv = jnp.exp(-math.log(ROPE_THETA) * jnp.arange(ROT_HALF, dtype=F32) / ROT_HALF)
    ang = pos.astype(F32)[:, None] * inv[None, :]
    cos, sin = jnp.cos(ang), jnp.sin(ang)
    t = pos.shape[0]
    one = jnp.ones((t, HEAD_DIM - ROT_DIM), F32)
    zero = jnp.zeros((t, HEAD_DIM - ROT_DIM), F32)
    zh = jnp.zeros((t, ROT_HALF), F32)
    cos_h = jnp.concatenate([cos, cos, one], axis=1)
    sa_h = jnp.concatenate([-sin, zh, zero], axis=1)
    sb_h = jnp.concatenate([zh, sin, zero], axis=1)
    id_h = jnp.ones((t, HEAD_DIM), F32)
    z_h = jnp.zeros((t, HEAD_DIM), F32)
    return jnp.concatenate([cos_h, cos_h, sa_h, sa_h, sb_h, sb_h,
                            cos_h, id_h, sa_h, z_h, sb_h, z_h], axis=1)


def _rope_apply(seg, cos, sa, sb):
    w = seg.shape[1]
    reps = w // LANES
    c = jnp.tile(cos, (1, reps))
    a = jnp.tile(sa, (1, reps))
    b = jnp.tile(sb, (1, reps))
    return seg * c + pltpu.roll(seg, w - ROT_HALF, 1) * a + pltpu.roll(seg, ROT_HALF, 1) * b


def _proj_kernel(x_ref, sc_ref, sh_ref, w_ref, b_ref, tab_ref, z_ref, *qrot_ref, kv_segs):
    x = x_ref[0]
    h = x * (1.0 + sc_ref[0]) + sh_ref[0]
    z = _dot(h.astype(BF16), w_ref[...]) + b_ref[...]
    tab = tab_ref[0]
    tq = [tab[:, i * LANES:(i + 1) * LANES] for i in range(3)]
    tk = [tab[:, i * LANES:(i + 1) * LANES] for i in range(3, 6)]
    q_rot = _rope_apply(z[:, :D_MODEL], *tq)
    z_ref[0] = z
    if qrot_ref:
        qrot_ref[0][0] = q_rot
    else:
        z_ref[0, :, :D_MODEL] = q_rot
    for c0 in kv_segs:
        z_ref[0, :, c0:c0 + KV_ROW] = _rope_apply(z[:, c0:c0 + KV_ROW], *tk)


def _proj(x, sc, sh, w, b, tab, tm, kv_segs, sep_qrot):
    g, t, _ = x.shape
    n = w.shape[1]
    tmod = tm if sc.shape[1] == t else 1
    out_shape = [jax.ShapeDtypeStruct((g, t, n), F32)]
    out_specs = [pl.BlockSpec((1, tm, n), lambda a, i: (a, i, 0))]
    if sep_qrot:
        out_shape.append(jax.ShapeDtypeStruct((g, t, D_MODEL), F32))
        out_specs.append(pl.BlockSpec((1, tm, D_MODEL), lambda a, i: (a, i, 0)))
    mod_spec = pl.BlockSpec((1, tmod, D_MODEL), (lambda a, i: (a, i, 0)) if tmod == tm else (lambda a, i: (a, 0, 0)))
    return pl.pallas_call(
        functools.partial(_proj_kernel, kv_segs=kv_segs),
        grid=(g, t // tm),
        in_specs=[pl.BlockSpec((1, tm, D_MODEL), lambda a, i: (a, i, 0)),
                  mod_spec, mod_spec,
                  pl.BlockSpec((D_MODEL, n), lambda a, i: (0, 0)),
                  pl.BlockSpec((1, n), lambda a, i: (0, 0)),
                  pl.BlockSpec((1, tm, 6 * LANES), lambda a, i: (0, i, 0))],
        out_specs=out_specs,
        out_shape=out_shape,
        compiler_params=_cparams(("arbitrary", "arbitrary")),
        name="proj",
    )(x, sc, sh, w, b, tab)


def _outproj_kernel(o_ref, x_ref, g_ref, w_ref, b_ref, lg_ref, lb_ref, out_ref):
    y = _dot(o_ref[0].astype(BF16), w_ref[...]) + b_ref[...]
    r = DN_ALPHA * x_ref[0] + g_ref[0] * y
    out_ref[0] = _layer_norm(r, lg_ref[...], lb_ref[...])


def _outproj_ln(o, x, gate, w, b, lg, lb, tm):
    g, t, _ = x.shape
    tmod = tm if gate.shape[1] == t else 1
    row = pl.BlockSpec((1, tm, D_MODEL), lambda a, i: (a, i, 0))
    vec = pl.BlockSpec((1, D_MODEL), lambda a, i: (0, 0))
    mod_spec = pl.BlockSpec((1, tmod, D_MODEL), (lambda a, i: (a, i, 0)) if tmod == tm else (lambda a, i: (a, 0, 0)))
    return pl.pallas_call(
        _outproj_kernel,
        grid=(g, t // tm),
        in_specs=[row, row, mod_spec,
                  pl.BlockSpec((D_MODEL, D_MODEL), lambda a, i: (0, 0)), vec, vec, vec],
        out_specs=row,
        out_shape=jax.ShapeDtypeStruct((g, t, D_MODEL), F32),
        compiler_params=_cparams(("arbitrary", "arbitrary")),
        name="outproj_ln",
    )(o, x, gate, w, b, lg, lb)


def _router_gates_t(h, rwt_ref, rb_ref):
    logits = _dot_nt(rwt_ref[...], h, precision=lax.Precision.HIGHEST)
    m = jnp.max(logits, axis=0, keepdims=True)
    ex = jnp.exp(logits - m)
    probs = ex / jnp.sum(ex, axis=0, keepdims=True)
    biased = probs + rb_ref[...]
    eidx = lax.broadcasted_iota(jnp.int32, biased.shape, 0)
    best = None
    g_sel = None
    for g in range(N_GROUPS):
        rows = [biased[g * EXPERTS_PER_GROUP + i:g * EXPERTS_PER_GROUP + i + 1] for i in range(EXPERTS_PER_GROUP)]
        gs = None
        for i in range(EXPERTS_PER_GROUP):
            for j in range(i + 1, EXPERTS_PER_GROUP):
                pair = rows[i] + rows[j]
                gs = pair if gs is None else jnp.maximum(gs, pair)
        if best is None:
            best, g_sel = gs, jnp.zeros(gs.shape, jnp.int32)
        else:
            upd = gs > best
            g_sel = jnp.where(upd, g, g_sel)
            best = jnp.maximum(best, gs)
    v = jnp.where(eidx // EXPERTS_PER_GROUP == g_sel, biased, NEG_INF)
    m1 = jnp.max(v, axis=0, keepdims=True)
    i1 = jnp.min(jnp.where(v == m1, eidx, N_EXPERTS), axis=0, keepdims=True)
    v2 = jnp.where(eidx == i1, -jnp.inf, v)
    m2 = jnp.max(v2, axis=0, keepdims=True)
    i2 = jnp.min(jnp.where(v2 == m2, eidx, N_EXPERTS), axis=0, keepdims=True)
    w1 = jnp.sum(jnp.where(eidx == i1, probs, 0.0), axis=0, keepdims=True)
    w2 = jnp.sum(jnp.where(eidx == i2, probs, 0.0), axis=0, keepdims=True)
    tot = w1 + w2
    return jnp.where(eidx == i1, w1 / tot, 0.0) + jnp.where(eidx == i2, w2 / tot, 0.0)


def _moe_kernel(x_ref, sc_ref, sh_ref, g_ref, rwt_ref, rb_ref, wg_ref, wu_ref, wd_ref, lg_ref, lb_ref,
                out_ref, h_scr, gate_scr, acc_scr):
    e = pl.program_id(2)

    @pl.when(e == 0)
    def _():
        h = x_ref[0] * (1.0 + sc_ref[0]) + sh_ref[0]
        h_scr[...] = h.astype(BF16)
        gt = _router_gates_t(h, rwt_ref, rb_ref)
        pad = jnp.zeros((LANES - N_EXPERTS, gt.shape[1]), F32)
        gate_scr[...] = jnp.concatenate([gt, pad], axis=0).T
        acc_scr[...] = jnp.zeros_like(acc_scr)

    h = h_scr[...]
    a = _dot(h, wg_ref[0])
    u = _dot(h, wu_ref[0])
    gates = gate_scr[...]
    lane = lax.broadcasted_iota(jnp.int32, gates.shape, 1)
    gcol = jnp.sum(jnp.where(lane == e, gates, 0.0), axis=1, keepdims=True)
    act = (a * jax.nn.sigmoid(a)) * u * gcol
    acc_scr[...] += _dot(act.astype(BF16), wd_ref[0])

    @pl.when(e == N_EXPERTS - 1)
    def _():
        r = DN_ALPHA * x_ref[0] + g_ref[0] * acc_scr[...]
        out_ref[0] = _layer_norm(r, lg_ref[...], lb_ref[...])


def _moe_ln(x, sc, sh, gate, rwt, rb, wg, wu, wd, lg, lb, tm):
    g, t, _ = x.shape
    tmod = tm if sc.shape[1] == t else 1
    row = pl.BlockSpec((1, tm, D_MODEL), lambda a, i, e: (a, i, 0))
    vec = pl.BlockSpec((1, D_MODEL), lambda a, i, e: (0, 0))
    mod_spec = pl.BlockSpec((1, tmod, D_MODEL),
                            (lambda a, i, e: (a, i, 0)) if tmod == tm else (lambda a, i, e: (a, 0, 0)))
    return pl.pallas_call(
        _moe_kernel,
        grid=(g, t // tm, N_EXPERTS),
        in_specs=[row, mod_spec, mod_spec, mod_spec,
                  pl.BlockSpec((N_EXPERTS, D_MODEL), lambda a, i, e: (0, 0)),
                  pl.BlockSpec((N_EXPERTS, 1), lambda a, i, e: (0, 0)),
                  pl.BlockSpec((1, D_MODEL, D_FF_EXPERT), lambda a, i, e: (e, 0, 0)),
                  pl.BlockSpec((1, D_MODEL, D_FF_EXPERT), lambda a, i, e: (e, 0, 0)),
                  pl.BlockSpec((1, D_FF_EXPERT, D_MODEL), lambda a, i, e: (e, 0, 0)),
                  vec, vec],
        out_specs=row,
        out_shape=jax.ShapeDtypeStruct((g, t, D_MODEL), F32),
        scratch_shapes=[pltpu.VMEM((tm, D_MODEL), BF16),
                        pltpu.VMEM((tm, LANES), F32),
                        pltpu.VMEM((tm, D_MODEL), F32)],
        compiler_params=_cparams(("arbitrary", "arbitrary", "arbitrary")),
        name="moe_ln",
    )(x, sc, sh, gate, rwt, rb, wg, wu, wd, lg, lb)


def _compress_blocks(chunks, w1r, pb, w2):
    nch = chunks.shape[0]
    part = _dot(chunks, w1r)
    pre = part[:, :CMP_HIDDEN] + pltpu.roll(part[:, CMP_HIDDEN:], nch - 1, 0)
    hid = jax.nn.gelu(pre + pb)
    return _dot(hid.astype(BF16), w2)


def _compress_kernel(ch_ref, w1r_ref, pb_ref, w2_ref, o_ref):
    for j in range(2 * N_KV):
        i = j % 2
        o_ref[0, j] = _compress_blocks(ch_ref[0, j].astype(BF16), w1r_ref[i], pb_ref[i], w2_ref[i])


def _compress_prompt(chunks, w1r, pb, w2):
    b, _, nch, _ = chunks.shape
    return pl.pallas_call(
        _compress_kernel,
        grid=(b,),
        in_specs=[pl.BlockSpec((1, 2 * N_KV, nch, CHUNK_W), lambda a: (a, 0, 0, 0)),
                  pl.BlockSpec((2, CHUNK_W, 2 * CMP_HIDDEN), lambda a: (0, 0, 0)),
                  pl.BlockSpec((2, 1, CMP_HIDDEN), lambda a: (0, 0, 0)),
                  pl.BlockSpec((2, CMP_HIDDEN, HEAD_DIM), lambda a: (0, 0, 0))],
        out_specs=pl.BlockSpec((1, 2 * N_KV, nch, HEAD_DIM), lambda a: (a, 0, 0, 0)),
        out_shape=jax.ShapeDtypeStruct((b, 2 * N_KV, nch, HEAD_DIM), F32),
        compiler_params=_cparams(("arbitrary",)),
        name="compress_prompt",
    )(chunks, w1r, pb, w2)


def _stack_heads(q):
    return jnp.concatenate([q[:, g * HEAD_DIM:(g + 1) * HEAD_DIM] for g in range(N_G)], axis=0)


def _chunk_score_matrix_t(n_sel, n_cmp):
    j = lax.broadcasted_iota(jnp.int32, (n_sel, n_cmp), 0)
    c = lax.broadcasted_iota(jnp.int32, (n_sel, n_cmp), 1)
    lo = j * CHUNKS_PER_SEL
    hi = lo + CHUNKS_PER_SEL - 1
    m = jnp.zeros((n_sel, n_cmp), F32)
    for r in range(CMP_SHIFTS):
        m = m + jnp.where((c + r >= lo) & (c + r <= hi), 1.0, 0.0)
    return m


def _nsa_prompt_kernel(q_ref, qr_ref, cb_ref, slc_ref, win_ref, gate_ref, o_ref, *, tq, kc):
    qi = pl.program_id(2)
    t0 = qi * tq
    n_cmp = cb_ref.shape[2]
    n_sel = SEQ // SEL_BLOCK
    tpos = t0 + lax.broadcasted_iota(jnp.int32, (tq, 1), 0)

    qs = _stack_heads(q_ref[0]).astype(BF16)
    k_c = cb_ref[0, 0].astype(BF16)
    v_c = cb_ref[0, 1].astype(BF16)
    s = (_dot_nt(qs, k_c) * ATTN_SCALE).reshape(N_G, tq, n_cmp)
    cend = lax.broadcasted_iota(jnp.int32, (tq, n_cmp), 1) * CMP_STRIDE + CMP_BLOCK - 1
    p = _masked_probs(s, (cend <= tpos)[None])
    o_cmp = _dot(p.reshape(N_G * tq, n_cmp).astype(BF16), v_c)
    imp = jnp.sum(p, axis=0)

    score_t = _dot_nt(_chunk_score_matrix_t(n_sel, n_cmp), imp, precision=lax.Precision.HIGHEST)
    jj = lax.broadcasted_iota(jnp.int32, (n_sel, tq), 0)
    tt = t0 + lax.broadcasted_iota(jnp.int32, (n_sel, tq), 1)
    cur = tt // SEL_BLOCK
    forced = (jj == 0) | (jj == cur) | (jj == cur - 1)
    sc = jnp.where(jj * SEL_BLOCK <= tt, jnp.where(forced, FORCE_SCORE, score_t), NEG_INF)
    rank = jnp.zeros((n_sel, tq), F32)
    for i in range(n_sel):
        ri = sc[i:i + 1]
        rank = rank + jnp.where(ri > sc, 1.0, jnp.where((ri == sc) & (jj > i), 1.0, 0.0))
    sel_t = jnp.where(rank < SEL_TOP, 1.0, 0.0)
    sel = jnp.concatenate([sel_t, jnp.zeros((LANES - n_sel, tq), F32)], axis=0).T.astype(BF16)

    qrs = _stack_heads(qr_ref[0]).astype(BF16)

    def body(c, carry):
        m, l, acc = carry
        k0 = pl.multiple_of(c * kc, kc)
        rows = slc_ref[0, pl.ds(k0, kc), :]
        k = rows[:, :HEAD_DIM].astype(BF16)
        v = rows[:, HEAD_DIM:].astype(BF16)
        sk = (_dot_nt(qrs, k) * ATTN_SCALE).reshape(N_G, tq, kc)
        kpos = k0 + lax.broadcasted_iota(jnp.int32, (LANES, kc), 1)
        expand = jnp.where(kpos // SEL_BLOCK == lax.broadcasted_iota(jnp.int32, (LANES, kc), 0), 1.0, 0.0)
        chosen = _dot(sel, expand.astype(BF16))
        mask = ((chosen > 0.5) & (kpos[:1] <= tpos))[None]
        sm = jnp.where(mask, sk, NEG_INF)
        m_new = jnp.maximum(m, jnp.max(sm, axis=-1, keepdims=True))
        alpha = jnp.exp(m - m_new)
        e = jnp.where(mask, jnp.exp(sm - m_new), 0.0)
        l = alpha * l + jnp.sum(e, axis=-1, keepdims=True)
        pv = _dot(e.reshape(N_G * tq, kc).astype(BF16), v).reshape(N_G, tq, HEAD_DIM)
        return m_new, l, alpha * acc + pv

    init = (jnp.full((N_G, tq, 1), NEG_INF, F32), jnp.zeros((N_G, tq, 1), F32),
            jnp.zeros((N_G, tq, HEAD_DIM), F32))
    _, l, acc = lax.fori_loop(0, (t0 + tq + kc - 1) // kc, body, init)
    o_slc = acc / jnp.maximum(l, TINY)

    o_win = _band_attend(qrs, win_ref, t0, tq, A_WINDOW, None)

    gt = jax.nn.sigmoid(gate_ref[0])
    o_cmp = o_cmp.reshape(N_G, tq, HEAD_DIM)
    for g in range(N_G):
        og = (gt[:, g:g + 1] * o_cmp[g] + gt[:, N_G + g:N_G + g + 1] * o_slc[g]
              + gt[:, 2 * N_G + g:2 * N_G + g + 1] * o_win[g])
        o_ref[0, :, g * HEAD_DIM:(g + 1) * HEAD_DIM] = og


def _band_attend(qrs, rows_ref, t0, tq, window, sinks):
    span = window + tq
    start = pl.multiple_of(jnp.maximum(t0 - window, 0), tq)
    rows = rows_ref[0, pl.ds(start, span), :]
    k = rows[:, :HEAD_DIM].astype(BF16)
    v = rows[:, HEAD_DIM:].astype(BF16)
    s = (_dot_nt(qrs, k) * ATTN_SCALE).reshape(N_G, tq, span)
    d = (t0 + lax.broadcasted_iota(jnp.int32, (tq, span), 0)) - (start + lax.broadcasted_iota(jnp.int32, (tq, span), 1))
    mask = ((d >= 0) & (d <= window))[None]
    if sinks is None:
        p = _masked_probs(s, mask)
    else:
        sink = jnp.concatenate([jnp.full((1, 1, 1), sk, F32) for sk in sinks], axis=0)
        sm = jnp.where(mask, s, NEG_INF)
        m = jnp.maximum(jnp.max(sm, axis=-1, keepdims=True), sink)
        e = jnp.where(mask, jnp.exp(sm - m), 0.0)
        den = jnp.sum(e, axis=-1, keepdims=True) + jnp.exp(sink - m)
        p = e / jnp.maximum(den, TINY)
    return _dot(p.reshape(N_G * tq, span).astype(BF16), v).reshape(N_G, tq, HEAD_DIM)


def _nsa_prompt_attend(z, q_rot, cblocks, tq=128, kc=512):
    b, t, _ = z.shape
    n_cmp = cblocks.shape[2]
    hw = N_G * HEAD_DIM
    blk = lambda base: pl.BlockSpec((1, t, LANES), lambda a, k, i: (a, 0, base // LANES + k))
    return pl.pallas_call(
        functools.partial(_nsa_prompt_kernel, tq=tq, kc=kc),
        grid=(b, N_KV, t // tq),
        in_specs=[pl.BlockSpec((1, tq, hw), lambda a, k, i: (a, i, k)),
                  pl.BlockSpec((1, tq, hw), lambda a, k, i: (a, i, k)),
                  pl.BlockSpec((1, 2, n_cmp, HEAD_DIM), lambda a, k, i: (a, k, 0, 0)),
                  blk(A_SLC0), blk(A_WIN0),
                  pl.BlockSpec((1, tq, LANES), lambda a, k, i: (a, i, A_GATE0 // LANES + k))],
        out_specs=pl.BlockSpec((1, tq, hw), lambda a, k, i: (a, i, k)),
        out_shape=jax.ShapeDtypeStruct((b, t, D_MODEL), F32),
        compiler_params=_cparams(("arbitrary", "arbitrary", "arbitrary")),
        name="nsa_prompt_attend",
    )(z, q_rot, cblocks, z, z, z)


def _swa_prompt_kernel(sink_ref, q_ref, rows_ref, o_ref, *, tq):
    kv = pl.program_id(1)
    t0 = pl.program_id(2) * tq
    qrs = _stack_heads(q_ref[0]).astype(BF16)
    sinks = [sink_ref[kv, g] for g in range(N_G)]
    o = _band_attend(qrs, rows_ref, t0, tq, B_WINDOW, sinks)
    for g in range(N_G):
        o_ref[0, :, g * HEAD_DIM:(g + 1) * HEAD_DIM] = o[g]


def _swa_prompt_attend(z, sinks, tq=128):
    b, t, _ = z.shape
    hw = N_G * HEAD_DIM
    return pl.pallas_call(
        functools.partial(_swa_prompt_kernel, tq=tq),
        grid=(b, N_KV, t // tq),
        in_specs=[pl.BlockSpec(memory_space=pltpu.SMEM),
                  pl.BlockSpec((1, tq, hw), lambda a, k, i: (a, i, k)),
                  pl.BlockSpec((1, t, LANES), lambda a, k, i: (a, 0, D_MODEL // LANES + k))],
        out_specs=pl.BlockSpec((1, tq, hw), lambda a, k, i: (a, i, k)),
        out_shape=jax.ShapeDtypeStruct((b, t, D_MODEL), F32),
        compiler_params=_cparams(("arbitrary", "arbitrary", "arbitrary")),
        name="swa_prompt_attend",
    )(sinks.reshape(N_KV, N_G), z, z)


S_CHUNKS = 520
S_ROWS = S_CHUNKS * CMP_STRIDE
S_SEL = 256
N_SEL_SAMPLE = PAST_LEN // SEL_BLOCK + 1


def _page_copies(cache_hbm, ia, pt_ref, n, p, buf, slot, sem, split):
    page = cache_hbm.at[ia, pt_ref[n, p]]
    rows = pl.ds(p * PAGE_SIZE, PAGE_SIZE)
    if not split:
        return [pltpu.make_async_copy(page, buf.at[slot, rows], sem.at[slot])]
    return [pltpu.make_async_copy(page.at[:, pl.ds(kv * LANES, LANES)], buf.at[slot, kv, rows], sem.at[slot])
            for kv in range(N_KV)]


def _gather_pages(cache_hbm, ia, pt_ref, buf, sem, split):
    n = pl.program_id(0)
    nn = pl.num_programs(0)
    slot = n % 2

    def start(seq, sl):
        def go(p, c):
            for cp in _page_copies(cache_hbm, ia, pt_ref, seq, p, buf, sl, sem, split):
                cp.start()
            return c
        lax.fori_loop(0, N_PAGES, go, 0)

    @pl.when(n == 0)
    def _():
        start(0, 0)

    @pl.when(n + 1 < nn)
    def _():
        start(n + 1, 1 - slot)

    def wait(p, c):
        for cp in _page_copies(cache_hbm, ia, pt_ref, n, p, buf, slot, sem, split):
            cp.wait()
        return c
    lax.fori_loop(0, N_PAGES, wait, 0)
    return slot


def _sample_q_rows(q, kv):
    return jnp.concatenate([q[:, (kv * N_G + g) * HEAD_DIM:(kv * N_G + g + 1) * HEAD_DIM] for g in range(N_G)], axis=0)


def _sample_q_blockdiag(q):
    parts = []
    for kv in range(N_KV):
        qs = _sample_q_rows(q, kv)
        left = jnp.zeros((qs.shape[0], kv * 2 * HEAD_DIM), F32)
        right = jnp.zeros((qs.shape[0], KV_ROW - kv * 2 * HEAD_DIM - HEAD_DIM), F32)
        parts.append(jnp.concatenate([left, qs, right], axis=1) if kv else jnp.concatenate([qs, right], axis=1))
    return jnp.concatenate(parts, axis=0)


def _nsa_sample_cmp_kernel(pt_ref, new_ref, q_ref, w1r_ref, pb_ref, w2_ref, cache_hbm,
                           ocmp_ref, sel_ref, buf, chunks, sem, *, ia):
    slot = _gather_pages(cache_hbm, ia, pt_ref, buf, sem, True)
    new_rows = new_ref[0]
    for kv in range(N_KV):
        buf[slot, kv, PAST_LEN:PAST_LEN + DEC_SEQ, :] = new_rows[:, kv * LANES:(kv + 1) * LANES]
        buf[slot, kv, PAST_LEN + DEC_SEQ:, :] = jnp.zeros((S_ROWS - PAST_LEN - DEC_SEQ, LANES), F32)

    lo = lax.broadcasted_iota(jnp.int32, (S_CHUNKS, LANES), 1) < HEAD_DIM
    for k in range(CMP_STRIDE // 2):
        for kv in range(N_KV):
            a = buf[slot, kv, pl.ds(2 * k, S_CHUNKS, stride=CMP_STRIDE), :]
            b = buf[slot, kv, pl.ds(2 * k + 1, S_CHUNKS, stride=CMP_STRIDE), :]
            chunks[2 * kv, :, k * LANES:(k + 1) * LANES] = jnp.where(lo, a, pltpu.roll(b, HEAD_DIM, 1)).astype(BF16)
            chunks[2 * kv + 1, :, k * LANES:(k + 1) * LANES] = jnp.where(lo, pltpu.roll(a, HEAD_DIM, 1), b).astype(BF16)

    q = q_ref[0]
    qpos = PAST_LEN + lax.broadcasted_iota(jnp.int32, (DEC_SEQ, 1), 0)
    cend = lax.broadcasted_iota(jnp.int32, (DEC_SEQ, S_CHUNKS), 1) * CMP_STRIDE + CMP_BLOCK - 1
    cmask = (cend <= qpos)[None]
    imps = []
    for kv in range(N_KV):
        k_c = _compress_blocks(chunks[2 * kv], w1r_ref[0], pb_ref[0], w2_ref[0]).astype(BF16)
        v_c = _compress_blocks(chunks[2 * kv + 1], w1r_ref[1], pb_ref[1], w2_ref[1]).astype(BF16)
        qs = _sample_q_rows(q, kv).astype(BF16)
        s = (_dot_nt(qs, k_c) * ATTN_SCALE).reshape(N_G, DEC_SEQ, S_CHUNKS)
        p = _masked_probs(s, cmask)
        o = _dot(p.reshape(N_G * DEC_SEQ, S_CHUNKS).astype(BF16), v_c)
        for g in range(N_G):
            h = kv * N_G + g
            ocmp_ref[0, :, h * HEAD_DIM:(h + 1) * HEAD_DIM] = o[g * DEC_SEQ:(g + 1) * DEC_SEQ]
        imps.append(jnp.sum(p, axis=0))
    imp = jnp.concatenate(imps, axis=0)

    score = _dot_nt(imp, _chunk_score_matrix_t(S_SEL, S_CHUNKS), precision=lax.Precision.HIGHEST)
    shape = score.shape
    jj = lax.broadcasted_iota(jnp.int32, shape, 1)
    tt = PAST_LEN + lax.broadcasted_iota(jnp.int32, shape, 0) % DEC_SEQ
    cur = tt // SEL_BLOCK
    forced = (jj == 0) | (jj == cur) | (jj == cur - 1)
    sc = jnp.where(jj * SEL_BLOCK <= tt, jnp.where(forced, FORCE_SCORE, score), NEG_INF)
    sc = jnp.where(jj < N_SEL_SAMPLE, sc, -jnp.inf)

    def rank_step(k, rank):
        r = pltpu.roll(sc, k, 1)
        return rank + jnp.where(r > sc, 1.0, jnp.where((r == sc) & (jj >= k), 1.0, 0.0))

    rank = lax.fori_loop(1, S_SEL, rank_step, jnp.zeros(shape, F32))
    sel_ref[0] = jnp.where(rank < SEL_TOP, 1.0, 0.0)


def _nsa_sample_cmp(ia, page_table, cache, new_rows, q, w1r, pb, w2):
    n = q.shape[0]
    grid_spec = pltpu.PrefetchScalarGridSpec(
        num_scalar_prefetch=1,
        grid=(n,),
        in_specs=[pl.BlockSpec((1, DEC_SEQ, KV_ROW), lambda a, pt: (a, 0, 0)),
                  pl.BlockSpec((1, DEC_SEQ, D_MODEL), lambda a, pt: (a, 0, 0)),
                  pl.BlockSpec((2, CHUNK_W, 2 * CMP_HIDDEN), lambda a, pt: (0, 0, 0)),
                  pl.BlockSpec((2, 1, CMP_HIDDEN), lambda a, pt: (0, 0, 0)),
                  pl.BlockSpec((2, CMP_HIDDEN, HEAD_DIM), lambda a, pt: (0, 0, 0)),
                  pl.BlockSpec(memory_space=pl.ANY)],
        out_specs=[pl.BlockSpec((1, DEC_SEQ, D_MODEL), lambda a, pt: (a, 0, 0)),
                   pl.BlockSpec((1, N_KV * DEC_SEQ, S_SEL), lambda a, pt: (a, 0, 0))],
        scratch_shapes=[pltpu.VMEM((2, N_KV, S_ROWS, LANES), F32),
                        pltpu.VMEM((2 * N_KV, S_CHUNKS, CHUNK_W), BF16),
                        pltpu.SemaphoreType.DMA((2,))])
    return pl.pallas_call(
        functools.partial(_nsa_sample_cmp_kernel, ia=ia),
        grid_spec=grid_spec,
        out_shape=[jax.ShapeDtypeStruct((n, DEC_SEQ, D_MODEL), F32),
                   jax.ShapeDtypeStruct((n, N_KV * DEC_SEQ, S_SEL), F32)],
        compiler_params=_cparams(("arbitrary",)),
        name="nsa_sample_cmp",
    )(page_table, new_rows, q, w1r, pb, w2, cache)


def _nsa_sample_slc_kernel(pt_ref, new_ref, qr_ref, sel_ref, wbuf_ref, wnew_ref, gate_ref, ocmp_ref, cache_hbm,
                           o_ref, buf, rows_bf, sem, *, ia):
    slot = _gather_pages(cache_hbm, ia, pt_ref, buf, sem, False)
    buf[slot, PAST_LEN:PAST_LEN + DEC_SEQ, :] = new_ref[0]
    buf[slot, PAST_LEN + DEC_SEQ:, :] = jnp.zeros((S_ROWS - PAST_LEN - DEC_SEQ, KV_ROW), F32)
    rows_bf[...] = buf[slot].astype(BF16)

    nq = N_KV * N_G * DEC_SEQ
    qbd = _sample_q_blockdiag(qr_ref[0]).astype(BF16)
    qpos = PAST_LEN + lax.broadcasted_iota(jnp.int32, (DEC_SEQ, 1), 0)

    rows = rows_bf[...]
    s = (_dot_nt(qbd, rows) * ATTN_SCALE).reshape(N_KV, N_G, DEC_SEQ, S_ROWS)
    sel = sel_ref[0].astype(BF16)
    kw = S_ROWS // 5
    chosen = []
    for c in range(5):
        kp = c * kw + lax.broadcasted_iota(jnp.int32, (S_SEL, kw), 1)
        expand = jnp.where(kp // SEL_BLOCK == lax.broadcasted_iota(jnp.int32, (S_SEL, kw), 0), 1.0, 0.0)
        chosen.append(_dot(sel, expand.astype(BF16)))
    chosen = jnp.concatenate(chosen, axis=1).reshape(N_KV, 1, DEC_SEQ, S_ROWS)
    kpos = lax.broadcasted_iota(jnp.int32, (1, S_ROWS), 1)
    mask = (chosen > 0.5) & (kpos <= qpos)[None, None]
    p = _masked_probs(s, mask)
    o_slc = _dot(p.reshape(nq, S_ROWS).astype(BF16), rows)

    wrows = jnp.concatenate([wbuf_ref[0], wnew_ref[0]], axis=0).astype(BF16)
    nw = A_WINDOW + DEC_SEQ
    sw = (_dot_nt(qbd, wrows) * ATTN_SCALE).reshape(N_KV * N_G, DEC_SEQ, nw)
    d = qpos - (PAST_LEN - A_WINDOW + lax.broadcasted_iota(jnp.int32, (DEC_SEQ, nw), 1))
    pw = _masked_probs(sw, ((d >= 0) & (d <= A_WINDOW))[None])
    o_win = _dot(pw.reshape(nq, nw).astype(BF16), wrows)

    gt = jax.nn.sigmoid(gate_ref[0])
    o_cmp = ocmp_ref[0]
    for kv in range(N_KV):
        v0 = kv * 2 * HEAD_DIM + HEAD_DIM
        for g in range(N_G):
            h = kv * N_G + g
            r0 = h * DEC_SEQ
            c0 = kv * LANES + g
            og = (gt[:, c0:c0 + 1] * o_cmp[:, h * HEAD_DIM:(h + 1) * HEAD_DIM]
                  + gt[:, c0 + N_G:c0 + N_G + 1] * o_slc[r0:r0 + DEC_SEQ, v0:v0 + HEAD_DIM]
                  + gt[:, c0 + 2 * N_G:c0 + 2 * N_G + 1] * o_win[r0:r0 + DEC_SEQ, v0:v0 + HEAD_DIM])
            o_ref[0, :, h * HEAD_DIM:(h + 1) * HEAD_DIM] = og


def _nsa_sample_slc(ia, page_table, cache, z, q_rot, sel, win_cache, o_cmp):
    n = z.shape[0]
    row_blk = lambda base: pl.BlockSpec((1, DEC_SEQ, KV_ROW), lambda a, pt: (a, 0, base // KV_ROW))
    grid_spec = pltpu.PrefetchScalarGridSpec(
        num_scalar_prefetch=1,
        grid=(n,),
        in_specs=[row_blk(A_SLC0),
                  pl.BlockSpec((1, DEC_SEQ, D_MODEL), lambda a, pt: (a, 0, 0)),
                  pl.BlockSpec((1, N_KV * DEC_SEQ, S_SEL), lambda a, pt: (a, 0, 0)),
                  pl.BlockSpec((None, 1, A_WINDOW, KV_ROW), lambda a, pt: (ia, a, 0, 0)),
                  row_blk(A_WIN0), row_blk(A_GATE0),
                  pl.BlockSpec((1, DEC_SEQ, D_MODEL), lambda a, pt: (a, 0, 0)),
                  pl.BlockSpec(memory_space=pl.ANY)],
        out_specs=pl.BlockSpec((1, DEC_SEQ, D_MODEL), lambda a, pt: (a, 0, 0)),
        scratch_shapes=[pltpu.VMEM((2, S_ROWS, KV_ROW), F32),
                        pltpu.VMEM((S_ROWS, KV_ROW), BF16),
                        pltpu.SemaphoreType.DMA((2,))])
    return pl.pallas_call(
        functools.partial(_nsa_sample_slc_kernel, ia=ia),
        grid_spec=grid_spec,
        out_shape=jax.ShapeDtypeStruct((n, DEC_SEQ, D_MODEL), F32),
        compiler_params=_cparams(("arbitrary",)),
        name="nsa_sample_slc",
    )(page_table, z, q_rot, sel, win_cache, z, z, o_cmp, cache)


def _swa_sample_kernel(q_ref, new_ref, wbuf_ref, sink_ref, o_ref):
    nq = N_KV * N_G * DEC_SEQ
    qbd = _sample_q_blockdiag(q_ref[0]).astype(BF16)
    rows = jnp.concatenate([wbuf_ref[0], new_ref[0]], axis=0).astype(BF16)
    nw = B_WINDOW + DEC_SEQ
    s = (_dot_nt(qbd, rows) * ATTN_SCALE).reshape(N_KV * N_G, DEC_SEQ, nw)
    qpos = PAST_LEN + lax.broadcasted_iota(jnp.int32, (DEC_SEQ, 1), 0)
    d = qpos - (PAST_LEN - B_WINDOW + lax.broadcasted_iota(jnp.int32, (DEC_SEQ, nw), 1))
    mask = ((d >= 0) & (d <= B_WINDOW))[None]
    sink = sink_ref[...]
    sm = jnp.where(mask, s, NEG_INF)
    m = jnp.maximum(jnp.max(sm, axis=-1, keepdims=True), sink)
    e = jnp.where(mask, jnp.exp(sm - m), 0.0)
    den = jnp.sum(e, axis=-1, keepdims=True) + jnp.exp(sink - m)
    p = e / jnp.maximum(den, TINY)
    o = _dot(p.reshape(nq, nw).astype(BF16), rows)
    for kv in range(N_KV):
        v0 = kv * 2 * HEAD_DIM + HEAD_DIM
        for g in range(N_G):
            h = kv * N_G + g
            o_ref[0, :, h * HEAD_DIM:(h + 1) * HEAD_DIM] = o[h * DEC_SEQ:(h + 1) * DEC_SEQ, v0:v0 + HEAD_DIM]


def _swa_sample_attend(ib, z, win_cache, sinks):
    n = z.shape[0]
    return pl.pallas_call(
        _swa_sample_kernel,
        grid=(n,),
        in_specs=[pl.BlockSpec((1, DEC_SEQ, D_MODEL), lambda a: (a, 0, 0)),
                  pl.BlockSpec((1, DEC_SEQ, KV_ROW), lambda a: (a, 0, D_MODEL // KV_ROW)),
                  pl.BlockSpec((None, 1, B_WINDOW, KV_ROW), lambda a: (ib, a, 0, 0)),
                  pl.BlockSpec((N_HEADS, 1, 1), lambda a: (0, 0, 0))],
        out_specs=pl.BlockSpec((1, DEC_SEQ, D_MODEL), lambda a: (a, 0, 0)),
        out_shape=jax.ShapeDtypeStruct((n, DEC_SEQ, D_MODEL), F32),
        compiler_params=_cparams(("arbitrary",)),
        name="swa_sample_attend",
    )(z, z, win_cache, sinks.reshape(N_HEADS, 1, 1))


def _nsa_col_perm():
    perm = np.full((A_COLS,), -1, np.int64)
    perm[:D_MODEL] = np.arange(D_MODEL)
    kvw = N_KV * HEAD_DIM
    for br, base in enumerate((A_CMP0, A_SLC0, A_WIN0)):
        old0 = D_MODEL + 2 * kvw * br
        for kv in range(N_KV):
            for i in range(2):
                dst = base + kv * 2 * HEAD_DIM + i * HEAD_DIM
                src = old0 + i * kvw + kv * HEAD_DIM
                perm[dst:dst + HEAD_DIM] = np.arange(src, src + HEAD_DIM)
    old_g = D_MODEL + 6 * kvw
    for kv in range(N_KV):
        for br in range(3):
            dst = A_GATE0 + kv * LANES + br * N_G
            src = old_g + br * N_HEADS + kv * N_G
            perm[dst:dst + N_G] = np.arange(src, src + N_G)
    return perm


def _swa_col_perm():
    perm = np.zeros((B_COLS,), np.int64)
    perm[:D_MODEL] = np.arange(D_MODEL)
    kvw = N_KV * HEAD_DIM
    for kv in range(N_KV):
        for i in range(2):
            dst = D_MODEL + kv * 2 * HEAD_DIM + i * HEAD_DIM
            src = D_MODEL + i * kvw + kv * HEAD_DIM
            perm[dst:dst + HEAD_DIM] = np.arange(src, src + HEAD_DIM)
    return perm


def _permute_cols(w, perm):
    cols = jnp.take(w, jnp.asarray(np.maximum(perm, 0)), axis=-1)
    return jnp.where(jnp.asarray(perm >= 0), cols, 0.0)


def _rows_view(z, c0, n, t):
    return z[..., c0:c0 + KV_ROW].reshape(n, t, N_KV, 2, HEAD_DIM)


def _trunk(x, ada, past, p, prompt):
    g, t, _ = x.shape
    tm = min(512, t)
    if prompt:
        pos = jnp.arange(t, dtype=jnp.int32)
        n_seq, t_seq = g, t
    else:
        n_seq, t_seq = t // DEC_SEQ, DEC_SEQ
        pos = jnp.tile(PAST_LEN + jnp.arange(DEC_SEQ, dtype=jnp.int32), n_seq)
    tab = _rope_tables(pos)[None]
    cmp_out, slc_out, nwin_out, swin_out = [], [], [], []
    for l in range(DEPTH):
        sh1, sc1, g1, sh2, sc2, g2 = [ada[l][..., k * D_MODEL:(k + 1) * D_MODEL] for k in range(6)]
        i = l // 2
        if l % 2 == 0:
            z, q_rot = _proj(x, sc1, sh1, p['nsa_w_in'][i], p['zero_a'], tab, tm, (A_SLC0, A_WIN0), True)
            cmp_rows = _rows_view(z, A_CMP0, n_seq, t_seq)
            slc_rows = _rows_view(z, A_SLC0, n_seq, t_seq)
            win_rows = _rows_view(z, A_WIN0, n_seq, t_seq)
            w1r, pb, w2 = p['cmp_w1r'][i], p['cmp_pb'][i], p['cmp_w2'][i]
            if prompt:
                nch = t // CMP_STRIDE
                chunks = z[..., A_CMP0:A_CMP0 + KV_ROW].reshape(g, nch, CMP_STRIDE, 2 * N_KV, HEAD_DIM)
                chunks = chunks.transpose(0, 3, 1, 2, 4).reshape(g, 2 * N_KV, nch, CHUNK_W)
                cblocks = _compress_prompt(chunks, w1r, pb, w2)
                o = _nsa_prompt_attend(z, q_rot, cblocks)
                nwin_out.append(win_rows[:, -min(A_WINDOW, t):])
            else:
                cache_cmp, cache_slc, cache_nwin, _, page_table = past
                zs = z.reshape(n_seq, DEC_SEQ, A_COLS)
                qs = zs[..., :D_MODEL]
                o_cmp, sel = _nsa_sample_cmp(i, page_table, cache_cmp, zs[..., A_CMP0:A_CMP0 + KV_ROW], qs, w1r, pb, w2)
                o = _nsa_sample_slc(i, page_table, cache_slc, zs, q_rot.reshape(n_seq, DEC_SEQ, D_MODEL), sel,
                                    cache_nwin, o_cmp).reshape(g, t, D_MODEL)
                nwin_out.append(jnp.concatenate([cache_nwin[i].reshape(n_seq, A_WINDOW, N_KV, 2, HEAD_DIM)[:, DEC_SEQ:],
                                                 win_rows], axis=1))
            cmp_out.append(cmp_rows)
            slc_out.append(slc_rows)
            w_o, b_o = p['nsa_w_o'][i], p['zero_d']
        else:
            (z,) = _proj(x, sc1, sh1, p['swa_w_in'][i], p['swa_b_in'][i], tab, tm, (D_MODEL,), False)
            rows = _rows_view(z, D_MODEL, n_seq, t_seq)
            if prompt:
                o = _swa_prompt_attend(z, p['swa_sinks'][i])
                swin_out.append(rows[:, -min(B_WINDOW, t):])
            else:
                cache_swin = past[3]
                o = _swa_sample_attend(i, z.reshape(n_seq, DEC_SEQ, B_COLS), cache_swin, p['swa_sinks'][i]).reshape(g, t, D_MODEL)
                swin_out.append(jnp.concatenate([cache_swin[i].reshape(n_seq, B_WINDOW, N_KV, 2, HEAD_DIM)[:, DEC_SEQ:],
                                                 rows], axis=1))
            w_o, b_o = p['swa_w_o'][i], p['swa_b_o'][i]
        x = _outproj_ln(o, x, g1, w_o, b_o, p['ln_g'][l, 0], p['ln_b'][l, 0], tm)
        x = _moe_ln(x, sc2, sh2, g2, p['router_wt'], p['router_b'], p['moe_w_gate'][l], p['moe_w_up'][l],
                    p['moe_w_down'][l], p['ln_g'][l, 1], p['ln_b'][l, 1], tm)
    return x, (jnp.stack(cmp_out), jnp.stack(slc_out), jnp.stack(nwin_out), jnp.stack(swin_out))


def kernel(x_prompt, x_sample, cache_nsa_cmp, cache_nsa_slc, cache_nsa_win, cache_swa_win, page_table, c_prompt, c_sample, nsa_w_in, nsa_cmp_pe, nsa_cmp_w1, nsa_cmp_b1, nsa_cmp_w2, nsa_w_o, swa_w_in, swa_b_in, swa_sinks, swa_w_o, swa_b_o, ada_w, ada_b, ln_g, ln_b, router_w, router_b, moe_w_gate, moe_w_up, moe_w_down):
    n_a = nsa_w_in.shape[0]
    batch, seq, _ = x_prompt.shape
    n_dec = x_sample.shape[0]
    n_pool = cache_nsa_cmp.shape[1]

    pe_term = jnp.stack([_pe_term(nsa_cmp_pe[i], nsa_cmp_w1[i]) for i in range(n_a)])
    w1r = nsa_cmp_w1.reshape(n_a, 2, CMP_SHIFTS, CHUNK_W, CMP_HIDDEN).transpose(0, 1, 3, 2, 4)
    p = {
        'nsa_w_in': _permute_cols(nsa_w_in, _nsa_col_perm()).astype(BF16),
        'zero_a': jnp.zeros((1, A_COLS), F32),
        'zero_d': jnp.zeros((1, D_MODEL), F32),
        'cmp_w1r': w1r.reshape(n_a, 2, CHUNK_W, CMP_SHIFTS * CMP_HIDDEN).astype(BF16),
        'cmp_pb': (pe_term + nsa_cmp_b1).reshape(n_a, 2, 1, CMP_HIDDEN),
        'cmp_w2': nsa_cmp_w2.astype(BF16),
        'nsa_w_o': nsa_w_o.astype(BF16),
        'swa_w_in': _permute_cols(swa_w_in, _swa_col_perm()).astype(BF16),
        'swa_b_in': _permute_cols(swa_b_in, _swa_col_perm())[:, None, :],
        'swa_sinks': swa_sinks,
        'swa_w_o': swa_w_o.astype(BF16),
        'swa_b_o': swa_b_o[:, None, :],
        'ln_g': ln_g[:, :, None, :],
        'ln_b': ln_b[:, :, None, :],
        'router_wt': router_w.T,
        'router_b': router_b[:, None],
        'moe_w_gate': moe_w_gate.astype(BF16),
        'moe_w_up': moe_w_up.astype(BF16),
        'moe_w_down': moe_w_down.astype(BF16),
    }

    ada = _ada(jnp.concatenate([c_prompt, c_sample], axis=0), ada_w, ada_b)
    ada_p = ada[:, :batch, None, :]
    ada_s = jnp.broadcast_to(ada[:, batch:, None, :], (DEPTH, n_dec, DEC_SEQ, 6 * D_MODEL))
    ada_s = ada_s.reshape(DEPTH, 1, n_dec * DEC_SEQ, 6 * D_MODEL)

    y_p, (cmp_p, slc_p, nwin_p, swin_p) = _trunk(x_prompt, ada_p, None, p, True)
    past = (cache_nsa_cmp.reshape(n_a, n_pool, PAGE_SIZE, KV_ROW),
            cache_nsa_slc.reshape(n_a, n_pool, PAGE_SIZE, KV_ROW),
            cache_nsa_win.reshape(n_a, n_dec, A_WINDOW, KV_ROW),
            cache_swa_win.reshape(cache_swa_win.shape[0], n_dec, B_WINDOW, KV_ROW),
            page_table)
    y_s, (cmp_s, slc_s, nwin_s, swin_s) = _trunk(x_sample.reshape(1, n_dec * DEC_SEQ, D_MODEL), ada_s, past, p, False)
    return (y_p, y_s.reshape(n_dec, DEC_SEQ, D_MODEL), cmp_p, slc_p, nwin_p, swin_p, cmp_s, slc_s, nwin_s, swin_s)
```

```python
import functools
import math

import numpy as np
import jax
import jax.numpy as jnp
from jax import lax
from jax.experimental import pallas as pl
from jax.experimental.pallas import tpu as pltpu

D_MODEL = 1024
SEQ = 2048
DEPTH = 4
DEC_SEQ = 8
PAST_LEN = 8192
PAGE_SIZE = 128
N_PAGES = PAST_LEN // PAGE_SIZE

HEAD_DIM = 64
ROT_DIM = HEAD_DIM // 4
ROT_HALF = ROT_DIM // 2
ROPE_THETA = 500000.0
ATTN_SCALE = HEAD_DIM ** -0.5

N_HEADS = D_MODEL // HEAD_DIM
N_KV = 2
N_G = N_HEADS // N_KV
KV_ROW = N_KV * 2 * HEAD_DIM
CMP_BLOCK = 32
CMP_STRIDE = 16
CMP_HIDDEN = 4 * HEAD_DIM
CMP_SHIFTS = CMP_BLOCK // CMP_STRIDE
CHUNK_W = CMP_STRIDE * HEAD_DIM
SEL_BLOCK = 64
SEL_TOP = 16
CHUNKS_PER_SEL = SEL_BLOCK // CMP_STRIDE
A_WINDOW = 512
B_WINDOW = 128
FORCE_SCORE = 1e9

N_EXPERTS = 16
N_GROUPS = 4
EXPERTS_PER_GROUP = N_EXPERTS // N_GROUPS
D_FF_EXPERT = D_MODEL // 2

DN_ALPHA = (2.0 * DEPTH) ** 0.25
LN_EPS = 1e-5
NEG_INF = -1e30
TINY = 1e-30
LOG2E = math.log2(math.e)

LANES = 128
A_COLS = 2048
A_CMP0, A_SLC0, A_WIN0, A_GATE0 = 1024, 1280, 1536, 1792
B_COLS = D_MODEL + KV_ROW
VMEM_LIMIT = 56 * 1024 * 1024

F32 = jnp.float32
BF16 = jnp.bfloat16


def _cparams(sem):
    return pltpu.CompilerParams(dimension_semantics=sem, vmem_limit_bytes=VMEM_LIMIT)


def _dot(a, b):
    return jnp.dot(a, b, preferred_element_type=F32)


def _dot_nt(a, b, precision=None):
    return lax.dot_general(a, b, (((1,), (1,)), ((), ())), precision=precision,
                           preferred_element_type=F32)


def _masked_probs(s, mask):
    s = jnp.where(mask, s, NEG_INF)
    m = jnp.max(s, axis=-1, keepdims=True)
    e = jnp.where(mask, jnp.exp(s - m), 0.0)
    den = jnp.sum(e, axis=-1, keepdims=True)
    return e / jnp.maximum(den, TINY)


def _layer_norm(r, g, b):
    mu = jnp.mean(r, axis=-1, keepdims=True)
    rc = r - mu
    var = jnp.mean(rc * rc, axis=-1, keepdims=True)
    return rc * lax.rsqrt(var + LN_EPS) * g + b


def _ada_kernel(c_ref, w_ref, b_ref, o_ref):
    c = c_ref[...]
    a = (c * jax.nn.sigmoid(c)).astype(BF16)
    o_ref[0] = _dot(a, w_ref[0].astype(BF16)) + b_ref[0]


def _ada(c_all, ada_w, ada_b):
    m = c_all.shape[0]
    n = ada_w.shape[-1]
    tn = 1536
    return pl.pallas_call(
        _ada_kernel,
        grid=(DEPTH, n // tn),
        in_specs=[pl.BlockSpec((m, D_MODEL), lambda l, j: (0, 0)),
                  pl.BlockSpec((1, D_MODEL, tn), lambda l, j: (l, 0, j)),
                  pl.BlockSpec((1, 1, tn), lambda l, j: (l, 0, j))],
        out_specs=pl.BlockSpec((1, m, tn), lambda l, j: (l, 0, j)),
        out_shape=jax.ShapeDtypeStruct((DEPTH, m, n), F32),
        compiler_params=_cparams(("arbitrary", "arbitrary")),
        name="ada",
    )(c_all, ada_w, ada_b.reshape(DEPTH, 1, n))


def _pe_term_kernel(pe_ref, w_ref, o_ref):
    o_ref[0] = _dot(pe_ref[0].astype(BF16), w_ref[0].astype(BF16))


def _pe_term(pe, w1):
    k = CMP_BLOCK * HEAD_DIM
    pe8 = jnp.broadcast_to(pe.reshape(2, 1, k), (2, 8, k))
    out = pl.pallas_call(
        _pe_term_kernel,
        grid=(2,),
        in_specs=[pl.BlockSpec((1, 8, k), lambda i: (i, 0, 0)),
                  pl.BlockSpec((1, k, CMP_HIDDEN), lambda i: (i, 0, 0))],
        out_specs=pl.BlockSpec((1, 8, CMP_HIDDEN), lambda i: (i, 0, 0)),
        out_shape=jax.ShapeDtypeStruct((2, 8, CMP_HIDDEN), F32),
        compiler_params=_cparams(("arbitrary",)),
        name="pe_term",
    )(pe8, w1)
    return out[:, 0]


def _rope_tables(pos):
    inv = jnp.exp(-math.log(ROPE_THETA) * jnp.arange(ROT_HALF, dtype=F32) / ROT_HALF)
    ang = pos.astype(F32)[:, None] * inv[None, :]
    cos, sin = jnp.cos(ang), jnp.sin(ang)
    t = pos.shape[0]
    one = jnp.ones((t, HEAD_DIM - ROT_DIM), F32)
    zero = jnp.zeros((t, HEAD_DIM - ROT_DIM), F32)
    zh = jnp.zeros((t, ROT_HALF), F32)
    cos_h = jnp.concatenate([cos, cos, one], axis=1)
    sa_h = jnp.concatenate([-sin, zh, zero], axis=1)
    sb_h = jnp.concatenate([zh, sin, zero], axis=1)
    id_h = jnp.ones((t, HEAD_DIM), F32)
    z_h = jnp.zeros((t, HEAD_DIM), F32)
    return jnp.concatenate([cos_h, cos_h, sa_h, sa_h, sb_h, sb_h,
                            cos_h, id_h, sa_h, z_h, sb_h, z_h], axis=1)


def _rope_apply(seg, cos, sa, sb):
    w = seg.shape[1]
    reps = w // LANES
    c = jnp.tile(cos, (1, reps))
    a = jnp.tile(sa, (1, reps))
    b = jnp.tile(sb, (1, reps))
    return seg * c + pltpu.roll(seg, w - ROT_HALF, 1) * a + pltpu.roll(seg, ROT_HALF, 1) * b


def _proj_kernel(x_ref, sc_ref, sh_ref, w_ref, b_ref, tab_ref, z_ref, *qrot_ref, kv_segs):
    x = x_ref[0]
    h = x * (1.0 + sc_ref[0]) + sh_ref[0]
    z = _dot(h.astype(BF16), w_ref[...]) + b_ref[...]
    tab = tab_ref[0]
    tq = [tab[:, i * LANES:(i + 1) * LANES] for i in range(3)]
    tk = [tab[:, i * LANES:(i + 1) * LANES] for i in range(3, 6)]
    q_rot = _rope_apply(z[:, :D_MODEL], *tq)
    z_ref[0] = z
    if qrot_ref:
        qrot_ref[0][0] = q_rot
    else:
        z_ref[0, :, :D_MODEL] = q_rot
    for c0 in kv_segs:
        z_ref[0, :, c0:c0 + KV_ROW] = _rope_apply(z[:, c0:c0 + KV_ROW], *tk)


def _proj(x, sc, sh, w, b, tab, tm, kv_segs, sep_qrot):
    g, t, _ = x.shape
    n = w.shape[1]
    tmod = tm if sc.shape[1] == t else 1
    out_shape = [jax.ShapeDtypeStruct((g, t, n), F32)]
    out_specs = [pl.BlockSpec((1, tm, n), lambda a, i: (a, i, 0))]
    if sep_qrot:
        out_shape.append(jax.ShapeDtypeStruct((g, t, D_MODEL), F32))
        out_specs.append(pl.BlockSpec((1, tm, D_MODEL), lambda a, i: (a, i, 0)))
    mod_spec = pl.BlockSpec((1, tmod, D_MODEL), (lambda a, i: (a, i, 0)) if tmod == tm else (lambda a, i: (a, 0, 0)))
    return pl.pallas_call(
        functools.partial(_proj_kernel, kv_segs=kv_segs),
        grid=(g, t // tm),
        in_specs=[pl.BlockSpec((1, tm, D_MODEL), lambda a, i: (a, i, 0)),
                  mod_spec, mod_spec,
                  pl.BlockSpec((D_MODEL, n), lambda a, i: (0, 0)),
                  pl.BlockSpec((1, n), lambda a, i: (0, 0)),
                  pl.BlockSpec((1, tm, 6 * LANES), lambda a, i: (0, i, 0))],
        out_specs=out_specs,
        out_shape=out_shape,
        compiler_params=_cparams(("arbitrary", "arbitrary")),
        name="proj",
    )(x, sc, sh, w, b, tab)


def _outproj_kernel(o_ref, x_ref, g_ref, w_ref, b_ref, lg_ref, lb_ref, out_ref):
    y = _dot(o_ref[0].astype(BF16), w_ref[...]) + b_ref[...]
    r = DN_ALPHA * x_ref[0] + g_ref[0] * y
    out_ref[0] = _layer_norm(r, lg_ref[...], lb_ref[...])


def _outproj_ln(o, x, gate, w, b, lg, lb, tm):
    g, t, _ = x.shape
    tmod = tm if gate.shape[1] == t else 1
    row = pl.BlockSpec((1, tm, D_MODEL), lambda a, i: (a, i, 0))
    vec = pl.BlockSpec((1, D_MODEL), lambda a, i: (0, 0))
    mod_spec = pl.BlockSpec((1, tmod, D_MODEL), (lambda a, i: (a, i, 0)) if tmod == tm else (lambda a, i: (a, 0, 0)))
    return pl.pallas_call(
        _outproj_kernel,
        grid=(g, t // tm),
        in_specs=[row, row, mod_spec,
                  pl.BlockSpec((D_MODEL, D_MODEL), lambda a, i: (0, 0)), vec, vec, vec],
        out_specs=row,
        out_shape=jax.ShapeDtypeStruct((g, t, D_MODEL), F32),
        compiler_params=_cparams(("arbitrary", "arbitrary")),
        name="outproj_ln",
    )(o, x, gate, w, b, lg, lb)


def _router_gates_t(h, rwt_ref, rb_ref):
    logits = _dot_nt(rwt_ref[...], h, precision=lax.Precision.HIGHEST)
    m = jnp.max(logits, axis=0, keepdims=True)
    ex = jnp.exp(logits - m)
    probs = ex / jnp.sum(ex, axis=0, keepdims=True)
    biased = probs + rb_ref[...]
    eidx = lax.broadcasted_iota(jnp.int32, biased.shape, 0)
    best = None
    g_sel = None
    for g in range(N_GROUPS):
        rows = [biased[g * EXPERTS_PER_GROUP + i:g * EXPERTS_PER_GROUP + i + 1] for i in range(EXPERTS_PER_GROUP)]
        gs = None
        for i in range(EXPERTS_PER_GROUP):
            for j in range(i + 1, EXPERTS_PER_GROUP):
                pair = rows[i] + rows[j]
                gs = pair if gs is None else jnp.maximum(gs, pair)
        if best is None:
            best, g_sel = gs, jnp.zeros(gs.shape, jnp.int32)
        else:
            upd = gs > best
            g_sel = jnp.where(upd, g, g_sel)
            best = jnp.maximum(best, gs)
    v = jnp.where(eidx // EXPERTS_PER_GROUP == g_sel, biased, NEG_INF)
    m1 = jnp.max(v, axis=0, keepdims=True)
    i1 = jnp.min(jnp.where(v == m1, eidx, N_EXPERTS), axis=0, keepdims=True)
    v2 = jnp.where(eidx == i1, -jnp.inf, v)
    m2 = jnp.max(v2, axis=0, keepdims=True)
    i2 = jnp.min(jnp.where(v2 == m2, eidx, N_EXPERTS), axis=0, keepdims=True)
    w1 = jnp.sum(jnp.where(eidx == i1, probs, 0.0), axis=0, keepdims=True)
    w2 = jnp.sum(jnp.where(eidx == i2, probs, 0.0), axis=0, keepdims=True)
    tot = w1 + w2
    return jnp.where(eidx == i1, w1 / tot, 0.0) + jnp.where(eidx == i2, w2 / tot, 0.0)


def _moe_kernel(x_ref, sc_ref, sh_ref, g_ref, rwt_ref, rb_ref, wg_ref, wu_ref, wd_ref, lg_ref, lb_ref,
                out_ref, h_scr, gate_scr, acc_scr):
    e = pl.program_id(2)

    @pl.when(e == 0)
    def _():
        h = x_ref[0] * (1.0 + sc_ref[0]) + sh_ref[0]
        h_scr[...] = h.astype(BF16)
        gt = _router_gates_t(h, rwt_ref, rb_ref)
        pad = jnp.zeros((LANES - N_EXPERTS, gt.shape[1]), F32)
        gate_scr[...] = jnp.concatenate([gt, pad], axis=0).T
        acc_scr[...] = jnp.zeros_like(acc_scr)

    h = h_scr[...]
    a = _dot(h, wg_ref[0])
    u = _dot(h, wu_ref[0])
    gates = gate_scr[...]
    lane = lax.broadcasted_iota(jnp.int32, gates.shape, 1)
    gcol = jnp.sum(jnp.where(lane == e, gates, 0.0), axis=1, keepdims=True)
    act = (a * jax.nn.sigmoid(a)) * u * gcol
    acc_scr[...] += _dot(act.astype(BF16), wd_ref[0])

    @pl.when(e == N_EXPERTS - 1)
    def _():
        r = DN_ALPHA * x_ref[0] + g_ref[0] * acc_scr[...]
        out_ref[0] = _layer_norm(r, lg_ref[...], lb_ref[...])


def _moe_ln(x, sc, sh, gate, rwt, rb, wg, wu, wd, lg, lb, tm):
    g, t, _ = x.shape
    tmod = tm if sc.shape[1] == t else 1
    row = pl.BlockSpec((1, tm, D_MODEL), lambda a, i, e: (a, i, 0))
    vec = pl.BlockSpec((1, D_MODEL), lambda a, i, e: (0, 0))
    mod_spec = pl.BlockSpec((1, tmod, D_MODEL),
                            (lambda a, i, e: (a, i, 0)) if tmod == tm else (lambda a, i, e: (a, 0, 0)))
    return pl.pallas_call(
        _moe_kernel,
        grid=(g, t // tm, N_EXPERTS),
        in_specs=[row, mod_spec, mod_spec, mod_spec,
                  pl.BlockSpec((N_EXPERTS, D_MODEL), lambda a, i, e: (0, 0)),
                  pl.BlockSpec((N_EXPERTS, 1), lambda a, i, e: (0, 0)),
                  pl.BlockSpec((1, D_MODEL, D_FF_EXPERT), lambda a, i, e: (e, 0, 0)),
                  pl.BlockSpec((1, D_MODEL, D_FF_EXPERT), lambda a, i, e: (e, 0, 0)),
                  pl.BlockSpec((1, D_FF_EXPERT, D_MODEL), lambda a, i, e: (e, 0, 0)),
                  vec, vec],
        out_specs=row,
        out_shape=jax.ShapeDtypeStruct((g, t, D_MODEL), F32),
        scratch_shapes=[pltpu.VMEM((tm, D_MODEL), BF16),
                        pltpu.VMEM((tm, LANES), F32),
                        pltpu.VMEM((tm, D_MODEL), F32)],
        compiler_params=_cparams(("arbitrary", "arbitrary", "arbitrary")),
        name="moe_ln",
    )(x, sc, sh, gate, rwt, rb, wg, wu, wd, lg, lb)


def _compress_blocks(chunks, w1r, pb, w2):
    nch = chunks.shape[0]
    part = _dot(chunks, w1r)
    pre = part[:, :CMP_HIDDEN] + pltpu.roll(part[:, CMP_HIDDEN:], nch - 1, 0)
    hid = jax.nn.gelu(pre + pb)
    return _dot(hid.astype(BF16), w2)


def _compress_kernel(ch_ref, w1r_ref, pb_ref, w2_ref, o_ref):
    for j in range(2 * N_KV):
        i = j % 2
        o_ref[0, j] = _compress_blocks(ch_ref[0, j].astype(BF16), w1r_ref[i], pb_ref[i], w2_ref[i])


def _compress_prompt(chunks, w1r, pb, w2):
    b, _, nch, _ = chunks.shape
    return pl.pallas_call(
        _compress_kernel,
        grid=(b,),
        in_specs=[pl.BlockSpec((1, 2 * N_KV, nch, CHUNK_W), lambda a: (a, 0, 0, 0)),
                  pl.BlockSpec((2, CHUNK_W, 2 * CMP_HIDDEN), lambda a: (0, 0, 0)),
                  pl.BlockSpec((2, 1, CMP_HIDDEN), lambda a: (0, 0, 0)),
                  pl.BlockSpec((2, CMP_HIDDEN, HEAD_DIM), lambda a: (0, 0, 0))],
        out_specs=pl.BlockSpec((1, 2 * N_KV, nch, HEAD_DIM), lambda a: (a, 0, 0, 0)),
        out_shape=jax.ShapeDtypeStruct((b, 2 * N_KV, nch, HEAD_DIM), F32),
        compiler_params=_cparams(("arbitrary",)),
        name="compress_prompt",
    )(chunks, w1r, pb, w2)


def _stack_heads(q):
    return jnp.concatenate([q[:, g * HEAD_DIM:(g + 1) * HEAD_DIM] for g in range(N_G)], axis=0)


def _chunk_score_matrix_t(n_sel, n_cmp):
    j = lax.broadcasted_iota(jnp.int32, (n_sel, n_cmp), 0)
    c = lax.broadcasted_iota(jnp.int32, (n_sel, n_cmp), 1)
    lo = j * CHUNKS_PER_SEL
    hi = lo + CHUNKS_PER_SEL - 1
    m = jnp.zeros((n_sel, n_cmp), F32)
    for r in range(CMP_SHIFTS):
        m = m + jnp.where((c + r >= lo) & (c + r <= hi), 1.0, 0.0)
    return m


def _nsa_prompt_kernel(q_ref, qr_ref, cb_ref, slc_ref, win_ref, gate_ref, o_ref, *, tq, kc):
    qi = pl.program_id(2)
    t0 = qi * tq
    n_cmp = cb_ref.shape[2]
    n_sel = SEQ // SEL_BLOCK
    tpos = t0 + lax.broadcasted_iota(jnp.int32, (tq, 1), 0)

    qs = _stack_heads(q_ref[0]).astype(BF16)
    k_c = cb_ref[0, 0].astype(BF16)
    v_c = cb_ref[0, 1].astype(BF16)
    s = (_dot_nt(qs, k_c) * ATTN_SCALE).reshape(N_G, tq, n_cmp)
    cend = lax.broadcasted_iota(jnp.int32, (tq, n_cmp), 1) * CMP_STRIDE + CMP_BLOCK - 1
    p = _masked_probs(s, (cend <= tpos)[None])
    o_cmp = _dot(p.reshape(N_G * tq, n_cmp).astype(BF16), v_c)
    imp = jnp.sum(p, axis=0)

    score_t = _dot_nt(_chunk_score_matrix_t(n_sel, n_cmp), imp, precision=lax.Precision.HIGHEST)
    jj = lax.broadcasted_iota(jnp.int32, (n_sel, tq), 0)
    tt = t0 + lax.broadcasted_iota(jnp.int32, (n_sel, tq), 1)
    cur = tt // SEL_BLOCK
    forced = (jj == 0) | (jj == cur) | (jj == cur - 1)
    sc = jnp.where(jj * SEL_BLOCK <= tt, jnp.where(forced, FORCE_SCORE, score_t), NEG_INF)
    rank = jnp.zeros((n_sel, tq), F32)
    for i in range(n_sel):
        ri = sc[i:i + 1]
        rank = rank + jnp.where(ri > sc, 1.0, jnp.where((ri == sc) & (jj > i), 1.0, 0.0))
    sel_t = jnp.where(rank < SEL_TOP, 1.0, 0.0)
    sel = jnp.concatenate([sel_t, jnp.zeros((LANES - n_sel, tq), F32)], axis=0).T.astype(BF16)

    qrs = (_stack_heads(qr_ref[0]) * (ATTN_SCALE * LOG2E)).astype(BF16)

    def body(c, carry):
        m, l, acc = carry
        k0 = pl.multiple_of(c * kc, kc)
        rows = slc_ref[0, pl.ds(k0, kc), :]
        k = rows[:, :HEAD_DIM].astype(BF16)
        v = rows[:, HEAD_DIM:].astype(BF16)
        sk = _dot_nt(qrs, k).reshape(N_G, tq, kc)
        kpos = k0 + lax.broadcasted_iota(jnp.int32, (LANES, kc), 1)
        expand = jnp.where(kpos // SEL_BLOCK == lax.broadcasted_iota(jnp.int32, (LANES, kc), 0), 1.0, 0.0)
        chosen = _dot(sel, expand.astype(BF16))
        sm = sk + jnp.where((chosen > 0.5) & (kpos[:1] <= tpos), 0.0, NEG_INF)[None]
        m_new = jnp.maximum(m, jnp.max(sm, axis=-1, keepdims=True))
        alpha = jnp.exp2(m - m_new)
        e = jnp.exp2(sm - m_new)
        l = alpha * l + jnp.sum(e, axis=-1, keepdims=True)
        pv = _dot(e.reshape(N_G * tq, kc).astype(BF16), v).reshape(N_G, tq, HEAD_DIM)
        return m_new, l, alpha * acc + pv

    init = (jnp.full((N_G, tq, 1), NEG_INF, F32), jnp.zeros((N_G, tq, 1), F32),
            jnp.zeros((N_G, tq, HEAD_DIM), F32))
    _, l, acc = lax.fori_loop(0, (t0 + tq + kc - 1) // kc, body, init)
    o_slc = acc / jnp.maximum(l, TINY)

    o_win = _band_attend(qrs, win_ref, t0, tq, A_WINDOW, None)

    gt = jax.nn.sigmoid(gate_ref[0])
    o_cmp = o_cmp.reshape(N_G, tq, HEAD_DIM)
    for g in range(N_G):
        og = (gt[:, g:g + 1] * o_cmp[g] + gt[:, N_G + g:N_G + g + 1] * o_slc[g]
              + gt[:, 2 * N_G + g:2 * N_G + g + 1] * o_win[g])
        o_ref[0, :, g * HEAD_DIM:(g + 1) * HEAD_DIM] = og


def _band_attend(qrs, rows_ref, t0, tq, window, sinks):
    span = window + tq
    start = pl.multiple_of(jnp.maximum(t0 - window, 0), tq)
    rows = rows_ref[0, pl.ds(start, span), :]
    k = rows[:, :HEAD_DIM].astype(BF16)
    v = rows[:, HEAD_DIM:].astype(BF16)
    d = (t0 + lax.broadcasted_iota(jnp.int32, (tq, span), 0)) - (start + lax.broadcasted_iota(jnp.int32, (tq, span), 1))
    s = _dot_nt(qrs, k).reshape(N_G, tq, span) + jnp.where((d >= 0) & (d <= window), 0.0, NEG_INF)[None]
    m = jnp.max(s, axis=-1, keepdims=True)
    if sinks is not None:
        sink = jnp.concatenate([jnp.full((1, tq, 1), sk, F32) for sk in sinks], axis=0)
        m = jnp.maximum(m, sink)
    e = jnp.exp2(s - m)
    den = jnp.sum(e, axis=-1, keepdims=True)
    if sinks is not None:
        den = den + jnp.exp2(sink - m)
    o = _dot(e.reshape(N_G * tq, span).astype(BF16), v).reshape(N_G, tq, HEAD_DIM)
    return o / jnp.maximum(den, TINY)


def _nsa_prompt_attend(z, q_rot, cblocks, tq=128, kc=512):
    b, t, _ = z.shape
    n_cmp = cblocks.shape[2]
    hw = N_G * HEAD_DIM
    blk = lambda base: pl.BlockSpec((1, t, LANES), lambda a, k, i: (a, 0, base // LANES + k))
    return pl.pallas_call(
        functools.partial(_nsa_prompt_kernel, tq=tq, kc=kc),
        grid=(b, N_KV, t // tq),
        in_specs=[pl.BlockSpec((1, tq, hw), lambda a, k, i: (a, i, k)),
                  pl.BlockSpec((1, tq, hw), lambda a, k, i: (a, i, k)),
                  pl.BlockSpec((1, 2, n_cmp, HEAD_DIM), lambda a, k, i: (a, k, 0, 0)),
                  blk(A_SLC0), blk(A_WIN0),
                  pl.BlockSpec((1, tq, LANES), lambda a, k, i: (a, i, A_GATE0 // LANES + k))],
        out_specs=pl.BlockSpec((1, tq, hw), lambda a, k, i: (a, i, k)),
        out_shape=jax.ShapeDtypeStruct((b, t, D_MODEL), F32),
        compiler_params=_cparams(("arbitrary", "arbitrary", "arbitrary")),
        name="nsa_prompt_attend",
    )(z, q_rot, cblocks, z, z, z)


def _swa_prompt_kernel(sink_ref, q_ref, rows_ref, o_ref, *, tq):
    kv = pl.program_id(1)
    t0 = pl.program_id(2) * tq
    qrs = (_stack_heads(q_ref[0]) * (ATTN_SCALE * LOG2E)).astype(BF16)
    sinks = [sink_ref[kv, g] * LOG2E for g in range(N_G)]
    o = _band_attend(qrs, rows_ref, t0, tq, B_WINDOW, sinks)
    for g in range(N_G):
        o_ref[0, :, g * HEAD_DIM:(g + 1) * HEAD_DIM] = o[g]


def _swa_prompt_attend(z, sinks, tq=128):
    b, t, _ = z.shape
    hw = N_G * HEAD_DIM
    return pl.pallas_call(
        functools.partial(_swa_prompt_kernel, tq=tq),
        grid=(b, N_KV, t // tq),
        in_specs=[pl.BlockSpec(memory_space=pltpu.SMEM),
                  pl.BlockSpec((1, tq, hw), lambda a, k, i: (a, i, k)),
                  pl.BlockSpec((1, t, LANES), lambda a, k, i: (a, 0, D_MODEL // LANES + k))],
        out_specs=pl.BlockSpec((1, tq, hw), lambda a, k, i: (a, i, k)),
        out_shape=jax.ShapeDtypeStruct((b, t, D_MODEL), F32),
        compiler_params=_cparams(("arbitrary", "arbitrary", "arbitrary")),
        name="swa_prompt_attend",
    )(sinks.reshape(N_KV, N_G), z, z)


S_CHUNKS = 520
S_ROWS = S_CHUNKS * CMP_STRIDE
S_SEL = 256
N_SEL_SAMPLE = PAST_LEN // SEL_BLOCK + 1
SUBLANES = 8
SLABS = CMP_STRIDE // SUBLANES
PAGES_PER_ITER = 4
assert N_SEL_SAMPLE == LANES + 1 and DEC_SEQ == SUBLANES and N_PAGES % PAGES_PER_ITER == 0


def _page_copy(cache_hbm, ia, pt_ref, n, p, buf, slot, sem):
    return pltpu.make_async_copy(cache_hbm.at[ia, pt_ref[n, p]],
                                 buf.at[slot, :, pl.ds(pl.multiple_of(p * PAGE_SIZE, PAGE_SIZE), PAGE_SIZE)],
                                 sem.at[slot])


def _gather_pages(cache_hbm, ia, pt_ref, buf, sem):
    n = pl.program_id(0)
    nn = pl.num_programs(0)
    slot = n % 2

    def start(seq, sl):
        def go(p, c):
            _page_copy(cache_hbm, ia, pt_ref, seq, p, buf, sl, sem).start()
            return c
        lax.fori_loop(0, N_PAGES, go, 0)

    @pl.when(n == 0)
    def _():
        start(0, 0)

    @pl.when(n + 1 < nn)
    def _():
        start(n + 1, 1 - slot)

    def wait(p, c):
        _page_copy(cache_hbm, ia, pt_ref, n, p, buf, slot, sem).wait()
        return c
    lax.fori_loop(0, N_PAGES, wait, 0)
    return slot


def _sample_q_rows(q, kv):
    return jnp.concatenate([q[:, (kv * N_G + g) * HEAD_DIM:(kv * N_G + g + 1) * HEAD_DIM] for g in range(N_G)], axis=0)


def _sample_q_blockdiag(q):
    parts = []
    for kv in range(N_KV):
        qs = _sample_q_rows(q, kv)
        left = jnp.zeros((qs.shape[0], kv * 2 * HEAD_DIM), F32)
        right = jnp.zeros((qs.shape[0], KV_ROW - kv * 2 * HEAD_DIM - HEAD_DIM), F32)
        parts.append(jnp.concatenate([left, qs, right], axis=1) if kv else jnp.concatenate([qs, right], axis=1))
    return jnp.concatenate(parts, axis=0)


def _nsa_sample_cmp_kernel(pt_ref, new_ref, q_ref, w1r_ref, pb_ref, w2_ref, cache_hbm,
                           ocmp_ref, sel_ref, pages, rows, chunks, sem, *, ia):
    slot = _gather_pages(cache_hbm, ia, pt_ref, pages, sem)

    def to_rows(it, c):
        for u in range(PAGES_PER_ITER):
            p = it * PAGES_PER_ITER + u
            r0 = pl.multiple_of(p * PAGE_SIZE, PAGE_SIZE)
            c0 = pl.multiple_of(p * (PAGE_SIZE // SLABS), PAGE_SIZE // SLABS)
            for kv in range(N_KV):
                t = pages[slot, kv * LANES:(kv + 1) * LANES, pl.ds(r0, PAGE_SIZE)].T
                for v in range(PAGE_SIZE // SUBLANES):
                    rows[kv, v % SLABS, pl.ds(c0 + (v // SLABS) * SUBLANES, SUBLANES), :] = t[v * SUBLANES:(v + 1) * SUBLANES]
        return c
    lax.fori_loop(0, N_PAGES // PAGES_PER_ITER, to_rows, 0)
    new_rows = new_ref[0]
    tail = (PAST_LEN // CMP_STRIDE) * SUBLANES
    for kv in range(N_KV):
        rows[kv, 0, tail:tail + DEC_SEQ, :] = new_rows[:, kv * LANES:(kv + 1) * LANES]
        rows[kv, 0, tail + DEC_SEQ:, :] = jnp.zeros((S_CHUNKS * SUBLANES - tail - DEC_SEQ, LANES), F32)
        for slab in range(1, SLABS):
            rows[kv, slab, tail:, :] = jnp.zeros((S_CHUNKS * SUBLANES - tail, LANES), F32)

    lo = lax.broadcasted_iota(jnp.int32, (S_CHUNKS, LANES), 1) < HEAD_DIM
    for k in range(CMP_STRIDE // 2):
        for kv in range(N_KV):
            sa, sb = 2 * k, 2 * k + 1
            a = rows[kv, sa // SUBLANES, pl.ds(sa % SUBLANES, S_CHUNKS, stride=SUBLANES), :]
            b = rows[kv, sb // SUBLANES, pl.ds(sb % SUBLANES, S_CHUNKS, stride=SUBLANES), :]
            chunks[2 * kv, :, k * LANES:(k + 1) * LANES] = jnp.where(lo, a, pltpu.roll(b, HEAD_DIM, 1)).astype(BF16)
            chunks[2 * kv + 1, :, k * LANES:(k + 1) * LANES] = jnp.where(lo, pltpu.roll(a, HEAD_DIM, 1), b).astype(BF16)

    q = q_ref[0]
    qpos = PAST_LEN + lax.broadcasted_iota(jnp.int32, (DEC_SEQ, 1), 0)
    cend = lax.broadcasted_iota(jnp.int32, (DEC_SEQ, S_CHUNKS), 1) * CMP_STRIDE + CMP_BLOCK - 1
    cmask = (cend <= qpos)[None]
    imps = []
    for kv in range(N_KV):
        k_c = _compress_blocks(chunks[2 * kv], w1r_ref[0], pb_ref[0], w2_ref[0]).astype(BF16)
        v_c = _compress_blocks(chunks[2 * kv + 1], w1r_ref[1], pb_ref[1], w2_ref[1]).astype(BF16)
        qs = _sample_q_rows(q, kv).astype(BF16)
        s = (_dot_nt(qs, k_c) * ATTN_SCALE).reshape(N_G, DEC_SEQ, S_CHUNKS)
        p = _masked_probs(s, cmask)
        o = _dot(p.reshape(N_G * DEC_SEQ, S_CHUNKS).astype(BF16), v_c)
        for g in range(N_G):
            h = kv * N_G + g
            ocmp_ref[0, :, h * HEAD_DIM:(h + 1) * HEAD_DIM] = o[g * DEC_SEQ:(g + 1) * DEC_SEQ]
        imps.append(jnp.sum(p, axis=0))
    imp = jnp.concatenate(imps, axis=0)

    score = _dot_nt(imp, _chunk_score_matrix_t(S_SEL, S_CHUNKS), precision=lax.Precision.HIGHEST)
    shape = score.shape
    jj = lax.broadcasted_iota(jnp.int32, shape, 1)
    tt = PAST_LEN + lax.broadcasted_iota(jnp.int32, shape, 0) % DEC_SEQ
    cur = tt // SEL_BLOCK
    forced = (jj == 0) | (jj == cur) | (jj == cur - 1)
    sc = jnp.where(jj * SEL_BLOCK <= tt, jnp.where(forced, FORCE_SCORE, score), NEG_INF)

    sc_a = sc[:, :LANES]
    extra = jnp.sum(jnp.where(jj == LANES, sc, 0.0), axis=1, keepdims=True)
    lane = lax.broadcasted_iota(jnp.int32, sc_a.shape, 1)
    rank = jnp.where(extra > sc_a, 1.0, 0.0)
    for k in range(1, LANES):
        r = pltpu.roll(sc_a, k, 1)
        rank = rank + jnp.where(r > sc_a, 1.0, jnp.where(r == sc_a, jnp.where(lane >= k, 1.0, 0.0), 0.0))
    rank_x = jnp.sum(jnp.where(sc_a >= extra, 1.0, 0.0), axis=1, keepdims=True)
    sel_ref[0, :, :LANES] = jnp.where(rank < SEL_TOP, 1.0, 0.0)
    sel_ref[0, :, LANES:] = jnp.where((lane == 0) & (rank_x < SEL_TOP), 1.0, 0.0)


def _nsa_sample_cmp(ia, page_table, cache, new_rows, q, w1r, pb, w2):
    n = q.shape[0]
    grid_spec = pltpu.PrefetchScalarGridSpec(
        num_scalar_prefetch=1,
        grid=(n,),
        in_specs=[pl.BlockSpec((1, DEC_SEQ, KV_ROW), lambda a, pt: (a, 0, 0)),
                  pl.BlockSpec((1, DEC_SEQ, D_MODEL), lambda a, pt: (a, 0, 0)),
                  pl.BlockSpec((2, CHUNK_W, 2 * CMP_HIDDEN), lambda a, pt: (0, 0, 0)),
                  pl.BlockSpec((2, 1, CMP_HIDDEN), lambda a, pt: (0, 0, 0)),
                  pl.BlockSpec((2, CMP_HIDDEN, HEAD_DIM), lambda a, pt: (0, 0, 0)),
                  pl.BlockSpec(memory_space=pl.ANY)],
        out_specs=[pl.BlockSpec((1, DEC_SEQ, D_MODEL), lambda a, pt: (a, 0, 0)),
                   pl.BlockSpec((1, N_KV * DEC_SEQ, S_SEL), lambda a, pt: (a, 0, 0))],
        scratch_shapes=[pltpu.VMEM((2, KV_ROW, PAST_LEN), F32),
                        pltpu.VMEM((N_KV, SLABS, S_CHUNKS * SUBLANES, LANES), F32),
                        pltpu.VMEM((2 * N_KV, S_CHUNKS, CHUNK_W), BF16),
                        pltpu.SemaphoreType.DMA((2,))])
    return pl.pallas_call(
        functools.partial(_nsa_sample_cmp_kernel, ia=ia),
        grid_spec=grid_spec,
        out_shape=[jax.ShapeDtypeStruct((n, DEC_SEQ, D_MODEL), F32),
                   jax.ShapeDtypeStruct((n, N_KV * DEC_SEQ, S_SEL), F32)],
        compiler_params=_cparams(("arbitrary",)),
        name="nsa_sample_cmp",
    )(page_table, new_rows, q, w1r, pb, w2, cache)


def _nsa_sample_slc_kernel(pt_ref, new_ref, qr_ref, sel_ref, wbuf_ref, wnew_ref, gate_ref, ocmp_ref, cache_hbm,
                           o_ref, pages, pages_bf, sem, *, ia):
    slot = _gather_pages(cache_hbm, ia, pt_ref, pages, sem)
    pages_bf[...] = pages[slot].astype(BF16)

    nq = N_KV * N_G * DEC_SEQ
    qbd = (_sample_q_blockdiag(qr_ref[0]) * ATTN_SCALE).astype(BF16)
    qpos = PAST_LEN + lax.broadcasted_iota(jnp.int32, (DEC_SEQ, 1), 0)

    past = pages_bf[...]
    new = new_ref[0].astype(BF16)
    s_p = _dot(qbd, past).reshape(N_KV, N_G, DEC_SEQ, PAST_LEN)
    s_n = _dot_nt(qbd, new).reshape(N_KV, N_G, DEC_SEQ, DEC_SEQ)
    sel = sel_ref[0]
    sel_bf = sel.astype(BF16)
    kw = PAST_LEN // 4
    chosen = []
    for c in range(4):
        kp = c * kw + lax.broadcasted_iota(jnp.int32, (S_SEL, kw), 1)
        expand = jnp.where(kp // SEL_BLOCK == lax.broadcasted_iota(jnp.int32, (S_SEL, kw), 0), 1.0, 0.0)
        chosen.append(_dot(sel_bf, expand.astype(BF16)))
    chosen = jnp.concatenate(chosen, axis=1).reshape(N_KV, 1, DEC_SEQ, PAST_LEN)
    bias_p = jnp.where(chosen > 0.5, 0.0, NEG_INF)
    new_blk = sel[:, PAST_LEN // SEL_BLOCK:PAST_LEN // SEL_BLOCK + 1].reshape(N_KV, 1, DEC_SEQ, 1)
    npos = PAST_LEN + lax.broadcasted_iota(jnp.int32, (DEC_SEQ, DEC_SEQ), 1)
    bias_n = jnp.where((new_blk > 0.5) & (npos <= qpos)[None, None], 0.0, NEG_INF)
    s_p = s_p + bias_p
    s_n = s_n + bias_n
    m = jnp.maximum(jnp.max(s_p, axis=-1, keepdims=True), jnp.max(s_n, axis=-1, keepdims=True))
    e_p = jnp.exp(s_p - m)
    e_n = jnp.exp(s_n - m)
    den = jnp.sum(e_p, axis=-1, keepdims=True) + jnp.sum(e_n, axis=-1, keepdims=True)
    o_slc = (_dot_nt(e_p.reshape(nq, PAST_LEN).astype(BF16), past)
             + _dot(e_n.reshape(nq, DEC_SEQ).astype(BF16), new)) / jnp.maximum(den.reshape(nq, 1), TINY)

    wrows = jnp.concatenate([wbuf_ref[0], wnew_ref[0]], axis=0).astype(BF16)
    nw = A_WINDOW + DEC_SEQ
    sw = _dot_nt(qbd, wrows).reshape(N_KV * N_G, DEC_SEQ, nw)
    d = qpos - (PAST_LEN - A_WINDOW + lax.broadcasted_iota(jnp.int32, (DEC_SEQ, nw), 1))
    pw = _masked_probs(sw, ((d >= 0) & (d <= A_WINDOW))[None])
    o_win = _dot(pw.reshape(nq, nw).astype(BF16), wrows)

    gt = jax.nn.sigmoid(gate_ref[0])
    o_cmp = ocmp_ref[0]
    for kv in range(N_KV):
        v0 = kv * 2 * HEAD_DIM + HEAD_DIM
        for g in range(N_G):
            h = kv * N_G + g
            r0 = h * DEC_SEQ
            c0 = kv * LANES + g
            og = (gt[:, c0:c0 + 1] * o_cmp[:, h * HEAD_DIM:(h + 1) * HEAD_DIM]
                  + gt[:, c0 + N_G:c0 + N_G + 1] * o_slc[r0:r0 + DEC_SEQ, v0:v0 + HEAD_DIM]
                  + gt[:, c0 + 2 * N_G:c0 + 2 * N_G + 1] * o_win[r0:r0 + DEC_SEQ, v0:v0 + HEAD_DIM])
            o_ref[0, :, h * HEAD_DIM:(h + 1) * HEAD_DIM] = og


def _nsa_sample_slc(ia, page_table, cache, z, q_rot, sel, win_cache, o_cmp):
    n = z.shape[0]
    row_blk = lambda base: pl.BlockSpec((1, DEC_SEQ, KV_ROW), lambda a, pt: (a, 0, base // KV_ROW))
    grid_spec = pltpu.PrefetchScalarGridSpec(
        num_scalar_prefetch=1,
        grid=(n,),
        in_specs=[row_blk(A_SLC0),
                  pl.BlockSpec((1, DEC_SEQ, D_MODEL), lambda a, pt: (a, 0, 0)),
                  pl.BlockSpec((1, N_KV * DEC_SEQ, S_SEL), lambda a, pt: (a, 0, 0)),
                  pl.BlockSpec((None, 1, A_WINDOW, KV_ROW), lambda a, pt: (ia, a, 0, 0)),
                  row_blk(A_WIN0), row_blk(A_GATE0),
                  pl.BlockSpec((1, DEC_SEQ, D_MODEL), lambda a, pt: (a, 0, 0)),
                  pl.BlockSpec(memory_space=pl.ANY)],
        out_specs=pl.BlockSpec((1, DEC_SEQ, D_MODEL), lambda a, pt: (a, 0, 0)),
        scratch_shapes=[pltpu.VMEM((2, KV_ROW, PAST_LEN), F32),
                        pltpu.VMEM((KV_ROW, PAST_LEN), BF16),
                        pltpu.SemaphoreType.DMA((2,))])
    return pl.pallas_call(
        functools.partial(_nsa_sample_slc_kernel, ia=ia),
        grid_spec=grid_spec,
        out_shape=jax.ShapeDtypeStruct((n, DEC_SEQ, D_MODEL), F32),
        compiler_params=_cparams(("arbitrary",)),
        name="nsa_sample_slc",
    )(page_table, z, q_rot, sel, win_cache, z, z, o_cmp, cache)


def _swa_sample_kernel(q_ref, new_ref, wbuf_ref, sink_ref, o_ref):
    nq = N_KV * N_G * DEC_SEQ
    qbd = _sample_q_blockdiag(q_ref[0]).astype(BF16)
    rows = jnp.concatenate([wbuf_ref[0], new_ref[0]], axis=0).astype(BF16)
    nw = B_WINDOW + DEC_SEQ
    s = (_dot_nt(qbd, rows) * ATTN_SCALE).reshape(N_KV * N_G, DEC_SEQ, nw)
    qpos = PAST_LEN + lax.broadcasted_iota(jnp.int32, (DEC_SEQ, 1), 0)
    d = qpos - (PAST_LEN - B_WINDOW + lax.broadcasted_iota(jnp.int32, (DEC_SEQ, nw), 1))
    mask = ((d >= 0) & (d <= B_WINDOW))[None]
    sink = sink_ref[...]
    sm = jnp.where(mask, s, NEG_INF)
    m = jnp.maximum(jnp.max(sm, axis=-1, keepdims=True), sink)
    e = jnp.where(mask, jnp.exp(sm - m), 0.0)
    den = jnp.sum(e, axis=-1, keepdims=True) + jnp.exp(sink - m)
    p = e / jnp.maximum(den, TINY)
    o = _dot(p.reshape(nq, nw).astype(BF16), rows)
    for kv in range(N_KV):
        v0 = kv * 2 * HEAD_DIM + HEAD_DIM
        for g in range(N_G):
            h = kv * N_G + g
            o_ref[0, :, h * HEAD_DIM:(h + 1) * HEAD_DIM] = o[h * DEC_SEQ:(h + 1) * DEC_SEQ, v0:v0 + HEAD_DIM]


def _swa_sample_attend(ib, z, win_cache, sinks):
    n = z.shape[0]
    return pl.pallas_call(
        _swa_sample_kernel,
        grid=(n,),
        in_specs=[pl.BlockSpec((1, DEC_SEQ, D_MODEL), lambda a: (a, 0, 0)),
                  pl.BlockSpec((1, DEC_SEQ, KV_ROW), lambda a: (a, 0, D_MODEL // KV_ROW)),
                  pl.BlockSpec((None, 1, B_WINDOW, KV_ROW), lambda a: (ib, a, 0, 0)),
                  pl.BlockSpec((N_HEADS, 1, 1), lambda a: (0, 0, 0))],
        out_specs=pl.BlockSpec((1, DEC_SEQ, D_MODEL), lambda a: (a, 0, 0)),
        out_shape=jax.ShapeDtypeStruct((n, DEC_SEQ, D_MODEL), F32),
        compiler_params=_cparams(("arbitrary",)),
        name="swa_sample_attend",
    )(z, z, win_cache, sinks.reshape(N_HEADS, 1, 1))


def _nsa_col_perm():
    perm = np.full((A_COLS,), -1, np.int64)
    perm[:D_MODEL] = np.arange(D_MODEL)
    kvw = N_KV * HEAD_DIM
    for br, base in enumerate((A_CMP0, A_SLC0, A_WIN0)):
        old0 = D_MODEL + 2 * kvw * br
        for kv in range(N_KV):
            for i in range(2):
                dst = base + kv * 2 * HEAD_DIM + i * HEAD_DIM
                src = old0 + i * kvw + kv * HEAD_DIM
                perm[dst:dst + HEAD_DIM] = np.arange(src, src + HEAD_DIM)
    old_g = D_MODEL + 6 * kvw
    for kv in range(N_KV):
        for br in range(3):
            dst = A_GATE0 + kv * LANES + br * N_G
            src = old_g + br * N_HEADS + kv * N_G
            perm[dst:dst + N_G] = np.arange(src, src + N_G)
    return perm


def _swa_col_perm():
    perm = np.zeros((B_COLS,), np.int64)
    perm[:D_MODEL] = np.arange(D_MODEL)
    kvw = N_KV * HEAD_DIM
    for kv in range(N_KV):
        for i in range(2):
            dst = D_MODEL + kv * 2 * HEAD_DIM + i * HEAD_DIM
            src = D_MODEL + i * kvw + kv * HEAD_DIM
            perm[dst:dst + HEAD_DIM] = np.arange(src, src + HEAD_DIM)
    return perm


def _permute_cols(w, perm):
    cols = jnp.take(w, jnp.asarray(np.maximum(perm, 0)), axis=-1)
    return jnp.where(jnp.asarray(perm >= 0), cols, 0.0)


def _rows_view(z, c0, n, t):
    return z[..., c0:c0 + KV_ROW].reshape(n, t, N_KV, 2, HEAD_DIM)


def _trunk(x, ada, past, p, prompt):
    g, t, _ = x.shape
    tm = min(512, t)
    if prompt:
        pos = jnp.arange(t, dtype=jnp.int32)
        n_seq, t_seq = g, t
    else:
        n_seq, t_seq = t // DEC_SEQ, DEC_SEQ
        pos = jnp.tile(PAST_LEN + jnp.arange(DEC_SEQ, dtype=jnp.int32), n_seq)
    tab = _rope_tables(pos)[None]
    cmp_out, slc_out, nwin_out, swin_out = [], [], [], []
    for l in range(DEPTH):
        sh1, sc1, g1, sh2, sc2, g2 = [ada[l][..., k * D_MODEL:(k + 1) * D_MODEL] for k in range(6)]
        i = l // 2
        if l % 2 == 0:
            z, q_rot = _proj(x, sc1, sh1, p['nsa_w_in'][i], p['zero_a'], tab, tm, (A_SLC0, A_WIN0), True)
            cmp_rows = _rows_view(z, A_CMP0, n_seq, t_seq)
            slc_rows = _rows_view(z, A_SLC0, n_seq, t_seq)
            win_rows = _rows_view(z, A_WIN0, n_seq, t_seq)
            w1r, pb, w2 = p['cmp_w1r'][i], p['cmp_pb'][i], p['cmp_w2'][i]
            if prompt:
                nch = t // CMP_STRIDE
                chunks = z[..., A_CMP0:A_CMP0 + KV_ROW].reshape(g, nch, CMP_STRIDE, 2 * N_KV, HEAD_DIM)
                chunks = chunks.transpose(0, 3, 1, 2, 4).reshape(g, 2 * N_KV, nch, CHUNK_W)
                cblocks = _compress_prompt(chunks, w1r, pb, w2)
                o = _nsa_prompt_attend(z, q_rot, cblocks)
                nwin_out.append(win_rows[:, -min(A_WINDOW, t):])
            else:
                cache_cmp, cache_slc, cache_nwin, _, page_table = past
                zs = z.reshape(n_seq, DEC_SEQ, A_COLS)
                qs = zs[..., :D_MODEL]
                o_cmp, sel = _nsa_sample_cmp(i, page_table, cache_cmp, zs[..., A_CMP0:A_CMP0 + KV_ROW], qs, w1r, pb, w2)
                o = _nsa_sample_slc(i, page_table, cache_slc, zs, q_rot.reshape(n_seq, DEC_SEQ, D_MODEL), sel,
                                    cache_nwin, o_cmp).reshape(g, t, D_MODEL)
                nwin_out.append(jnp.concatenate([cache_nwin[i].reshape(n_seq, A_WINDOW, N_KV, 2, HEAD_DIM)[:, DEC_SEQ:],
                                                 win_rows], axis=1))
            cmp_out.append(cmp_rows)
            slc_out.append(slc_rows)
            w_o, b_o = p['nsa_w_o'][i], p['zero_d']
        else:
            (z,) = _proj(x, sc1, sh1, p['swa_w_in'][i], p['swa_b_in'][i], tab, tm, (D_MODEL,), False)
            rows = _rows_view(z, D_MODEL, n_seq, t_seq)
            if prompt:
                o = _swa_prompt_attend(z, p['swa_sinks'][i])
                swin_out.append(rows[:, -min(B_WINDOW, t):])
            else:
                cache_swin = past[3]
                o = _swa_sample_attend(i, z.reshape(n_seq, DEC_SEQ, B_COLS), cache_swin, p['swa_sinks'][i]).reshape(g, t, D_MODEL)
                swin_out.append(jnp.concatenate([cache_swin[i].reshape(n_seq, B_WINDOW, N_KV, 2, HEAD_DIM)[:, DEC_SEQ:],
                                                 rows], axis=1))
            w_o, b_o = p['swa_w_o'][i], p['swa_b_o'][i]
        x = _outproj_ln(o, x, g1, w_o, b_o, p['ln_g'][l, 0], p['ln_b'][l, 0], tm)
        x = _moe_ln(x, sc2, sh2, g2, p['router_wt'], p['router_b'], p['moe_w_gate'][l], p['moe_w_up'][l],
                    p['moe_w_down'][l], p['ln_g'][l, 1], p['ln_b'][l, 1], tm)
    return x, (jnp.stack(cmp_out), jnp.stack(slc_out), jnp.stack(nwin_out), jnp.stack(swin_out))


def kernel(x_prompt, x_sample, cache_nsa_cmp, cache_nsa_slc, cache_nsa_win, cache_swa_win, page_table, c_prompt, c_sample, nsa_w_in, nsa_cmp_pe, nsa_cmp_w1, nsa_cmp_b1, nsa_cmp_w2, nsa_w_o, swa_w_in, swa_b_in, swa_sinks, swa_w_o, swa_b_o, ada_w, ada_b, ln_g, ln_b, router_w, router_b, moe_w_gate, moe_w_up, moe_w_down):
    n_a = nsa_w_in.shape[0]
    batch, seq, _ = x_prompt.shape
    n_dec = x_sample.shape[0]
    n_pool = cache_nsa_cmp.shape[1]

    pe_term = jnp.stack([_pe_term(nsa_cmp_pe[i], nsa_cmp_w1[i]) for i in range(n_a)])
    w1r = nsa_cmp_w1.reshape(n_a, 2, CMP_SHIFTS, CHUNK_W, CMP_HIDDEN).transpose(0, 1, 3, 2, 4)
    p = {
        'nsa_w_in': _permute_cols(nsa_w_in, _nsa_col_perm()).astype(BF16),
        'zero_a': jnp.zeros((1, A_COLS), F32),
        'zero_d': jnp.zeros((1, D_MODEL), F32),
        'cmp_w1r': w1r.reshape(n_a, 2, CHUNK_W, CMP_SHIFTS * CMP_HIDDEN).astype(BF16),
        'cmp_pb': (pe_term + nsa_cmp_b1).reshape(n_a, 2, 1, CMP_HIDDEN),
        'cmp_w2': nsa_cmp_w2.astype(BF16),
        'nsa_w_o': nsa_w_o.astype(BF16),
        'swa_w_in': _permute_cols(swa_w_in, _swa_col_perm()).astype(BF16),
        'swa_b_in': _permute_cols(swa_b_in, _swa_col_perm())[:, None, :],
        'swa_sinks': swa_sinks,
        'swa_w_o': swa_w_o.astype(BF16),
        'swa_b_o': swa_b_o[:, None, :],
        'ln_g': ln_g[:, :, None, :],
        'ln_b': ln_b[:, :, None, :],
        'router_wt': router_w.T,
        'router_b': router_b[:, None],
        'moe_w_gate': moe_w_gate.astype(BF16),
        'moe_w_up': moe_w_up.astype(BF16),
        'moe_w_down': moe_w_down.astype(BF16),
    }

    ada = _ada(jnp.concatenate([c_prompt, c_sample], axis=0), ada_w, ada_b)
    ada_p = ada[:, :batch, None, :]
    ada_s = jnp.broadcast_to(ada[:, batch:, None, :], (DEPTH, n_dec, DEC_SEQ, 6 * D_MODEL))
    ada_s = ada_s.reshape(DEPTH, 1, n_dec * DEC_SEQ, 6 * D_MODEL)

    y_p, (cmp_p, slc_p, nwin_p, swin_p) = _trunk(x_prompt, ada_p, None, p, True)
    pages = lambda c: c.transpose(0, 1, 3, 4, 5, 2).reshape(n_a, n_pool, KV_ROW, PAGE_SIZE)
    past = (pages(cache_nsa_cmp), pages(cache_nsa_slc),
            cache_nsa_win.reshape(n_a, n_dec, A_WINDOW, KV_ROW),
            cache_swa_win.reshape(cache_swa_win.shape[0], n_dec, B_WINDOW, KV_ROW),
            page_table)
    y_s, (cmp_s, slc_s, nwin_s, swin_s) = _trunk(x_sample.reshape(1, n_dec * DEC_SEQ, D_MODEL), ada_s, past, p, False)
    return (y_p, y_s.reshape(n_dec, DEC_SEQ, D_MODEL), cmp_p, slc_p, nwin_p, swin_p, cmp_s, slc_s, nwin_s, swin_s)
```

```python
import functools
import math

import numpy as np
import jax
import jax.numpy as jnp
from jax import lax
from jax.experimental import pallas as pl
from jax.experimental.pallas import tpu as pltpu

D_MODEL = 1024
SEQ = 2048
DEPTH = 4
DEC_SEQ = 8
PAST_LEN = 8192
PAGE_SIZE = 128
N_PAGES = PAST_LEN // PAGE_SIZE

HEAD_DIM = 64
ROT_DIM = HEAD_DIM // 4
ROT_HALF = ROT_DIM // 2
ROPE_THETA = 500000.0
ATTN_SCALE = HEAD_DIM ** -0.5

N_HEADS = D_MODEL // HEAD_DIM
N_KV = 2
N_G = N_HEADS // N_KV
KV_ROW = N_KV * 2 * HEAD_DIM
CMP_BLOCK = 32
CMP_STRIDE = 16
CMP_HIDDEN = 4 * HEAD_DIM
CMP_SHIFTS = CMP_BLOCK // CMP_STRIDE
CHUNK_W = CMP_STRIDE * HEAD_DIM
SEL_BLOCK = 64
SEL_TOP = 16
CHUNKS_PER_SEL = SEL_BLOCK // CMP_STRIDE
A_WINDOW = 512
B_WINDOW = 128
FORCE_SCORE = 1e9

N_EXPERTS = 16
N_GROUPS = 4
EXPERTS_PER_GROUP = N_EXPERTS // N_GROUPS
D_FF_EXPERT = D_MODEL // 2

DN_ALPHA = (2.0 * DEPTH) ** 0.25
LN_EPS = 1e-5
NEG_INF = -1e30
TINY = 1e-30
LOG2E = math.log2(math.e)

LANES = 128
SUBLANES = 8
MOE_TILE = 1024
MOE_SUB_TILE = 256
A_COLS = 2048
A_CMP0, A_SLC0, A_WIN0, A_GATE0 = 1024, 1280, 1536, 1792
B_COLS = D_MODEL + KV_ROW
VMEM_LIMIT = 56 * 1024 * 1024

F32 = jnp.float32
BF16 = jnp.bfloat16


def _cparams(sem):
    return pltpu.CompilerParams(dimension_semantics=sem, vmem_limit_bytes=VMEM_LIMIT)


def _dot(a, b):
    return jnp.dot(a, b, preferred_element_type=F32)


def _dot_nt(a, b, precision=None):
    return lax.dot_general(a, b, (((1,), (1,)), ((), ())), precision=precision,
                           preferred_element_type=F32)


def _masked_probs(s, mask):
    s = jnp.where(mask, s, NEG_INF)
    m = jnp.max(s, axis=-1, keepdims=True)
    e = jnp.where(mask, jnp.exp(s - m), 0.0)
    den = jnp.sum(e, axis=-1, keepdims=True)
    return e / jnp.maximum(den, TINY)


def _layer_norm(r, g, b):
    mu = jnp.mean(r, axis=-1, keepdims=True)
    rc = r - mu
    var = jnp.mean(rc * rc, axis=-1, keepdims=True)
    return rc * lax.rsqrt(var + LN_EPS) * g + b


def _ada_kernel(c_ref, w_ref, b_ref, o_ref):
    c = c_ref[...]
    a = (c * jax.nn.sigmoid(c)).astype(BF16)
    o_ref[0] = _dot(a, w_ref[0].astype(BF16)) + b_ref[0]


def _ada(c_all, ada_w, ada_b):
    m = c_all.shape[0]
    n = ada_w.shape[-1]
    tn = 1536
    return pl.pallas_call(
        _ada_kernel,
        grid=(DEPTH, n // tn),
        in_specs=[pl.BlockSpec((m, D_MODEL), lambda l, j: (0, 0)),
                  pl.BlockSpec((1, D_MODEL, tn), lambda l, j: (l, 0, j)),
                  pl.BlockSpec((1, 1, tn), lambda l, j: (l, 0, j))],
        out_specs=pl.BlockSpec((1, m, tn), lambda l, j: (l, 0, j)),
        out_shape=jax.ShapeDtypeStruct((DEPTH, m, n), F32),
        compiler_params=_cparams(("arbitrary", "arbitrary")),
        name="ada",
    )(c_all, ada_w, ada_b.reshape(DEPTH, 1, n))


def _pe_term_kernel(pe_ref, w_ref, o_ref):
    o_ref[0] = _dot(pe_ref[0].astype(BF16), w_ref[0].astype(BF16))


def _pe_term(pe, w1):
    k = CMP_BLOCK * HEAD_DIM
    pe8 = jnp.broadcast_to(pe.reshape(2, 1, k), (2, 8, k))
    out = pl.pallas_call(
        _pe_term_kernel,
        grid=(2,),
        in_specs=[pl.BlockSpec((1, 8, k), lambda i: (i, 0, 0)),
                  pl.BlockSpec((1, k, CMP_HIDDEN), lambda i: (i, 0, 0))],
        out_specs=pl.BlockSpec((1, 8, CMP_HIDDEN), lambda i: (i, 0, 0)),
        out_shape=jax.ShapeDtypeStruct((2, 8, CMP_HIDDEN), F32),
        compiler_params=_cparams(("arbitrary",)),
        name="pe_term",
    )(pe8, w1)
    return out[:, 0]


def _rope_tables(pos):
    inv = jnp.exp(-math.log(ROPE_THETA) * jnp.arange(ROT_HALF, dtype=F32) / ROT_HALF)
    ang = pos.astype(F32)[:, None] * inv[None, :]
    cos, sin = jnp.cos(ang), jnp.sin(ang)
    t = pos.shape[0]
    one = jnp.ones((t, HEAD_DIM - ROT_DIM), F32)
    zero = jnp.zeros((t, HEAD_DIM - ROT_DIM), F32)
    zh = jnp.zeros((t, ROT_HALF), F32)
    cos_h = jnp.concatenate([cos, cos, one], axis=1)
    sa_h = jnp.concatenate([-sin, zh, zero], axis=1)
    sb_h = jnp.concatenate([zh, sin, zero], axis=1)
    id_h = jnp.ones((t, HEAD_DIM), F32)
    z_h = jnp.zeros((t, HEAD_DIM), F32)
    return jnp.concatenate([cos_h, cos_h, sa_h, sa_h, sb_h, sb_h,
                            cos_h, id_h, sa_h, z_h, sb_h, z_h], axis=1)


def _rope_apply(seg, cos, sa, sb):
    w = seg.shape[1]
    reps = w // LANES
    c = jnp.tile(cos, (1, reps))
    a = jnp.tile(sa, (1, reps))
    b = jnp.tile(sb, (1, reps))
    return seg * c + pltpu.roll(seg, w - ROT_HALF, 1) * a + pltpu.roll(seg, ROT_HALF, 1) * b


def _proj_kernel(x_ref, sc_ref, sh_ref, w_ref, b_ref, tab_ref, z_ref, *qrot_ref, kv_segs):
    x = x_ref[0]
    h = x * (1.0 + sc_ref[0]) + sh_ref[0]
    z = _dot(h.astype(BF16), w_ref[...]) + b_ref[...]
    tab = tab_ref[0]
    tq = [tab[:, i * LANES:(i + 1) * LANES] for i in range(3)]
    tk = [tab[:, i * LANES:(i + 1) * LANES] for i in range(3, 6)]
    q_rot = _rope_apply(z[:, :D_MODEL], *tq)
    z_ref[0] = z
    if qrot_ref:
        qrot_ref[0][0] = q_rot
    else:
        z_ref[0, :, :D_MODEL] = q_rot
    for c0 in kv_segs:
        z_ref[0, :, c0:c0 + KV_ROW] = _rope_apply(z[:, c0:c0 + KV_ROW], *tk)


def _proj(x, sc, sh, w, b, tab, tm, kv_segs, sep_qrot):
    g, t, _ = x.shape
    n = w.shape[1]
    tmod = tm if sc.shape[1] == t else 1
    out_shape = [jax.ShapeDtypeStruct((g, t, n), F32)]
    out_specs = [pl.BlockSpec((1, tm, n), lambda a, i: (a, i, 0))]
    if sep_qrot:
        out_shape.append(jax.ShapeDtypeStruct((g, t, D_MODEL), F32))
        out_specs.append(pl.BlockSpec((1, tm, D_MODEL), lambda a, i: (a, i, 0)))
    mod_spec = pl.BlockSpec((1, tmod, D_MODEL), (lambda a, i: (a, i, 0)) if tmod == tm else (lambda a, i: (a, 0, 0)))
    return pl.pallas_call(
        functools.partial(_proj_kernel, kv_segs=kv_segs),
        grid=(g, t // tm),
        in_specs=[pl.BlockSpec((1, tm, D_MODEL), lambda a, i: (a, i, 0)),
                  mod_spec, mod_spec,
                  pl.BlockSpec((D_MODEL, n), lambda a, i: (0, 0)),
                  pl.BlockSpec((1, n), lambda a, i: (0, 0)),
                  pl.BlockSpec((1, tm, 6 * LANES), lambda a, i: (0, i, 0))],
        out_specs=out_specs,
        out_shape=out_shape,
        compiler_params=_cparams(("arbitrary", "arbitrary")),
        name="proj",
    )(x, sc, sh, w, b, tab)


def _outproj_kernel(o_ref, x_ref, g_ref, w_ref, b_ref, lg_ref, lb_ref, out_ref):
    y = _dot(o_ref[0].astype(BF16), w_ref[...]) + b_ref[...]
    r = DN_ALPHA * x_ref[0] + g_ref[0] * y
    out_ref[0] = _layer_norm(r, lg_ref[...], lb_ref[...])


def _outproj_ln(o, x, gate, w, b, lg, lb, tm):
    g, t, _ = x.shape
    tmod = tm if gate.shape[1] == t else 1
    row = pl.BlockSpec((1, tm, D_MODEL), lambda a, i: (a, i, 0))
    vec = pl.BlockSpec((1, D_MODEL), lambda a, i: (0, 0))
    mod_spec = pl.BlockSpec((1, tmod, D_MODEL), (lambda a, i: (a, i, 0)) if tmod == tm else (lambda a, i: (a, 0, 0)))
    return pl.pallas_call(
        _outproj_kernel,
        grid=(g, t // tm),
        in_specs=[row, row, mod_spec,
                  pl.BlockSpec((D_MODEL, D_MODEL), lambda a, i: (0, 0)), vec, vec, vec],
        out_specs=row,
        out_shape=jax.ShapeDtypeStruct((g, t, D_MODEL), F32),
        compiler_params=_cparams(("arbitrary", "arbitrary")),
        name="outproj_ln",
    )(o, x, gate, w, b, lg, lb)


def _router_gates_t(h, rwt_ref, rb_ref):
    logits = _dot_nt(rwt_ref[...], h, precision=lax.Precision.HIGHEST)
    m = jnp.max(logits, axis=0, keepdims=True)
    ex = jnp.exp(logits - m)
    probs = ex / jnp.sum(ex, axis=0, keepdims=True)
    biased = probs + rb_ref[...]
    eidx = lax.broadcasted_iota(jnp.int32, biased.shape, 0)
    best = None
    g_sel = None
    for g in range(N_GROUPS):
        rows = [biased[g * EXPERTS_PER_GROUP + i:g * EXPERTS_PER_GROUP + i + 1] for i in range(EXPERTS_PER_GROUP)]
        gs = None
        for i in range(EXPERTS_PER_GROUP):
            for j in range(i + 1, EXPERTS_PER_GROUP):
                pair = rows[i] + rows[j]
                gs = pair if gs is None else jnp.maximum(gs, pair)
        if best is None:
            best, g_sel = gs, jnp.zeros(gs.shape, jnp.int32)
        else:
            upd = gs > best
            g_sel = jnp.where(upd, g, g_sel)
            best = jnp.maximum(best, gs)
    v = jnp.where(eidx // EXPERTS_PER_GROUP == g_sel, biased, NEG_INF)
    m1 = jnp.max(v, axis=0, keepdims=True)
    i1 = jnp.min(jnp.where(v == m1, eidx, N_EXPERTS), axis=0, keepdims=True)
    v2 = jnp.where(eidx == i1, -jnp.inf, v)
    m2 = jnp.max(v2, axis=0, keepdims=True)
    i2 = jnp.min(jnp.where(v2 == m2, eidx, N_EXPERTS), axis=0, keepdims=True)
    w1 = jnp.sum(jnp.where(eidx == i1, probs, 0.0), axis=0, keepdims=True)
    w2 = jnp.sum(jnp.where(eidx == i2, probs, 0.0), axis=0, keepdims=True)
    tot = w1 + w2
    return jnp.where(eidx == i1, w1 / tot, 0.0) + jnp.where(eidx == i2, w2 / tot, 0.0), g_sel


def _moe_kernel(x_ref, sc_ref, sh_ref, g_ref, rwt_ref, rb_ref, wg_ref, wu_ref, wd_ref, lg_ref, lb_ref,
                out_ref, hs_scr, gate_scr, ys_scr, unperm_scr, seg_smem, *, sub):
    e = pl.program_id(2)
    tm = hs_scr.shape[0]

    @pl.when(e == 0)
    def _():
        h = x_ref[0] * (1.0 + sc_ref[0]) + sh_ref[0]
        gates_t, g_sel = _router_gates_t(h, rwt_ref, rb_ref)
        grp = lax.broadcasted_iota(jnp.int32, (SUBLANES, tm), 0)
        onehot = jnp.where(grp == g_sel, 1.0, 0.0)
        src = lax.broadcasted_iota(jnp.int32, (tm, tm), 0)
        dst = lax.broadcasted_iota(jnp.int32, (tm, tm), 1)
        earlier = jnp.where(src < dst, 1.0, 0.0).astype(BF16)
        cum = _dot(onehot.astype(BF16), earlier)
        pos = jnp.zeros((1, tm), F32)
        off = jnp.int32(0)
        for r in range(N_GROUPS):
            cnt = jnp.sum(onehot[r:r + 1, :]).astype(jnp.int32)
            seg_smem[r] = off
            seg_smem[N_GROUPS + r] = cnt
            pos = pos + onehot[r:r + 1, :] * (cum[r:r + 1, :] + off.astype(F32))
            off = off + cnt
        perm = jnp.where(src == pos.astype(jnp.int32), 1.0, 0.0)
        hs_scr[...] = _dot(perm.astype(BF16), h.astype(BF16)).astype(BF16)
        gs_t = _dot_nt(gates_t, perm, precision=lax.Precision.HIGHEST)
        pad = jnp.zeros((LANES - N_EXPERTS, tm), F32)
        gate_scr[...] = jnp.concatenate([gs_t, pad], axis=0).T
        unperm_scr[...] = perm.T.astype(BF16)
        ys_scr[...] = jnp.zeros_like(ys_scr)

    grp_e = e // EXPERTS_PER_GROUP
    off = seg_smem[grp_e]
    cnt = seg_smem[N_GROUPS + grp_e]
    first = off // sub
    last = (off + cnt - 1) // sub
    for j in range(tm // sub):
        @pl.when((cnt > 0) & (first <= j) & (j <= last))
        def _():
            rows = hs_scr[j * sub:(j + 1) * sub, :]
            a = _dot(rows, wg_ref[0])
            u = _dot(rows, wu_ref[0])
            gates = gate_scr[j * sub:(j + 1) * sub, :]
            lane = lax.broadcasted_iota(jnp.int32, gates.shape, 1)
            gcol = jnp.sum(jnp.where(lane == e, gates, 0.0), axis=1, keepdims=True)
            act = (a * jax.nn.sigmoid(a)) * u * gcol
            ys_scr[j * sub:(j + 1) * sub, :] += _dot(act.astype(BF16), wd_ref[0])

    @pl.when(e == N_EXPERTS - 1)
    def _():
        ys = ys_scr[...]
        hi = ys.astype(BF16)
        lo = (ys - hi.astype(F32)).astype(BF16)
        unperm = unperm_scr[...]
        y = _dot(unperm, hi) + _dot(unperm, lo)
        r = DN_ALPHA * x_ref[0] + g_ref[0] * y
        out_ref[0] = _layer_norm(r, lg_ref[...], lb_ref[...])


def _moe_ln(x, sc, sh, gate, rwt, rb, wg, wu, wd, lg, lb, tm):
    g, t, _ = x.shape
    sub = min(MOE_SUB_TILE, tm)
    tmod = tm if sc.shape[1] == t else 1
    row = pl.BlockSpec((1, tm, D_MODEL), lambda a, i, e: (a, i, 0))
    vec = pl.BlockSpec((1, D_MODEL), lambda a, i, e: (0, 0))
    mod_spec = pl.BlockSpec((1, tmod, D_MODEL),
                            (lambda a, i, e: (a, i, 0)) if tmod == tm else (lambda a, i, e: (a, 0, 0)))
    return pl.pallas_call(
        functools.partial(_moe_kernel, sub=sub),
        grid=(g, t // tm, N_EXPERTS),
        in_specs=[row, mod_spec, mod_spec, mod_spec,
                  pl.BlockSpec((N_EXPERTS, D_MODEL), lambda a, i, e: (0, 0)),
                  pl.BlockSpec((N_EXPERTS, 1), lambda a, i, e: (0, 0)),
                  pl.BlockSpec((1, D_MODEL, D_FF_EXPERT), lambda a, i, e: (e, 0, 0)),
                  pl.BlockSpec((1, D_MODEL, D_FF_EXPERT), lambda a, i, e: (e, 0, 0)),
                  pl.BlockSpec((1, D_FF_EXPERT, D_MODEL), lambda a, i, e: (e, 0, 0)),
                  vec, vec],
        out_specs=row,
        out_shape=jax.ShapeDtypeStruct((g, t, D_MODEL), F32),
        scratch_shapes=[pltpu.VMEM((tm, D_MODEL), BF16),
                        pltpu.VMEM((tm, LANES), F32),
                        pltpu.VMEM((tm, D_MODEL), F32),
                        pltpu.VMEM((tm, tm), BF16),
                        pltpu.SMEM((2 * N_GROUPS,), jnp.int32)],
        compiler_params=_cparams(("arbitrary", "arbitrary", "arbitrary")),
        name="moe_ln",
    )(x, sc, sh, gate, rwt, rb, wg, wu, wd, lg, lb)


def _compress_blocks(chunks, w1r, pb, w2):
    nch = chunks.shape[0]
    part = _dot(chunks, w1r)
    pre = part[:, :CMP_HIDDEN] + pltpu.roll(part[:, CMP_HIDDEN:], nch - 1, 0)
    hid = jax.nn.gelu(pre + pb)
    return _dot(hid.astype(BF16), w2)


def _compress_kernel(ch_ref, w1r_ref, pb_ref, w2_ref, o_ref):
    for j in range(2 * N_KV):
        i = j % 2
        o_ref[0, j] = _compress_blocks(ch_ref[0, j].astype(BF16), w1r_ref[i], pb_ref[i], w2_ref[i])


def _compress_prompt(chunks, w1r, pb, w2):
    b, _, nch, _ = chunks.shape
    return pl.pallas_call(
        _compress_kernel,
        grid=(b,),
        in_specs=[pl.BlockSpec((1, 2 * N_KV, nch, CHUNK_W), lambda a: (a, 0, 0, 0)),
                  pl.BlockSpec((2, CHUNK_W, 2 * CMP_HIDDEN), lambda a: (0, 0, 0)),
                  pl.BlockSpec((2, 1, CMP_HIDDEN), lambda a: (0, 0, 0)),
                  pl.BlockSpec((2, CMP_HIDDEN, HEAD_DIM), lambda a: (0, 0, 0))],
        out_specs=pl.BlockSpec((1, 2 * N_KV, nch, HEAD_DIM), lambda a: (a, 0, 0, 0)),
        out_shape=jax.ShapeDtypeStruct((b, 2 * N_KV, nch, HEAD_DIM), F32),
        compiler_params=_cparams(("arbitrary",)),
        name="compress_prompt",
    )(chunks, w1r, pb, w2)


def _stack_heads(q):
    return jnp.concatenate([q[:, g * HEAD_DIM:(g + 1) * HEAD_DIM] for g in range(N_G)], axis=0)


def _chunk_score_matrix_t(n_sel, n_cmp):
    j = lax.broadcasted_iota(jnp.int32, (n_sel, n_cmp), 0)
    c = lax.broadcasted_iota(jnp.int32, (n_sel, n_cmp), 1)
    lo = j * CHUNKS_PER_SEL
    hi = lo + CHUNKS_PER_SEL - 1
    m = jnp.zeros((n_sel, n_cmp), F32)
    for r in range(CMP_SHIFTS):
        m = m + jnp.where((c + r >= lo) & (c + r <= hi), 1.0, 0.0)
    return m


def _nsa_prompt_kernel(q_ref, qr_ref, cb_ref, slc_ref, win_ref, gate_ref, o_ref, m_scr, acc_scr, *, tq, kc):
    qi = pl.program_id(2)
    t0 = qi * tq
    n_cmp = cb_ref.shape[2]
    n_sel = SEQ // SEL_BLOCK
    tpos = t0 + lax.broadcasted_iota(jnp.int32, (tq, 1), 0)

    qs = _stack_heads(q_ref[0]).astype(BF16)
    k_c = cb_ref[0, 0].astype(BF16)
    v_c = cb_ref[0, 1].astype(BF16)
    s = (_dot_nt(qs, k_c) * ATTN_SCALE).reshape(N_G, tq, n_cmp)
    cend = lax.broadcasted_iota(jnp.int32, (tq, n_cmp), 1) * CMP_STRIDE + CMP_BLOCK - 1
    p = _masked_probs(s, (cend <= tpos)[None])
    o_cmp = _dot(p.reshape(N_G * tq, n_cmp).astype(BF16), v_c)
    imp = jnp.sum(p, axis=0)

    score_t = _dot_nt(_chunk_score_matrix_t(n_sel, n_cmp), imp, precision=lax.Precision.HIGHEST)
    jj = lax.broadcasted_iota(jnp.int32, (n_sel, tq), 0)
    tt = t0 + lax.broadcasted_iota(jnp.int32, (n_sel, tq), 1)
    cur = tt // SEL_BLOCK
    forced = (jj == 0) | (jj == cur) | (jj == cur - 1)
    sc = jnp.where(jj * SEL_BLOCK <= tt, jnp.where(forced, FORCE_SCORE, score_t), NEG_INF)
    rank = jnp.zeros((n_sel, tq), F32)
    for i in range(n_sel):
        ri = sc[i:i + 1]
        rank = rank + jnp.where(ri > sc, 1.0, jnp.where((ri == sc) & (jj > i), 1.0, 0.0))
    sel_t = jnp.where(rank < SEL_TOP, 1.0, 0.0)
    sel = jnp.concatenate([sel_t, jnp.zeros((LANES - n_sel, tq), F32)], axis=0).T.astype(BF16)

    qrs = (_stack_heads(qr_ref[0]) * (ATTN_SCALE * LOG2E)).astype(BF16)

    m_scr[...] = jnp.full(m_scr.shape, NEG_INF, F32)
    acc_scr[...] = jnp.zeros(acc_scr.shape, F32)

    def body(c, carry):
        k0 = pl.multiple_of(c * kc, kc)
        rows = slc_ref[0, pl.ds(k0, kc), :]
        k = rows[:, :HEAD_DIM].astype(BF16)
        sk = _dot_nt(qrs, k).reshape(N_G, tq, kc)
        kpos = k0 + lax.broadcasted_iota(jnp.int32, (LANES, kc), 1)
        expand = jnp.where(kpos // SEL_BLOCK == lax.broadcasted_iota(jnp.int32, (LANES, kc), 0), 1.0, 0.0)
        chosen = _dot(sel, expand.astype(BF16))
        sm = sk + jnp.where((chosen > 0.5) & (kpos[:1] <= tpos), 0.0, NEG_INF)[None]
        m = m_scr[...]
        m_new = jnp.maximum(m, jnp.max(sm, axis=-1, keepdims=True))
        e = jnp.exp2(sm - m_new)
        pv = _dot(e.reshape(N_G * tq, kc).astype(BF16), _values_and_ones(rows)).reshape(N_G, tq, LANES)
        acc_scr[...] = jnp.exp2(m - m_new) * acc_scr[...] + pv
        m_scr[...] = m_new
        return carry

    lax.fori_loop(0, (t0 + tq + kc - 1) // kc, body, 0)
    acc = acc_scr[...]
    o_slc = acc[:, :, :HEAD_DIM] / jnp.maximum(acc[:, :, HEAD_DIM:HEAD_DIM + 1], TINY)

    o_win = _band_attend(qrs, win_ref, t0, tq, A_WINDOW, None)

    gt = jax.nn.sigmoid(gate_ref[0])
    o_cmp = o_cmp.reshape(N_G, tq, HEAD_DIM)
    for g in range(N_G):
        og = (gt[:, g:g + 1] * o_cmp[g] + gt[:, N_G + g:N_G + g + 1] * o_slc[g]
              + gt[:, 2 * N_G + g:2 * N_G + g + 1] * o_win[g])
        o_ref[0, :, g * HEAD_DIM:(g + 1) * HEAD_DIM] = og


def _band_attend(qrs, rows_ref, t0, tq, window, sinks):
    span = window + tq
    start = pl.multiple_of(jnp.maximum(t0 - window, 0), tq)
    rows = rows_ref[0, pl.ds(start, span), :]
    k = rows[:, :HEAD_DIM].astype(BF16)
    d = (t0 + lax.broadcasted_iota(jnp.int32, (tq, span), 0)) - (start + lax.broadcasted_iota(jnp.int32, (tq, span), 1))
    s = _dot_nt(qrs, k).reshape(N_G, tq, span) + jnp.where((d >= 0) & (d <= window), 0.0, NEG_INF)[None]
    m = jnp.max(s, axis=-1, keepdims=True)
    if sinks is not None:
        sink = jnp.concatenate([jnp.full((1, tq, 1), sk, F32) for sk in sinks], axis=0)
        m = jnp.maximum(m, sink)
    e = jnp.exp2(s - m)
    den = jnp.sum(e, axis=-1, keepdims=True)
    if sinks is not None:
        den = den + jnp.exp2(sink - m)
    o = _dot(e.reshape(N_G * tq, span).astype(BF16), rows[:, HEAD_DIM:].astype(BF16)).reshape(N_G, tq, HEAD_DIM)
    return o / jnp.maximum(den, TINY)


def _values_and_ones(rows):
    lane = lax.broadcasted_iota(jnp.int32, rows.shape, 1)
    return jnp.where(lane < HEAD_DIM, pltpu.roll(rows, HEAD_DIM, 1), 1.0).astype(BF16)


def _nsa_prompt_attend(z, q_rot, cblocks, tq=128, kc=512):
    b, t, _ = z.shape
    n_cmp = cblocks.shape[2]
    hw = N_G * HEAD_DIM
    blk = lambda base: pl.BlockSpec((1, t, LANES), lambda a, k, i: (a, 0, base // LANES + k))
    return pl.pallas_call(
        functools.partial(_nsa_prompt_kernel, tq=tq, kc=kc),
        grid=(b, N_KV, t // tq),
        in_specs=[pl.BlockSpec((1, tq, hw), lambda a, k, i: (a, i, k)),
                  pl.BlockSpec((1, tq, hw), lambda a, k, i: (a, i, k)),
                  pl.BlockSpec((1, 2, n_cmp, HEAD_DIM), lambda a, k, i: (a, k, 0, 0)),
                  blk(A_SLC0), blk(A_WIN0),
                  pl.BlockSpec((1, tq, LANES), lambda a, k, i: (a, i, A_GATE0 // LANES + k))],
        out_specs=pl.BlockSpec((1, tq, hw), lambda a, k, i: (a, i, k)),
        out_shape=jax.ShapeDtypeStruct((b, t, D_MODEL), F32),
        scratch_shapes=[pltpu.VMEM((N_G, tq, 1), F32), pltpu.VMEM((N_G, tq, LANES), F32)],
        compiler_params=_cparams(("arbitrary", "arbitrary", "arbitrary")),
        name="nsa_prompt_attend",
    )(z, q_rot, cblocks, z, z, z)


def _swa_prompt_kernel(sink_ref, q_ref, rows_ref, o_ref, *, tq):
    kv = pl.program_id(1)
    t0 = pl.program_id(2) * tq
    qrs = (_stack_heads(q_ref[0]) * (ATTN_SCALE * LOG2E)).astype(BF16)
    sinks = [sink_ref[kv, g] * LOG2E for g in range(N_G)]
    o = _band_attend(qrs, rows_ref, t0, tq, B_WINDOW, sinks)
    for g in range(N_G):
        o_ref[0, :, g * HEAD_DIM:(g + 1) * HEAD_DIM] = o[g]


def _swa_prompt_attend(z, sinks, tq=128):
    b, t, _ = z.shape
    hw = N_G * HEAD_DIM
    return pl.pallas_call(
        functools.partial(_swa_prompt_kernel, tq=tq),
        grid=(b, N_KV, t // tq),
        in_specs=[pl.BlockSpec(memory_space=pltpu.SMEM),
                  pl.BlockSpec((1, tq, hw), lambda a, k, i: (a, i, k)),
                  pl.BlockSpec((1, t, LANES), lambda a, k, i: (a, 0, D_MODEL // LANES + k))],
        out_specs=pl.BlockSpec((1, tq, hw), lambda a, k, i: (a, i, k)),
        out_shape=jax.ShapeDtypeStruct((b, t, D_MODEL), F32),
        compiler_params=_cparams(("arbitrary", "arbitrary", "arbitrary")),
        name="swa_prompt_attend",
    )(sinks.reshape(N_KV, N_G), z, z)


S_CHUNKS = 520
S_ROWS = S_CHUNKS * CMP_STRIDE
S_SEL = 256
N_SEL_SAMPLE = PAST_LEN // SEL_BLOCK + 1
SLABS = CMP_STRIDE // SUBLANES
PAGES_PER_ITER = 4
assert N_SEL_SAMPLE == LANES + 1 and DEC_SEQ == SUBLANES and N_PAGES % PAGES_PER_ITER == 0


def _page_copy(cache_hbm, ia, pt_ref, n, p, buf, slot, sem):
    return pltpu.make_async_copy(cache_hbm.at[ia, pt_ref[n, p]],
                                 buf.at[slot, :, pl.ds(pl.multiple_of(p * PAGE_SIZE, PAGE_SIZE), PAGE_SIZE)],
                                 sem.at[slot])


def _gather_pages(cache_hbm, ia, pt_ref, buf, sem):
    n = pl.program_id(0)
    nn = pl.num_programs(0)
    slot = n % 2

    def start(seq, sl):
        def go(p, c):
            _page_copy(cache_hbm, ia, pt_ref, seq, p, buf, sl, sem).start()
            return c
        lax.fori_loop(0, N_PAGES, go, 0)

    @pl.when(n == 0)
    def _():
        start(0, 0)

    @pl.when(n + 1 < nn)
    def _():
        start(n + 1, 1 - slot)

    def wait(p, c):
        _page_copy(cache_hbm, ia, pt_ref, n, p, buf, slot, sem).wait()
        return c
    lax.fori_loop(0, N_PAGES, wait, 0)
    return slot


def _sample_q_rows(q, kv):
    return jnp.concatenate([q[:, (kv * N_G + g) * HEAD_DIM:(kv * N_G + g + 1) * HEAD_DIM] for g in range(N_G)], axis=0)


def _sample_q_blockdiag(q):
    parts = []
    for kv in range(N_KV):
        qs = _sample_q_rows(q, kv)
        left = jnp.zeros((qs.shape[0], kv * 2 * HEAD_DIM), F32)
        right = jnp.zeros((qs.shape[0], KV_ROW - kv * 2 * HEAD_DIM - HEAD_DIM), F32)
        parts.append(jnp.concatenate([left, qs, right], axis=1) if kv else jnp.concatenate([qs, right], axis=1))
    return jnp.concatenate(parts, axis=0)


def _nsa_sample_cmp_kernel(pt_ref, new_ref, q_ref, w1r_ref, pb_ref, w2_ref, cache_hbm,
                           ocmp_ref, sel_ref, pages, rows, chunks, sem, *, ia):
    slot = _gather_pages(cache_hbm, ia, pt_ref, pages, sem)

    def to_rows(it, c):
        for u in range(PAGES_PER_ITER):
            p = it * PAGES_PER_ITER + u
            r0 = pl.multiple_of(p * PAGE_SIZE, PAGE_SIZE)
            c0 = pl.multiple_of(p * (PAGE_SIZE // SLABS), PAGE_SIZE // SLABS)
            for kv in range(N_KV):
                t = pages[slot, kv * LANES:(kv + 1) * LANES, pl.ds(r0, PAGE_SIZE)].T
                for v in range(PAGE_SIZE // SUBLANES):
                    rows[kv, v % SLABS, pl.ds(c0 + (v // SLABS) * SUBLANES, SUBLANES), :] = t[v * SUBLANES:(v + 1) * SUBLANES]
        return c
    lax.fori_loop(0, N_PAGES // PAGES_PER_ITER, to_rows, 0)
    new_rows = new_ref[0]
    tail = (PAST_LEN // CMP_STRIDE) * SUBLANES
    for kv in range(N_KV):
        rows[kv, 0, tail:tail + DEC_SEQ, :] = new_rows[:, kv * LANES:(kv + 1) * LANES]
        rows[kv, 0, tail + DEC_SEQ:, :] = jnp.zeros((S_CHUNKS * SUBLANES - tail - DEC_SEQ, LANES), F32)
        for slab in range(1, SLABS):
            rows[kv, slab, tail:, :] = jnp.zeros((S_CHUNKS * SUBLANES - tail, LANES), F32)

    lo = lax.broadcasted_iota(jnp.int32, (S_CHUNKS, LANES), 1) < HEAD_DIM
    for k in range(CMP_STRIDE // 2):
        for kv in range(N_KV):
            sa, sb = 2 * k, 2 * k + 1
            a = rows[kv, sa // SUBLANES, pl.ds(sa % SUBLANES, S_CHUNKS, stride=SUBLANES), :]
            b = rows[kv, sb // SUBLANES, pl.ds(sb % SUBLANES, S_CHUNKS, stride=SUBLANES), :]
            chunks[2 * kv, :, k * LANES:(k + 1) * LANES] = jnp.where(lo, a, pltpu.roll(b, HEAD_DIM, 1)).astype(BF16)
            chunks[2 * kv + 1, :, k * LANES:(k + 1) * LANES] = jnp.where(lo, pltpu.roll(a, HEAD_DIM, 1), b).astype(BF16)

    q = q_ref[0]
    qpos = PAST_LEN + lax.broadcasted_iota(jnp.int32, (DEC_SEQ, 1), 0)
    cend = lax.broadcasted_iota(jnp.int32, (DEC_SEQ, S_CHUNKS), 1) * CMP_STRIDE + CMP_BLOCK - 1
    cmask = (cend <= qpos)[None]
    imps = []
    for kv in range(N_KV):
        k_c = _compress_blocks(chunks[2 * kv], w1r_ref[0], pb_ref[0], w2_ref[0]).astype(BF16)
        v_c = _compress_blocks(chunks[2 * kv + 1], w1r_ref[1], pb_ref[1], w2_ref[1]).astype(BF16)
        qs = _sample_q_rows(q, kv).astype(BF16)
        s = (_dot_nt(qs, k_c) * ATTN_SCALE).reshape(N_G, DEC_SEQ, S_CHUNKS)
        p = _masked_probs(s, cmask)
        o = _dot(p.reshape(N_G * DEC_SEQ, S_CHUNKS).astype(BF16), v_c)
        for g in range(N_G):
            h = kv * N_G + g
            ocmp_ref[0, :, h * HEAD_DIM:(h + 1) * HEAD_DIM] = o[g * DEC_SEQ:(g + 1) * DEC_SEQ]
        imps.append(jnp.sum(p, axis=0))
    imp = jnp.concatenate(imps, axis=0)

    score = _dot_nt(imp, _chunk_score_matrix_t(S_SEL, S_CHUNKS), precision=lax.Precision.HIGHEST)
    shape = score.shape
    jj = lax.broadcasted_iota(jnp.int32, shape, 1)
    tt = PAST_LEN + lax.broadcasted_iota(jnp.int32, shape, 0) % DEC_SEQ
    cur = tt // SEL_BLOCK
    forced = (jj == 0) | (jj == cur) | (jj == cur - 1)
    sc = jnp.where(jj * SEL_BLOCK <= tt, jnp.where(forced, FORCE_SCORE, score), NEG_INF)

    sc_a = sc[:, :LANES]
    extra = jnp.sum(jnp.where(jj == LANES, sc, 0.0), axis=1, keepdims=True)
    lane = lax.broadcasted_iota(jnp.int32, sc_a.shape, 1)
    rank = jnp.where(extra > sc_a, 1.0, 0.0)
    for k in range(1, LANES):
        r = pltpu.roll(sc_a, k, 1)
        rank = rank + jnp.where(r > sc_a, 1.0, jnp.where(r == sc_a, jnp.where(lane >= k, 1.0, 0.0), 0.0))
    rank_x = jnp.sum(jnp.where(sc_a >= extra, 1.0, 0.0), axis=1, keepdims=True)
    sel_ref[0, :, :LANES] = jnp.where(rank < SEL_TOP, 1.0, 0.0)
    sel_ref[0, :, LANES:] = jnp.where((lane == 0) & (rank_x < SEL_TOP), 1.0, 0.0)


def _nsa_sample_cmp(ia, page_table, cache, new_rows, q, w1r, pb, w2):
    n = q.shape[0]
    grid_spec = pltpu.PrefetchScalarGridSpec(
        num_scalar_prefetch=1,
        grid=(n,),
        in_specs=[pl.BlockSpec((1, DEC_SEQ, KV_ROW), lambda a, pt: (a, 0, 0)),
                  pl.BlockSpec((1, DEC_SEQ, D_MODEL), lambda a, pt: (a, 0, 0)),
                  pl.BlockSpec((2, CHUNK_W, 2 * CMP_HIDDEN), lambda a, pt: (0, 0, 0)),
                  pl.BlockSpec((2, 1, CMP_HIDDEN), lambda a, pt: (0, 0, 0)),
                  pl.BlockSpec((2, CMP_HIDDEN, HEAD_DIM), lambda a, pt: (0, 0, 0)),
                  pl.BlockSpec(memory_space=pl.ANY)],
        out_specs=[pl.BlockSpec((1, DEC_SEQ, D_MODEL), lambda a, pt: (a, 0, 0)),
                   pl.BlockSpec((1, N_KV * DEC_SEQ, S_SEL), lambda a, pt: (a, 0, 0))],
        scratch_shapes=[pltpu.VMEM((2, KV_ROW, PAST_LEN), F32),
                        pltpu.VMEM((N_KV, SLABS, S_CHUNKS * SUBLANES, LANES), F32),
                        pltpu.VMEM((2 * N_KV, S_CHUNKS, CHUNK_W), BF16),
                        pltpu.SemaphoreType.DMA((2,))])
    return pl.pallas_call(
        functools.partial(_nsa_sample_cmp_kernel, ia=ia),
        grid_spec=grid_spec,
        out_shape=[jax.ShapeDtypeStruct((n, DEC_SEQ, D_MODEL), F32),
                   jax.ShapeDtypeStruct((n, N_KV * DEC_SEQ, S_SEL), F32)],
        compiler_params=_cparams(("arbitrary",)),
        name="nsa_sample_cmp",
    )(page_table, new_rows, q, w1r, pb, w2, cache)


def _nsa_sample_slc_kernel(pt_ref, new_ref, qr_ref, sel_ref, wbuf_ref, wnew_ref, gate_ref, ocmp_ref, cache_hbm,
                           o_ref, pages, pages_bf, sem, *, ia):
    slot = _gather_pages(cache_hbm, ia, pt_ref, pages, sem)
    pages_bf[...] = pages[slot].astype(BF16)

    nq = N_KV * N_G * DEC_SEQ
    qbd = (_sample_q_blockdiag(qr_ref[0]) * ATTN_SCALE).astype(BF16)
    qpos = PAST_LEN + lax.broadcasted_iota(jnp.int32, (DEC_SEQ, 1), 0)

    past = pages_bf[...]
    new = new_ref[0].astype(BF16)
    s_p = _dot(qbd, past).reshape(N_KV, N_G, DEC_SEQ, PAST_LEN)
    s_n = _dot_nt(qbd, new).reshape(N_KV, N_G, DEC_SEQ, DEC_SEQ)
    sel = sel_ref[0]
    sel_bf = sel.astype(BF16)
    kw = PAST_LEN // 4
    chosen = []
    for c in range(4):
        kp = c * kw + lax.broadcasted_iota(jnp.int32, (S_SEL, kw), 1)
        expand = jnp.where(kp // SEL_BLOCK == lax.broadcasted_iota(jnp.int32, (S_SEL, kw), 0), 1.0, 0.0)
        chosen.append(_dot(sel_bf, expand.astype(BF16)))
    chosen = jnp.concatenate(chosen, axis=1).reshape(N_KV, 1, DEC_SEQ, PAST_LEN)
    bias_p = jnp.where(chosen > 0.5, 0.0, NEG_INF)
    new_blk = sel[:, PAST_LEN // SEL_BLOCK:PAST_LEN // SEL_BLOCK + 1].reshape(N_KV, 1, DEC_SEQ, 1)
    npos = PAST_LEN + lax.broadcasted_iota(jnp.int32, (DEC_SEQ, DEC_SEQ), 1)
    bias_n = jnp.where((new_blk > 0.5) & (npos <= qpos)[None, None], 0.0, NEG_INF)
    s_p = s_p + bias_p
    s_n = s_n + bias_n
    m = jnp.maximum(jnp.max(s_p, axis=-1, keepdims=True), jnp.max(s_n, axis=-1, keepdims=True))
    e_p = jnp.exp(s_p - m)
    e_n = jnp.exp(s_n - m)
    den = jnp.sum(e_p, axis=-1, keepdims=True) + jnp.sum(e_n, axis=-1, keepdims=True)
    o_slc = (_dot_nt(e_p.reshape(nq, PAST_LEN).astype(BF16), past)
             + _dot(e_n.reshape(nq, DEC_SEQ).astype(BF16), new)) / jnp.maximum(den.reshape(nq, 1), TINY)

    wrows = jnp.concatenate([wbuf_ref[0], wnew_ref[0]], axis=0).astype(BF16)
    nw = A_WINDOW + DEC_SEQ
    sw = _dot_nt(qbd, wrows).reshape(N_KV * N_G, DEC_SEQ, nw)
    d = qpos - (PAST_LEN - A_WINDOW + lax.broadcasted_iota(jnp.int32, (DEC_SEQ, nw), 1))
    pw = _masked_probs(sw, ((d >= 0) & (d <= A_WINDOW))[None])
    o_win = _dot(pw.reshape(nq, nw).astype(BF16), wrows)

    gt = jax.nn.sigmoid(gate_ref[0])
    o_cmp = ocmp_ref[0]
    for kv in range(N_KV):
        v0 = kv * 2 * HEAD_DIM + HEAD_DIM
        for g in range(N_G):
            h = kv * N_G + g
            r0 = h * DEC_SEQ
            c0 = kv * LANES + g
            og = (gt[:, c0:c0 + 1] * o_cmp[:, h * HEAD_DIM:(h + 1) * HEAD_DIM]
                  + gt[:, c0 + N_G:c0 + N_G + 1] * o_slc[r0:r0 + DEC_SEQ, v0:v0 + HEAD_DIM]
                  + gt[:, c0 + 2 * N_G:c0 + 2 * N_G + 1] * o_win[r0:r0 + DEC_SEQ, v0:v0 + HEAD_DIM])
            o_ref[0, :, h * HEAD_DIM:(h + 1) * HEAD_DIM] = og


def _nsa_sample_slc(ia, page_table, cache, z, q_rot, sel, win_cache, o_cmp):
    n = z.shape[0]
    row_blk = lambda base: pl.BlockSpec((1, DEC_SEQ, KV_ROW), lambda a, pt: (a, 0, base // KV_ROW))
    grid_spec = pltpu.PrefetchScalarGridSpec(
        num_scalar_prefetch=1,
        grid=(n,),
        in_specs=[row_blk(A_SLC0),
                  pl.BlockSpec((1, DEC_SEQ, D_MODEL), lambda a, pt: (a, 0, 0)),
                  pl.BlockSpec((1, N_KV * DEC_SEQ, S_SEL), lambda a, pt: (a, 0, 0)),
                  pl.BlockSpec((None, 1, A_WINDOW, KV_ROW), lambda a, pt: (ia, a, 0, 0)),
                  row_blk(A_WIN0), row_blk(A_GATE0),
                  pl.BlockSpec((1, DEC_SEQ, D_MODEL), lambda a, pt: (a, 0, 0)),
                  pl.BlockSpec(memory_space=pl.ANY)],
        out_specs=pl.BlockSpec((1, DEC_SEQ, D_MODEL), lambda a, pt: (a, 0, 0)),
        scratch_shapes=[pltpu.VMEM((2, KV_ROW, PAST_LEN), F32),
                        pltpu.VMEM((KV_ROW, PAST_LEN), BF16),
                        pltpu.SemaphoreType.DMA((2,))])
    return pl.pallas_call(
        functools.partial(_nsa_sample_slc_kernel, ia=ia),
        grid_spec=grid_spec,
        out_shape=jax.ShapeDtypeStruct((n, DEC_SEQ, D_MODEL), F32),
        compiler_params=_cparams(("arbitrary",)),
        name="nsa_sample_slc",
    )(page_table, z, q_rot, sel, win_cache, z, z, o_cmp, cache)


def _swa_sample_kernel(q_ref, new_ref, wbuf_ref, sink_ref, o_ref):
    nq = N_KV * N_G * DEC_SEQ
    qbd = _sample_q_blockdiag(q_ref[0]).astype(BF16)
    rows = jnp.concatenate([wbuf_ref[0], new_ref[0]], axis=0).astype(BF16)
    nw = B_WINDOW + DEC_SEQ
    s = (_dot_nt(qbd, rows) * ATTN_SCALE).reshape(N_KV * N_G, DEC_SEQ, nw)
    qpos = PAST_LEN + lax.broadcasted_iota(jnp.int32, (DEC_SEQ, 1), 0)
    d = qpos - (PAST_LEN - B_WINDOW + lax.broadcasted_iota(jnp.int32, (DEC_SEQ, nw), 1))
    mask = ((d >= 0) & (d <= B_WINDOW))[None]
    sink = sink_ref[...]
    sm = jnp.where(mask, s, NEG_INF)
    m = jnp.maximum(jnp.max(sm, axis=-1, keepdims=True), sink)
    e = jnp.where(mask, jnp.exp(sm - m), 0.0)
    den = jnp.sum(e, axis=-1, keepdims=True) + jnp.exp(sink - m)
    p = e / jnp.maximum(den, TINY)
    o = _dot(p.reshape(nq, nw).astype(BF16), rows)
    for kv in range(N_KV):
        v0 = kv * 2 * HEAD_DIM + HEAD_DIM
        for g in range(N_G):
            h = kv * N_G + g
            o_ref[0, :, h * HEAD_DIM:(h + 1) * HEAD_DIM] = o[h * DEC_SEQ:(h + 1) * DEC_SEQ, v0:v0 + HEAD_DIM]


def _swa_sample_attend(ib, z, win_cache, sinks):
    n = z.shape[0]
    return pl.pallas_call(
        _swa_sample_kernel,
        grid=(n,),
        in_specs=[pl.BlockSpec((1, DEC_SEQ, D_MODEL), lambda a: (a, 0, 0)),
                  pl.BlockSpec((1, DEC_SEQ, KV_ROW), lambda a: (a, 0, D_MODEL // KV_ROW)),
                  pl.BlockSpec((None, 1, B_WINDOW, KV_ROW), lambda a: (ib, a, 0, 0)),
                  pl.BlockSpec((N_HEADS, 1, 1), lambda a: (0, 0, 0))],
        out_specs=pl.BlockSpec((1, DEC_SEQ, D_MODEL), lambda a: (a, 0, 0)),
        out_shape=jax.ShapeDtypeStruct((n, DEC_SEQ, D_MODEL), F32),
        compiler_params=_cparams(("arbitrary",)),
        name="swa_sample_attend",
    )(z, z, win_cache, sinks.reshape(N_HEADS, 1, 1))


def _nsa_col_perm():
    perm = np.full((A_COLS,), -1, np.int64)
    perm[:D_MODEL] = np.arange(D_MODEL)
    kvw = N_KV * HEAD_DIM
    for br, base in enumerate((A_CMP0, A_SLC0, A_WIN0)):
        old0 = D_MODEL + 2 * kvw * br
        for kv in range(N_KV):
            for i in range(2):
                dst = base + kv * 2 * HEAD_DIM + i * HEAD_DIM
                src = old0 + i * kvw + kv * HEAD_DIM
                perm[dst:dst + HEAD_DIM] = np.arange(src, src + HEAD_DIM)
    old_g = D_MODEL + 6 * kvw
    for kv in range(N_KV):
        for br in range(3):
            dst = A_GATE0 + kv * LANES + br * N_G
            src = old_g + br * N_HEADS + kv * N_G
            perm[dst:dst + N_G] = np.arange(src, src + N_G)
    return perm


def _swa_col_perm():
    perm = np.zeros((B_COLS,), np.int64)
    perm[:D_MODEL] = np.arange(D_MODEL)
    kvw = N_KV * HEAD_DIM
    for kv in range(N_KV):
        for i in range(2):
            dst = D_MODEL + kv * 2 * HEAD_DIM + i * HEAD_DIM
            src = D_MODEL + i * kvw + kv * HEAD_DIM
            perm[dst:dst + HEAD_DIM] = np.arange(src, src + HEAD_DIM)
    return perm


def _permute_cols(w, perm):
    cols = jnp.take(w, jnp.asarray(np.maximum(perm, 0)), axis=-1)
    return jnp.where(jnp.asarray(perm >= 0), cols, 0.0)


def _rows_view(z, c0, n, t):
    return z[..., c0:c0 + KV_ROW].reshape(n, t, N_KV, 2, HEAD_DIM)


def _trunk(x, ada, past, p, prompt):
    g, t, _ = x.shape
    tm = min(512, t)
    if prompt:
        pos = jnp.arange(t, dtype=jnp.int32)
        n_seq, t_seq = g, t
    else:
        n_seq, t_seq = t // DEC_SEQ, DEC_SEQ
        pos = jnp.tile(PAST_LEN + jnp.arange(DEC_SEQ, dtype=jnp.int32), n_seq)
    tab = _rope_tables(pos)[None]
    cmp_out, slc_out, nwin_out, swin_out = [], [], [], []
    for l in range(DEPTH):
        sh1, sc1, g1, sh2, sc2, g2 = [ada[l][..., k * D_MODEL:(k + 1) * D_MODEL] for k in range(6)]
        i = l // 2
        if l % 2 == 0:
            z, q_rot = _proj(x, sc1, sh1, p['nsa_w_in'][i], p['zero_a'], tab, tm, (A_SLC0, A_WIN0), True)
            cmp_rows = _rows_view(z, A_CMP0, n_seq, t_seq)
            slc_rows = _rows_view(z, A_SLC0, n_seq, t_seq)
            win_rows = _rows_view(z, A_WIN0, n_seq, t_seq)
            w1r, pb, w2 = p['cmp_w1r'][i], p['cmp_pb'][i], p['cmp_w2'][i]
            if prompt:
                nch = t // CMP_STRIDE
                chunks = z[..., A_CMP0:A_CMP0 + KV_ROW].reshape(g, nch, CMP_STRIDE, 2 * N_KV, HEAD_DIM)
                chunks = chunks.transpose(0, 3, 1, 2, 4).reshape(g, 2 * N_KV, nch, CHUNK_W)
                cblocks = _compress_prompt(chunks, w1r, pb, w2)
                o = _nsa_prompt_attend(z, q_rot, cblocks)
                nwin_out.append(win_rows[:, -min(A_WINDOW, t):])
            else:
                cache_cmp, cache_slc, cache_nwin, _, page_table = past
                zs = z.reshape(n_seq, DEC_SEQ, A_COLS)
                qs = zs[..., :D_MODEL]
                o_cmp, sel = _nsa_sample_cmp(i, page_table, cache_cmp, zs[..., A_CMP0:A_CMP0 + KV_ROW], qs, w1r, pb, w2)
                o = _nsa_sample_slc(i, page_table, cache_slc, zs, q_rot.reshape(n_seq, DEC_SEQ, D_MODEL), sel,
                                    cache_nwin, o_cmp).reshape(g, t, D_MODEL)
                nwin_out.append(jnp.concatenate([cache_nwin[i].reshape(n_seq, A_WINDOW, N_KV, 2, HEAD_DIM)[:, DEC_SEQ:],
                                                 win_rows], axis=1))
            cmp_out.append(cmp_rows)
            slc_out.append(slc_rows)
            w_o, b_o = p['nsa_w_o'][i], p['zero_d']
        else:
            (z,) = _proj(x, sc1, sh1, p['swa_w_in'][i], p['swa_b_in'][i], tab, tm, (D_MODEL,), False)
            rows = _rows_view(z, D_MODEL, n_seq, t_seq)
            if prompt:
                o = _swa_prompt_attend(z, p['swa_sinks'][i])
                swin_out.append(rows[:, -min(B_WINDOW, t):])
            else:
                cache_swin = past[3]
                o = _swa_sample_attend(i, z.reshape(n_seq, DEC_SEQ, B_COLS), cache_swin, p['swa_sinks'][i]).reshape(g, t, D_MODEL)
                swin_out.append(jnp.concatenate([cache_swin[i].reshape(n_seq, B_WINDOW, N_KV, 2, HEAD_DIM)[:, DEC_SEQ:],
                                                 rows], axis=1))
            w_o, b_o = p['swa_w_o'][i], p['swa_b_o'][i]
        x = _outproj_ln(o, x, g1, w_o, b_o, p['ln_g'][l, 0], p['ln_b'][l, 0], tm)
        x = _moe_ln(x, sc2, sh2, g2, p['router_wt'], p['router_b'], p['moe_w_gate'][l], p['moe_w_up'][l],
                    p['moe_w_down'][l], p['ln_g'][l, 1], p['ln_b'][l, 1], min(MOE_TILE, t))
    return x, (jnp.stack(cmp_out), jnp.stack(slc_out), jnp.stack(nwin_out), jnp.stack(swin_out))


def kernel(x_prompt, x_sample, cache_nsa_cmp, cache_nsa_slc, cache_nsa_win, cache_swa_win, page_table, c_prompt, c_sample, nsa_w_in, nsa_cmp_pe, nsa_cmp_w1, nsa_cmp_b1, nsa_cmp_w2, nsa_w_o, swa_w_in, swa_b_in, swa_sinks, swa_w_o, swa_b_o, ada_w, ada_b, ln_g, ln_b, router_w, router_b, moe_w_gate, moe_w_up, moe_w_down):
    n_a = nsa_w_in.shape[0]
    batch, seq, _ = x_prompt.shape
    n_dec = x_sample.shape[0]
    n_pool = cache_nsa_cmp.shape[1]

    pe_term = jnp.stack([_pe_term(nsa_cmp_pe[i], nsa_cmp_w1[i]) for i in range(n_a)])
    w1r = nsa_cmp_w1.reshape(n_a, 2, CMP_SHIFTS, CHUNK_W, CMP_HIDDEN).transpose(0, 1, 3, 2, 4)
    p = {
        'nsa_w_in': _permute_cols(nsa_w_in, _nsa_col_perm()).astype(BF16),
        'zero_a': jnp.zeros((1, A_COLS), F32),
        'zero_d': jnp.zeros((1, D_MODEL), F32),
        'cmp_w1r': w1r.reshape(n_a, 2, CHUNK_W, CMP_SHIFTS * CMP_HIDDEN).astype(BF16),
        'cmp_pb': (pe_term + nsa_cmp_b1).reshape(n_a, 2, 1, CMP_HIDDEN),
        'cmp_w2': nsa_cmp_w2.astype(BF16),
        'nsa_w_o': nsa_w_o.astype(BF16),
        'swa_w_in': _permute_cols(swa_w_in, _swa_col_perm()).astype(BF16),
        'swa_b_in': _permute_cols(swa_b_in, _swa_col_perm())[:, None, :],
        'swa_sinks': swa_sinks,
        'swa_w_o': swa_w_o.astype(BF16),
        'swa_b_o': swa_b_o[:, None, :],
        'ln_g': ln_g[:, :, None, :],
        'ln_b': ln_b[:, :, None, :],
        'router_wt': router_w.T,
        'router_b': router_b[:, None],
        'moe_w_gate': moe_w_gate.astype(BF16),
        'moe_w_up': moe_w_up.astype(BF16),
        'moe_w_down': moe_w_down.astype(BF16),
    }

    ada = _ada(jnp.concatenate([c_prompt, c_sample], axis=0), ada_w, ada_b)
    ada_p = ada[:, :batch, None, :]
    ada_s = jnp.broadcast_to(ada[:, batch:, None, :], (DEPTH, n_dec, DEC_SEQ, 6 * D_MODEL))
    ada_s = ada_s.reshape(DEPTH, 1, n_dec * DEC_SEQ, 6 * D_MODEL)

    y_p, (cmp_p, slc_p, nwin_p, swin_p) = _trunk(x_prompt, ada_p, None, p, True)
    pages = lambda c: c.transpose(0, 1, 3, 4, 5, 2).reshape(n_a, n_pool, KV_ROW, PAGE_SIZE)
    past = (pages(cache_nsa_cmp), pages(cache_nsa_slc),
            cache_nsa_win.reshape(n_a, n_dec, A_WINDOW, KV_ROW),
            cache_swa_win.reshape(cache_swa_win.shape[0], n_dec, B_WINDOW, KV_ROW),
            page_table)
    y_s, (cmp_s, slc_s, nwin_s, swin_s) = _trunk(x_sample.reshape(1, n_dec * DEC_SEQ, D_MODEL), ada_s, past, p, False)
    return (y_p, y_s.reshape(n_dec, DEC_SEQ, D_MODEL), cmp_p, slc_p, nwin_p, swin_p, cmp_s, slc_s, nwin_s, swin_s)
```

```python
import functools
import math

import numpy as np
import jax
import jax.numpy as jnp
from jax import lax
from jax.experimental import pallas as pl
from jax.experimental.pallas import tpu as pltpu

D_MODEL = 1024
SEQ = 2048
DEPTH = 4
DEC_SEQ = 8
PAST_LEN = 8192
PAGE_SIZE = 128
N_PAGES = PAST_LEN // PAGE_SIZE

HEAD_DIM = 64
ROT_DIM = HEAD_DIM // 4
ROT_HALF = ROT_DIM // 2
ROPE_THETA = 500000.0
ATTN_SCALE = HEAD_DIM ** -0.5

N_HEADS = D_MODEL // HEAD_DIM
N_KV = 2
N_G = N_HEADS // N_KV
KV_ROW = N_KV * 2 * HEAD_DIM
CMP_BLOCK = 32
CMP_STRIDE = 16
CMP_HIDDEN = 4 * HEAD_DIM
CMP_SHIFTS = CMP_BLOCK // CMP_STRIDE
CHUNK_W = CMP_STRIDE * HEAD_DIM
SEL_BLOCK = 64
SEL_TOP = 16
CHUNKS_PER_SEL = SEL_BLOCK // CMP_STRIDE
A_WINDOW = 512
B_WINDOW = 128
FORCE_SCORE = 1e9

N_EXPERTS = 16
N_GROUPS = 4
EXPERTS_PER_GROUP = N_EXPERTS // N_GROUPS
D_FF_EXPERT = D_MODEL // 2

DN_ALPHA = (2.0 * DEPTH) ** 0.25
LN_EPS = 1e-5
NEG_INF = -1e30
TINY = 1e-30
LOG2E = math.log2(math.e)

LANES = 128
SUBLANES = 8
MOE_TILE = 1024
MOE_SUB_TILE = 128
A_COLS = 2048
A_CMP0, A_SLC0, A_WIN0, A_GATE0 = 1024, 1280, 1536, 1792
B_COLS = D_MODEL + KV_ROW
VMEM_LIMIT = 56 * 1024 * 1024

F32 = jnp.float32
BF16 = jnp.bfloat16


def _cparams(sem):
    return pltpu.CompilerParams(dimension_semantics=sem, vmem_limit_bytes=VMEM_LIMIT)


def _dot(a, b):
    return jnp.dot(a, b, preferred_element_type=F32)


def _dot_nt(a, b, precision=None):
    return lax.dot_general(a, b, (((1,), (1,)), ((), ())), precision=precision,
                           preferred_element_type=F32)


def _masked_probs(s, mask):
    s = jnp.where(mask, s, NEG_INF)
    m = jnp.max(s, axis=-1, keepdims=True)
    e = jnp.where(mask, jnp.exp(s - m), 0.0)
    den = jnp.sum(e, axis=-1, keepdims=True)
    return e / jnp.maximum(den, TINY)


def _layer_norm(r, g, b):
    mu = jnp.mean(r, axis=-1, keepdims=True)
    rc = r - mu
    var = jnp.mean(rc * rc, axis=-1, keepdims=True)
    return rc * lax.rsqrt(var + LN_EPS) * g + b


def _ada_kernel(c_ref, w_ref, b_ref, o_ref):
    c = c_ref[...]
    a = (c * jax.nn.sigmoid(c)).astype(BF16)
    o_ref[0] = _dot(a, w_ref[0].astype(BF16)) + b_ref[0]


def _ada(c_all, ada_w, ada_b):
    m = c_all.shape[0]
    n = ada_w.shape[-1]
    tn = 1536
    return pl.pallas_call(
        _ada_kernel,
        grid=(DEPTH, n // tn),
        in_specs=[pl.BlockSpec((m, D_MODEL), lambda l, j: (0, 0)),
                  pl.BlockSpec((1, D_MODEL, tn), lambda l, j: (l, 0, j)),
                  pl.BlockSpec((1, 1, tn), lambda l, j: (l, 0, j))],
        out_specs=pl.BlockSpec((1, m, tn), lambda l, j: (l, 0, j)),
        out_shape=jax.ShapeDtypeStruct((DEPTH, m, n), F32),
        compiler_params=_cparams(("arbitrary", "arbitrary")),
        name="ada",
    )(c_all, ada_w, ada_b.reshape(DEPTH, 1, n))


def _pe_term_kernel(pe_ref, w_ref, o_ref):
    o_ref[0] = _dot(pe_ref[0].astype(BF16), w_ref[0].astype(BF16))


def _pe_term(pe, w1):
    k = CMP_BLOCK * HEAD_DIM
    pe8 = jnp.broadcast_to(pe.reshape(2, 1, k), (2, 8, k))
    out = pl.pallas_call(
        _pe_term_kernel,
        grid=(2,),
        in_specs=[pl.BlockSpec((1, 8, k), lambda i: (i, 0, 0)),
                  pl.BlockSpec((1, k, CMP_HIDDEN), lambda i: (i, 0, 0))],
        out_specs=pl.BlockSpec((1, 8, CMP_HIDDEN), lambda i: (i, 0, 0)),
        out_shape=jax.ShapeDtypeStruct((2, 8, CMP_HIDDEN), F32),
        compiler_params=_cparams(("arbitrary",)),
        name="pe_term",
    )(pe8, w1)
    return out[:, 0]


def _rope_tables(pos):
    inv = jnp.exp(-math.log(ROPE_THETA) * jnp.arange(ROT_HALF, dtype=F32) / ROT_HALF)
    ang = pos.astype(F32)[:, None] * inv[None, :]
    cos, sin = jnp.cos(ang), jnp.sin(ang)
    t = pos.shape[0]
    one = jnp.ones((t, HEAD_DIM - ROT_DIM), F32)
    zero = jnp.zeros((t, HEAD_DIM - ROT_DIM), F32)
    zh = jnp.zeros((t, ROT_HALF), F32)
    cos_h = jnp.concatenate([cos, cos, one], axis=1)
    sa_h = jnp.concatenate([-sin, zh, zero], axis=1)
    sb_h = jnp.concatenate([zh, sin, zero], axis=1)
    id_h = jnp.ones((t, HEAD_DIM), F32)
    z_h = jnp.zeros((t, HEAD_DIM), F32)
    return jnp.concatenate([cos_h, cos_h, sa_h, sa_h, sb_h, sb_h,
                            cos_h, id_h, sa_h, z_h, sb_h, z_h], axis=1)


def _rope_apply(seg, cos, sa, sb):
    w = seg.shape[1]
    reps = w // LANES
    c = jnp.tile(cos, (1, reps))
    a = jnp.tile(sa, (1, reps))
    b = jnp.tile(sb, (1, reps))
    return seg * c + pltpu.roll(seg, w - ROT_HALF, 1) * a + pltpu.roll(seg, ROT_HALF, 1) * b


def _proj_kernel(x_ref, sc_ref, sh_ref, w_ref, b_ref, tab_ref, z_ref, *more_refs, plain_segs, kv_segs, sep_qrot,
                 rows_t):
    more_refs = list(more_refs)
    x = x_ref[0]
    h = x * (1.0 + sc_ref[0]) + sh_ref[0]
    z = _dot(h.astype(BF16), w_ref[...]) + b_ref[...]
    tab = tab_ref[0]
    tq = [tab[:, i * LANES:(i + 1) * LANES] for i in range(3)]
    tk = [tab[:, i * LANES:(i + 1) * LANES] for i in range(3, 6)]
    q_rot = _rope_apply(z[:, :D_MODEL], *tq)
    z_ref[0] = z
    if sep_qrot:
        more_refs.pop(0)[0] = q_rot
    else:
        z_ref[0, :, :D_MODEL] = q_rot
    for c0 in plain_segs:
        if rows_t:
            more_refs.pop(0)[0] = z[:, c0:c0 + KV_ROW].T
    for c0 in kv_segs:
        seg = _rope_apply(z[:, c0:c0 + KV_ROW], *tk)
        z_ref[0, :, c0:c0 + KV_ROW] = seg
        if rows_t:
            more_refs.pop(0)[0] = seg.T


def _proj(x, sc, sh, w, b, tab, tm, plain_segs, kv_segs, sep_qrot, rows_t):
    g, t, _ = x.shape
    n = w.shape[1]
    tmod = tm if sc.shape[1] == t else 1
    out_shape = [jax.ShapeDtypeStruct((g, t, n), F32)]
    out_specs = [pl.BlockSpec((1, tm, n), lambda a, i: (a, i, 0))]
    if sep_qrot:
        out_shape.append(jax.ShapeDtypeStruct((g, t, D_MODEL), F32))
        out_specs.append(pl.BlockSpec((1, tm, D_MODEL), lambda a, i: (a, i, 0)))
    if rows_t:
        for _ in plain_segs + kv_segs:
            out_shape.append(jax.ShapeDtypeStruct((g, KV_ROW, t), F32))
            out_specs.append(pl.BlockSpec((1, KV_ROW, tm), lambda a, i: (a, 0, i)))
    mod_spec = pl.BlockSpec((1, tmod, D_MODEL), (lambda a, i: (a, i, 0)) if tmod == tm else (lambda a, i: (a, 0, 0)))
    return pl.pallas_call(
        functools.partial(_proj_kernel, plain_segs=plain_segs, kv_segs=kv_segs, sep_qrot=sep_qrot, rows_t=rows_t),
        grid=(g, t // tm),
        in_specs=[pl.BlockSpec((1, tm, D_MODEL), lambda a, i: (a, i, 0)),
                  mod_spec, mod_spec,
                  pl.BlockSpec((D_MODEL, n), lambda a, i: (0, 0)),
                  pl.BlockSpec((1, n), lambda a, i: (0, 0)),
                  pl.BlockSpec((1, tm, 6 * LANES), lambda a, i: (0, i, 0))],
        out_specs=out_specs,
        out_shape=out_shape,
        compiler_params=_cparams(("arbitrary", "arbitrary")),
        name="proj",
    )(x, sc, sh, w, b, tab)


def _outproj_kernel(o_ref, x_ref, g_ref, w_ref, b_ref, lg_ref, lb_ref, out_ref):
    y = _dot(o_ref[0].astype(BF16), w_ref[...]) + b_ref[...]
    r = DN_ALPHA * x_ref[0] + g_ref[0] * y
    out_ref[0] = _layer_norm(r, lg_ref[...], lb_ref[...])


def _outproj_ln(o, x, gate, w, b, lg, lb, tm):
    g, t, _ = x.shape
    tmod = tm if gate.shape[1] == t else 1
    row = pl.BlockSpec((1, tm, D_MODEL), lambda a, i: (a, i, 0))
    vec = pl.BlockSpec((1, D_MODEL), lambda a, i: (0, 0))
    mod_spec = pl.BlockSpec((1, tmod, D_MODEL), (lambda a, i: (a, i, 0)) if tmod == tm else (lambda a, i: (a, 0, 0)))
    return pl.pallas_call(
        _outproj_kernel,
        grid=(g, t // tm),
        in_specs=[row, row, mod_spec,
                  pl.BlockSpec((D_MODEL, D_MODEL), lambda a, i: (0, 0)), vec, vec, vec],
        out_specs=row,
        out_shape=jax.ShapeDtypeStruct((g, t, D_MODEL), F32),
        compiler_params=_cparams(("arbitrary", "arbitrary")),
        name="outproj_ln",
    )(o, x, gate, w, b, lg, lb)


def _router_gates_t(h, rwt_ref, rb_ref):
    logits = _dot_nt(rwt_ref[...], h, precision=lax.Precision.HIGHEST)
    m = jnp.max(logits, axis=0, keepdims=True)
    ex = jnp.exp(logits - m)
    probs = ex / jnp.sum(ex, axis=0, keepdims=True)
    biased = probs + rb_ref[...]
    eidx = lax.broadcasted_iota(jnp.int32, biased.shape, 0)
    best = None
    g_sel = None
    for g in range(N_GROUPS):
        rows = [biased[g * EXPERTS_PER_GROUP + i:g * EXPERTS_PER_GROUP + i + 1] for i in range(EXPERTS_PER_GROUP)]
        gs = None
        for i in range(EXPERTS_PER_GROUP):
            for j in range(i + 1, EXPERTS_PER_GROUP):
                pair = rows[i] + rows[j]
                gs = pair if gs is None else jnp.maximum(gs, pair)
        if best is None:
            best, g_sel = gs, jnp.zeros(gs.shape, jnp.int32)
        else:
            upd = gs > best
            g_sel = jnp.where(upd, g, g_sel)
            best = jnp.maximum(best, gs)
    v = jnp.where(eidx // EXPERTS_PER_GROUP == g_sel, biased, NEG_INF)
    m1 = jnp.max(v, axis=0, keepdims=True)
    i1 = jnp.min(jnp.where(v == m1, eidx, N_EXPERTS), axis=0, keepdims=True)
    v2 = jnp.where(eidx == i1, -jnp.inf, v)
    m2 = jnp.max(v2, axis=0, keepdims=True)
    i2 = jnp.min(jnp.where(v2 == m2, eidx, N_EXPERTS), axis=0, keepdims=True)
    w1 = jnp.sum(jnp.where(eidx == i1, probs, 0.0), axis=0, keepdims=True)
    w2 = jnp.sum(jnp.where(eidx == i2, probs, 0.0), axis=0, keepdims=True)
    tot = w1 + w2
    return jnp.where(eidx == i1, w1 / tot, 0.0) + jnp.where(eidx == i2, w2 / tot, 0.0), g_sel


def _moe_kernel(x_ref, sc_ref, sh_ref, g_ref, rwt_ref, rb_ref, wg_ref, wu_ref, wd_ref, lg_ref, lb_ref,
                out_ref, hs_scr, gate_scr, ys_scr, unperm_scr, seg_smem, *, sub):
    e = pl.program_id(2)
    tm = hs_scr.shape[0]

    @pl.when(e == 0)
    def _():
        h = x_ref[0] * (1.0 + sc_ref[0]) + sh_ref[0]
        gates_t, g_sel = _router_gates_t(h, rwt_ref, rb_ref)
        grp = lax.broadcasted_iota(jnp.int32, (SUBLANES, tm), 0)
        onehot = jnp.where(grp == g_sel, 1.0, 0.0)
        src = lax.broadcasted_iota(jnp.int32, (tm, tm), 0)
        dst = lax.broadcasted_iota(jnp.int32, (tm, tm), 1)
        earlier = jnp.where(src < dst, 1.0, 0.0).astype(BF16)
        cum = _dot(onehot.astype(BF16), earlier)
        pos = jnp.zeros((1, tm), F32)
        off = jnp.int32(0)
        for r in range(N_GROUPS):
            cnt = jnp.sum(onehot[r:r + 1, :]).astype(jnp.int32)
            seg_smem[r] = off
            seg_smem[N_GROUPS + r] = cnt
            pos = pos + onehot[r:r + 1, :] * (cum[r:r + 1, :] + off.astype(F32))
            off = off + cnt
        perm = jnp.where(src == pos.astype(jnp.int32), 1.0, 0.0)
        hs_scr[...] = _dot(perm.astype(BF16), h.astype(BF16)).astype(BF16)
        gs_t = _dot_nt(gates_t, perm, precision=lax.Precision.HIGHEST)
        pad = jnp.zeros((LANES - N_EXPERTS, tm), F32)
        gate_scr[...] = jnp.concatenate([gs_t, pad], axis=0).T
        unperm_scr[...] = perm.T.astype(BF16)
        ys_scr[...] = jnp.zeros_like(ys_scr)

    grp_e = e // EXPERTS_PER_GROUP
    off = seg_smem[grp_e]
    cnt = seg_smem[N_GROUPS + grp_e]
    first = off // sub
    last = (off + cnt - 1) // sub
    for j in range(tm // sub):
        @pl.when((cnt > 0) & (first <= j) & (j <= last))
        def _():
            rows = hs_scr[j * sub:(j + 1) * sub, :]
            a = _dot(rows, wg_ref[0])
            u = _dot(rows, wu_ref[0])
            gates = gate_scr[j * sub:(j + 1) * sub, :]
            lane = lax.broadcasted_iota(jnp.int32, gates.shape, 1)
            gcol = jnp.sum(jnp.where(lane == e, gates, 0.0), axis=1, keepdims=True)
            act = (a * jax.nn.sigmoid(a)) * u * gcol
            ys_scr[j * sub:(j + 1) * sub, :] += _dot(act.astype(BF16), wd_ref[0])

    @pl.when(e == N_EXPERTS - 1)
    def _():
        ys = ys_scr[...]
        hi = ys.astype(BF16)
        lo = (ys - hi.astype(F32)).astype(BF16)
        unperm = unperm_scr[...]
        y = _dot(unperm, hi) + _dot(unperm, lo)
        r = DN_ALPHA * x_ref[0] + g_ref[0] * y
        out_ref[0] = _layer_norm(r, lg_ref[...], lb_ref[...])


def _moe_ln(x, sc, sh, gate, rwt, rb, wg, wu, wd, lg, lb, tm):
    g, t, _ = x.shape
    sub = min(MOE_SUB_TILE, tm)
    tmod = tm if sc.shape[1] == t else 1
    row = pl.BlockSpec((1, tm, D_MODEL), lambda a, i, e: (a, i, 0))
    vec = pl.BlockSpec((1, D_MODEL), lambda a, i, e: (0, 0))
    mod_spec = pl.BlockSpec((1, tmod, D_MODEL),
                            (lambda a, i, e: (a, i, 0)) if tmod == tm else (lambda a, i, e: (a, 0, 0)))
    return pl.pallas_call(
        functools.partial(_moe_kernel, sub=sub),
        grid=(g, t // tm, N_EXPERTS),
        in_specs=[row, mod_spec, mod_spec, mod_spec,
                  pl.BlockSpec((N_EXPERTS, D_MODEL), lambda a, i, e: (0, 0)),
                  pl.BlockSpec((N_EXPERTS, 1), lambda a, i, e: (0, 0)),
                  pl.BlockSpec((1, D_MODEL, D_FF_EXPERT), lambda a, i, e: (e, 0, 0)),
                  pl.BlockSpec((1, D_MODEL, D_FF_EXPERT), lambda a, i, e: (e, 0, 0)),
                  pl.BlockSpec((1, D_FF_EXPERT, D_MODEL), lambda a, i, e: (e, 0, 0)),
                  vec, vec],
        out_specs=row,
        out_shape=jax.ShapeDtypeStruct((g, t, D_MODEL), F32),
        scratch_shapes=[pltpu.VMEM((tm, D_MODEL), BF16),
                        pltpu.VMEM((tm, LANES), F32),
                        pltpu.VMEM((tm, D_MODEL), F32),
                        pltpu.VMEM((tm, tm), BF16),
                        pltpu.SMEM((2 * N_GROUPS,), jnp.int32)],
        compiler_params=_cparams(("arbitrary", "arbitrary", "arbitrary")),
        name="moe_ln",
    )(x, sc, sh, gate, rwt, rb, wg, wu, wd, lg, lb)


def _compress_blocks(chunks, w1r, pb, w2):
    nch = chunks.shape[0]
    part = _dot(chunks, w1r)
    pre = part[:, :CMP_HIDDEN] + pltpu.roll(part[:, CMP_HIDDEN:], nch - 1, 0)
    hid = jax.nn.gelu(pre + pb)
    return _dot(hid.astype(BF16), w2)


def _compress_kernel(ch_ref, w1r_ref, pb_ref, w2_ref, o_ref):
    for j in range(2 * N_KV):
        i = j % 2
        o_ref[0, j] = _compress_blocks(ch_ref[0, j].astype(BF16), w1r_ref[i], pb_ref[i], w2_ref[i])


def _compress_prompt(chunks, w1r, pb, w2):
    b, _, nch, _ = chunks.shape
    return pl.pallas_call(
        _compress_kernel,
        grid=(b,),
        in_specs=[pl.BlockSpec((1, 2 * N_KV, nch, CHUNK_W), lambda a: (a, 0, 0, 0)),
                  pl.BlockSpec((2, CHUNK_W, 2 * CMP_HIDDEN), lambda a: (0, 0, 0)),
                  pl.BlockSpec((2, 1, CMP_HIDDEN), lambda a: (0, 0, 0)),
                  pl.BlockSpec((2, CMP_HIDDEN, HEAD_DIM), lambda a: (0, 0, 0))],
        out_specs=pl.BlockSpec((1, 2 * N_KV, nch, HEAD_DIM), lambda a: (a, 0, 0, 0)),
        out_shape=jax.ShapeDtypeStruct((b, 2 * N_KV, nch, HEAD_DIM), F32),
        compiler_params=_cparams(("arbitrary",)),
        name="compress_prompt",
    )(chunks, w1r, pb, w2)


def _stack_heads(q):
    return jnp.concatenate([q[:, g * HEAD_DIM:(g + 1) * HEAD_DIM] for g in range(N_G)], axis=0)


def _chunk_score_matrix_t(n_sel, n_cmp):
    j = lax.broadcasted_iota(jnp.int32, (n_sel, n_cmp), 0)
    c = lax.broadcasted_iota(jnp.int32, (n_sel, n_cmp), 1)
    lo = j * CHUNKS_PER_SEL
    hi = lo + CHUNKS_PER_SEL - 1
    m = jnp.zeros((n_sel, n_cmp), F32)
    for r in range(CMP_SHIFTS):
        m = m + jnp.where((c + r >= lo) & (c + r <= hi), 1.0, 0.0)
    return m


def _nsa_prompt_kernel(q_ref, qr_ref, cb_ref, slc_ref, win_ref, gate_ref, o_ref, m_scr, acc_scr, *, tq, kc):
    qi = pl.program_id(2)
    t0 = qi * tq
    n_cmp = cb_ref.shape[2]
    n_sel = SEQ // SEL_BLOCK
    tpos = t0 + lax.broadcasted_iota(jnp.int32, (tq, 1), 0)

    qs = _stack_heads(q_ref[0]).astype(BF16)
    k_c = cb_ref[0, 0].astype(BF16)
    v_c = cb_ref[0, 1].astype(BF16)
    s = (_dot_nt(qs, k_c) * ATTN_SCALE).reshape(N_G, tq, n_cmp)
    cend = lax.broadcasted_iota(jnp.int32, (tq, n_cmp), 1) * CMP_STRIDE + CMP_BLOCK - 1
    p = _masked_probs(s, (cend <= tpos)[None])
    o_cmp = _dot(p.reshape(N_G * tq, n_cmp).astype(BF16), v_c)
    imp = jnp.sum(p, axis=0)

    score_t = _dot_nt(_chunk_score_matrix_t(n_sel, n_cmp), imp, precision=lax.Precision.HIGHEST)
    jj = lax.broadcasted_iota(jnp.int32, (n_sel, tq), 0)
    tt = t0 + lax.broadcasted_iota(jnp.int32, (n_sel, tq), 1)
    cur = tt // SEL_BLOCK
    forced = (jj == 0) | (jj == cur) | (jj == cur - 1)
    sc = jnp.where(jj * SEL_BLOCK <= tt, jnp.where(forced, FORCE_SCORE, score_t), NEG_INF)
    rank = jnp.zeros((n_sel, tq), F32)
    for i in range(n_sel):
        ri = sc[i:i + 1]
        rank = rank + jnp.where(ri > sc, 1.0, jnp.where((ri == sc) & (jj > i), 1.0, 0.0))
    sel_t = jnp.where(rank < SEL_TOP, 1.0, 0.0)
    sel = jnp.concatenate([sel_t, jnp.zeros((LANES - n_sel, tq), F32)], axis=0).T.astype(BF16)

    qrs = (_stack_heads(qr_ref[0]) * (ATTN_SCALE * LOG2E)).astype(BF16)

    m_scr[...] = jnp.full(m_scr.shape, NEG_INF, F32)
    acc_scr[...] = jnp.zeros(acc_scr.shape, F32)

    def body(c, carry):
        k0 = pl.multiple_of(c * kc, kc)
        rows = slc_ref[0, pl.ds(k0, kc), :]
        k = rows[:, :HEAD_DIM].astype(BF16)
        sk = _dot_nt(qrs, k).reshape(N_G, tq, kc)
        kpos = k0 + lax.broadcasted_iota(jnp.int32, (LANES, kc), 1)
        expand = jnp.where(kpos // SEL_BLOCK == lax.broadcasted_iota(jnp.int32, (LANES, kc), 0), 1.0, 0.0)
        chosen = _dot(sel, expand.astype(BF16))
        sm = sk + jnp.where((chosen > 0.5) & (kpos[:1] <= tpos), 0.0, NEG_INF)[None]
        m = m_scr[...]
        m_new = jnp.maximum(m, jnp.max(sm, axis=-1, keepdims=True))
        e = jnp.exp2(sm - m_new)
        pv = _dot(e.reshape(N_G * tq, kc).astype(BF16), _values_and_ones(rows)).reshape(N_G, tq, LANES)
        acc_scr[...] = jnp.exp2(m - m_new) * acc_scr[...] + pv
        m_scr[...] = m_new
        return carry

    lax.fori_loop(0, (t0 + tq + kc - 1) // kc, body, 0)
    acc = acc_scr[...]
    o_slc = acc[:, :, :HEAD_DIM] / jnp.maximum(acc[:, :, HEAD_DIM:HEAD_DIM + 1], TINY)

    o_win = _band_attend(qrs, win_ref, t0, tq, A_WINDOW, None)

    gt = jax.nn.sigmoid(gate_ref[0])
    o_cmp = o_cmp.reshape(N_G, tq, HEAD_DIM)
    for g in range(N_G):
        og = (gt[:, g:g + 1] * o_cmp[g] + gt[:, N_G + g:N_G + g + 1] * o_slc[g]
              + gt[:, 2 * N_G + g:2 * N_G + g + 1] * o_win[g])
        o_ref[0, :, g * HEAD_DIM:(g + 1) * HEAD_DIM] = og


def _band_attend(qrs, rows_ref, t0, tq, window, sinks):
    span = window + tq
    start = pl.multiple_of(jnp.maximum(t0 - window, 0), tq)
    rows = rows_ref[0, pl.ds(start, span), :]
    k = rows[:, :HEAD_DIM].astype(BF16)
    d = (t0 + lax.broadcasted_iota(jnp.int32, (tq, span), 0)) - (start + lax.broadcasted_iota(jnp.int32, (tq, span), 1))
    s = _dot_nt(qrs, k).reshape(N_G, tq, span) + jnp.where((d >= 0) & (d <= window), 0.0, NEG_INF)[None]
    m = jnp.max(s, axis=-1, keepdims=True)
    if sinks is not None:
        sink = jnp.concatenate([jnp.full((1, tq, 1), sk, F32) for sk in sinks], axis=0)
        m = jnp.maximum(m, sink)
    e = jnp.exp2(s - m)
    den = jnp.sum(e, axis=-1, keepdims=True)
    if sinks is not None:
        den = den + jnp.exp2(sink - m)
    o = _dot(e.reshape(N_G * tq, span).astype(BF16), rows[:, HEAD_DIM:].astype(BF16)).reshape(N_G, tq, HEAD_DIM)
    return o / jnp.maximum(den, TINY)


def _values_and_ones(rows):
    lane = lax.broadcasted_iota(jnp.int32, rows.shape, 1)
    return jnp.where(lane < HEAD_DIM, pltpu.roll(rows, HEAD_DIM, 1), 1.0).astype(BF16)


def _nsa_prompt_attend(z, q_rot, cblocks, tq=128, kc=512):
    b, t, _ = z.shape
    n_cmp = cblocks.shape[2]
    hw = N_G * HEAD_DIM
    blk = lambda base: pl.BlockSpec((1, t, LANES), lambda a, k, i: (a, 0, base // LANES + k))
    return pl.pallas_call(
        functools.partial(_nsa_prompt_kernel, tq=tq, kc=kc),
        grid=(b, N_KV, t // tq),
        in_specs=[pl.BlockSpec((1, tq, hw), lambda a, k, i: (a, i, k)),
                  pl.BlockSpec((1, tq, hw), lambda a, k, i: (a, i, k)),
                  pl.BlockSpec((1, 2, n_cmp, HEAD_DIM), lambda a, k, i: (a, k, 0, 0)),
                  blk(A_SLC0), blk(A_WIN0),
                  pl.BlockSpec((1, tq, LANES), lambda a, k, i: (a, i, A_GATE0 // LANES + k))],
        out_specs=pl.BlockSpec((1, tq, hw), lambda a, k, i: (a, i, k)),
        out_shape=jax.ShapeDtypeStruct((b, t, D_MODEL), F32),
        scratch_shapes=[pltpu.VMEM((N_G, tq, 1), F32), pltpu.VMEM((N_G, tq, LANES), F32)],
        compiler_params=_cparams(("arbitrary", "arbitrary", "arbitrary")),
        name="nsa_prompt_attend",
    )(z, q_rot, cblocks, z, z, z)


def _swa_prompt_kernel(sink_ref, q_ref, rows_ref, o_ref, *, tq):
    kv = pl.program_id(1)
    t0 = pl.program_id(2) * tq
    qrs = (_stack_heads(q_ref[0]) * (ATTN_SCALE * LOG2E)).astype(BF16)
    sinks = [sink_ref[kv, g] * LOG2E for g in range(N_G)]
    o = _band_attend(qrs, rows_ref, t0, tq, B_WINDOW, sinks)
    for g in range(N_G):
        o_ref[0, :, g * HEAD_DIM:(g + 1) * HEAD_DIM] = o[g]


def _swa_prompt_attend(z, sinks, tq=128):
    b, t, _ = z.shape
    hw = N_G * HEAD_DIM
    return pl.pallas_call(
        functools.partial(_swa_prompt_kernel, tq=tq),
        grid=(b, N_KV, t // tq),
        in_specs=[pl.BlockSpec(memory_space=pltpu.SMEM),
                  pl.BlockSpec((1, tq, hw), lambda a, k, i: (a, i, k)),
                  pl.BlockSpec((1, t, LANES), lambda a, k, i: (a, 0, D_MODEL // LANES + k))],
        out_specs=pl.BlockSpec((1, tq, hw), lambda a, k, i: (a, i, k)),
        out_shape=jax.ShapeDtypeStruct((b, t, D_MODEL), F32),
        compiler_params=_cparams(("arbitrary", "arbitrary", "arbitrary")),
        name="swa_prompt_attend",
    )(sinks.reshape(N_KV, N_G), z, z)


S_CHUNKS = 520
S_ROWS = S_CHUNKS * CMP_STRIDE
S_SEL = 256
N_SEL_SAMPLE = PAST_LEN // SEL_BLOCK + 1
SLABS = CMP_STRIDE // SUBLANES
PAGES_PER_ITER = 4
assert N_SEL_SAMPLE == LANES + 1 and DEC_SEQ == SUBLANES and N_PAGES % PAGES_PER_ITER == 0


def _page_copy(cache_hbm, ia, pt_ref, n, p, buf, slot, sem):
    return pltpu.make_async_copy(cache_hbm.at[ia, pt_ref[n, p]],
                                 buf.at[slot, :, pl.ds(pl.multiple_of(p * PAGE_SIZE, PAGE_SIZE), PAGE_SIZE)],
                                 sem.at[slot])


def _gather_pages(cache_hbm, ia, pt_ref, buf, sem):
    n = pl.program_id(0)
    nn = pl.num_programs(0)
    slot = n % 2

    def start(seq, sl):
        def go(p, c):
            _page_copy(cache_hbm, ia, pt_ref, seq, p, buf, sl, sem).start()
            return c
        lax.fori_loop(0, N_PAGES, go, 0)

    @pl.when(n == 0)
    def _():
        start(0, 0)

    @pl.when(n + 1 < nn)
    def _():
        start(n + 1, 1 - slot)

    def wait(p, c):
        _page_copy(cache_hbm, ia, pt_ref, n, p, buf, slot, sem).wait()
        return c
    lax.fori_loop(0, N_PAGES, wait, 0)
    return slot


def _sample_q_rows(q, kv):
    return jnp.concatenate([q[:, (kv * N_G + g) * HEAD_DIM:(kv * N_G + g + 1) * HEAD_DIM] for g in range(N_G)], axis=0)


def _sample_q_blockdiag(q):
    parts = []
    for kv in range(N_KV):
        qs = _sample_q_rows(q, kv)
        left = jnp.zeros((qs.shape[0], kv * 2 * HEAD_DIM), F32)
        right = jnp.zeros((qs.shape[0], KV_ROW - kv * 2 * HEAD_DIM - HEAD_DIM), F32)
        parts.append(jnp.concatenate([left, qs, right], axis=1) if kv else jnp.concatenate([qs, right], axis=1))
    return jnp.concatenate(parts, axis=0)


def _nsa_sample_cmp_kernel(pt_ref, new_ref, q_ref, w1r_ref, pb_ref, w2_ref, cache_hbm,
                           ocmp_ref, sel_ref, pages, rows, chunks, sem, *, ia):
    slot = _gather_pages(cache_hbm, ia, pt_ref, pages, sem)

    def to_rows(it, c):
        for u in range(PAGES_PER_ITER):
            p = it * PAGES_PER_ITER + u
            r0 = pl.multiple_of(p * PAGE_SIZE, PAGE_SIZE)
            c0 = pl.multiple_of(p * (PAGE_SIZE // SLABS), PAGE_SIZE // SLABS)
            for kv in range(N_KV):
                t = pages[slot, kv * LANES:(kv + 1) * LANES, pl.ds(r0, PAGE_SIZE)].T
                for v in range(PAGE_SIZE // SUBLANES):
                    rows[kv, v % SLABS, pl.ds(c0 + (v // SLABS) * SUBLANES, SUBLANES), :] = t[v * SUBLANES:(v + 1) * SUBLANES]
        return c
    lax.fori_loop(0, N_PAGES // PAGES_PER_ITER, to_rows, 0)
    new_rows = new_ref[0]
    tail = (PAST_LEN // CMP_STRIDE) * SUBLANES
    for kv in range(N_KV):
        rows[kv, 0, tail:tail + DEC_SEQ, :] = new_rows[:, kv * LANES:(kv + 1) * LANES]
        rows[kv, 0, tail + DEC_SEQ:, :] = jnp.zeros((S_CHUNKS * SUBLANES - tail - DEC_SEQ, LANES), F32)
        for slab in range(1, SLABS):
            rows[kv, slab, tail:, :] = jnp.zeros((S_CHUNKS * SUBLANES - tail, LANES), F32)

    lo = lax.broadcasted_iota(jnp.int32, (S_CHUNKS, LANES), 1) < HEAD_DIM
    for k in range(CMP_STRIDE // 2):
        for kv in range(N_KV):
            sa, sb = 2 * k, 2 * k + 1
            a = rows[kv, sa // SUBLANES, pl.ds(sa % SUBLANES, S_CHUNKS, stride=SUBLANES), :]
            b = rows[kv, sb // SUBLANES, pl.ds(sb % SUBLANES, S_CHUNKS, stride=SUBLANES), :]
            chunks[2 * kv, :, k * LANES:(k + 1) * LANES] = jnp.where(lo, a, pltpu.roll(b, HEAD_DIM, 1)).astype(BF16)
            chunks[2 * kv + 1, :, k * LANES:(k + 1) * LANES] = jnp.where(lo, pltpu.roll(a, HEAD_DIM, 1), b).astype(BF16)

    q = q_ref[0]
    qpos = PAST_LEN + lax.broadcasted_iota(jnp.int32, (DEC_SEQ, 1), 0)
    cend = lax.broadcasted_iota(jnp.int32, (DEC_SEQ, S_CHUNKS), 1) * CMP_STRIDE + CMP_BLOCK - 1
    cmask = (cend <= qpos)[None]
    imps = []
    for kv in range(N_KV):
        k_c = _compress_blocks(chunks[2 * kv], w1r_ref[0], pb_ref[0], w2_ref[0]).astype(BF16)
        v_c = _compress_blocks(chunks[2 * kv + 1], w1r_ref[1], pb_ref[1], w2_ref[1]).astype(BF16)
        qs = _sample_q_rows(q, kv).astype(BF16)
        s = (_dot_nt(qs, k_c) * ATTN_SCALE).reshape(N_G, DEC_SEQ, S_CHUNKS)
        p = _masked_probs(s, cmask)
        o = _dot(p.reshape(N_G * DEC_SEQ, S_CHUNKS).astype(BF16), v_c)
        for g in range(N_G):
            h = kv * N_G + g
            ocmp_ref[0, :, h * HEAD_DIM:(h + 1) * HEAD_DIM] = o[g * DEC_SEQ:(g + 1) * DEC_SEQ]
        imps.append(jnp.sum(p, axis=0))
    imp = jnp.concatenate(imps, axis=0)

    score = _dot_nt(imp, _chunk_score_matrix_t(S_SEL, S_CHUNKS), precision=lax.Precision.HIGHEST)
    shape = score.shape
    jj = lax.broadcasted_iota(jnp.int32, shape, 1)
    tt = PAST_LEN + lax.broadcasted_iota(jnp.int32, shape, 0) % DEC_SEQ
    cur = tt // SEL_BLOCK
    forced = (jj == 0) | (jj == cur) | (jj == cur - 1)
    sc = jnp.where(jj * SEL_BLOCK <= tt, jnp.where(forced, FORCE_SCORE, score), NEG_INF)

    sc_a = sc[:, :LANES]
    extra = jnp.sum(jnp.where(jj == LANES, sc, 0.0), axis=1, keepdims=True)
    lane = lax.broadcasted_iota(jnp.int32, sc_a.shape, 1)
    rank = jnp.where(extra > sc_a, 1.0, 0.0)
    for k in range(1, LANES):
        r = pltpu.roll(sc_a, k, 1)
        rank = rank + jnp.where(r > sc_a, 1.0, jnp.where(r == sc_a, jnp.where(lane >= k, 1.0, 0.0), 0.0))
    rank_x = jnp.sum(jnp.where(sc_a >= extra, 1.0, 0.0), axis=1, keepdims=True)
    sel_ref[0, :, :LANES] = jnp.where(rank < SEL_TOP, 1.0, 0.0)
    sel_ref[0, :, LANES:] = jnp.where((lane == 0) & (rank_x < SEL_TOP), 1.0, 0.0)


def _nsa_sample_cmp(ia, page_table, cache, new_rows, q, w1r, pb, w2):
    n = q.shape[0]
    grid_spec = pltpu.PrefetchScalarGridSpec(
        num_scalar_prefetch=1,
        grid=(n,),
        in_specs=[pl.BlockSpec((1, DEC_SEQ, KV_ROW), lambda a, pt: (a, 0, 0)),
                  pl.BlockSpec((1, DEC_SEQ, D_MODEL), lambda a, pt: (a, 0, 0)),
                  pl.BlockSpec((2, CHUNK_W, 2 * CMP_HIDDEN), lambda a, pt: (0, 0, 0)),
                  pl.BlockSpec((2, 1, CMP_HIDDEN), lambda a, pt: (0, 0, 0)),
                  pl.BlockSpec((2, CMP_HIDDEN, HEAD_DIM), lambda a, pt: (0, 0, 0)),
                  pl.BlockSpec(memory_space=pl.ANY)],
        out_specs=[pl.BlockSpec((1, DEC_SEQ, D_MODEL), lambda a, pt: (a, 0, 0)),
                   pl.BlockSpec((1, N_KV * DEC_SEQ, S_SEL), lambda a, pt: (a, 0, 0))],
        scratch_shapes=[pltpu.VMEM((2, KV_ROW, PAST_LEN), F32),
                        pltpu.VMEM((N_KV, SLABS, S_CHUNKS * SUBLANES, LANES), F32),
                        pltpu.VMEM((2 * N_KV, S_CHUNKS, CHUNK_W), BF16),
                        pltpu.SemaphoreType.DMA((2,))])
    return pl.pallas_call(
        functools.partial(_nsa_sample_cmp_kernel, ia=ia),
        grid_spec=grid_spec,
        out_shape=[jax.ShapeDtypeStruct((n, DEC_SEQ, D_MODEL), F32),
                   jax.ShapeDtypeStruct((n, N_KV * DEC_SEQ, S_SEL), F32)],
        compiler_params=_cparams(("arbitrary",)),
        name="nsa_sample_cmp",
    )(page_table, new_rows, q, w1r, pb, w2, cache)


def _window_cached_attend(qbd, wt, new, window, sink):
    nq = qbd.shape[0]
    heads = nq // DEC_SEQ
    qpos = PAST_LEN + lax.broadcasted_iota(jnp.int32, (DEC_SEQ, 1), 0)
    d_p = qpos - (PAST_LEN - window + lax.broadcasted_iota(jnp.int32, (DEC_SEQ, window), 1))
    d_n = qpos - (PAST_LEN + lax.broadcasted_iota(jnp.int32, (DEC_SEQ, DEC_SEQ), 1))
    s_p = _dot(qbd, wt).reshape(heads, DEC_SEQ, window) + jnp.where((d_p >= 0) & (d_p <= window), 0.0, NEG_INF)[None]
    s_n = _dot_nt(qbd, new).reshape(heads, DEC_SEQ, DEC_SEQ) + jnp.where((d_n >= 0) & (d_n <= window), 0.0, NEG_INF)[None]
    m = jnp.maximum(jnp.max(s_p, axis=-1, keepdims=True), jnp.max(s_n, axis=-1, keepdims=True))
    if sink is not None:
        m = jnp.maximum(m, sink)
    e_p = jnp.exp(s_p - m)
    e_n = jnp.exp(s_n - m)
    den = jnp.sum(e_p, axis=-1, keepdims=True) + jnp.sum(e_n, axis=-1, keepdims=True)
    if sink is not None:
        den = den + jnp.exp(sink - m)
    o = _dot_nt(e_p.reshape(nq, window).astype(BF16), wt) + _dot(e_n.reshape(nq, DEC_SEQ).astype(BF16), new)
    return o / jnp.maximum(den.reshape(nq, 1), TINY)


def _nsa_sample_slc_kernel(pt_ref, new_ref, qr_ref, sel_ref, wbuf_ref, wnew_ref, gate_ref, ocmp_ref, cache_hbm,
                           o_ref, pages, pages_bf, sem, *, ia):
    slot = _gather_pages(cache_hbm, ia, pt_ref, pages, sem)
    pages_bf[...] = pages[slot].astype(BF16)

    nq = N_KV * N_G * DEC_SEQ
    qbd = (_sample_q_blockdiag(qr_ref[0]) * ATTN_SCALE).astype(BF16)
    qpos = PAST_LEN + lax.broadcasted_iota(jnp.int32, (DEC_SEQ, 1), 0)

    past = pages_bf[...]
    new = new_ref[0].astype(BF16)
    s_p = _dot(qbd, past).reshape(N_KV, N_G, DEC_SEQ, PAST_LEN)
    s_n = _dot_nt(qbd, new).reshape(N_KV, N_G, DEC_SEQ, DEC_SEQ)
    sel = sel_ref[0]
    sel_bf = sel.astype(BF16)
    kw = PAST_LEN // 4
    chosen = []
    for c in range(4):
        kp = c * kw + lax.broadcasted_iota(jnp.int32, (S_SEL, kw), 1)
        expand = jnp.where(kp // SEL_BLOCK == lax.broadcasted_iota(jnp.int32, (S_SEL, kw), 0), 1.0, 0.0)
        chosen.append(_dot(sel_bf, expand.astype(BF16)))
    chosen = jnp.concatenate(chosen, axis=1).reshape(N_KV, 1, DEC_SEQ, PAST_LEN)
    bias_p = jnp.where(chosen > 0.5, 0.0, NEG_INF)
    new_blk = sel[:, PAST_LEN // SEL_BLOCK:PAST_LEN // SEL_BLOCK + 1].reshape(N_KV, 1, DEC_SEQ, 1)
    npos = PAST_LEN + lax.broadcasted_iota(jnp.int32, (DEC_SEQ, DEC_SEQ), 1)
    bias_n = jnp.where((new_blk > 0.5) & (npos <= qpos)[None, None], 0.0, NEG_INF)
    s_p = s_p + bias_p
    s_n = s_n + bias_n
    m = jnp.maximum(jnp.max(s_p, axis=-1, keepdims=True), jnp.max(s_n, axis=-1, keepdims=True))
    e_p = jnp.exp(s_p - m)
    e_n = jnp.exp(s_n - m)
    den = jnp.sum(e_p, axis=-1, keepdims=True) + jnp.sum(e_n, axis=-1, keepdims=True)
    o_slc = (_dot_nt(e_p.reshape(nq, PAST_LEN).astype(BF16), past)
             + _dot(e_n.reshape(nq, DEC_SEQ).astype(BF16), new)) / jnp.maximum(den.reshape(nq, 1), TINY)

    o_win = _window_cached_attend(qbd, wbuf_ref[0].astype(BF16), wnew_ref[0].astype(BF16), A_WINDOW, None)

    gt = jax.nn.sigmoid(gate_ref[0])
    o_cmp = ocmp_ref[0]
    for kv in range(N_KV):
        v0 = kv * 2 * HEAD_DIM + HEAD_DIM
        for g in range(N_G):
            h = kv * N_G + g
            r0 = h * DEC_SEQ
            c0 = kv * LANES + g
            og = (gt[:, c0:c0 + 1] * o_cmp[:, h * HEAD_DIM:(h + 1) * HEAD_DIM]
                  + gt[:, c0 + N_G:c0 + N_G + 1] * o_slc[r0:r0 + DEC_SEQ, v0:v0 + HEAD_DIM]
                  + gt[:, c0 + 2 * N_G:c0 + 2 * N_G + 1] * o_win[r0:r0 + DEC_SEQ, v0:v0 + HEAD_DIM])
            o_ref[0, :, h * HEAD_DIM:(h + 1) * HEAD_DIM] = og


def _nsa_sample_slc(ia, page_table, cache, z, q_rot, sel, win_cache, o_cmp):
    n = z.shape[0]
    row_blk = lambda base: pl.BlockSpec((1, DEC_SEQ, KV_ROW), lambda a, pt: (a, 0, base // KV_ROW))
    grid_spec = pltpu.PrefetchScalarGridSpec(
        num_scalar_prefetch=1,
        grid=(n,),
        in_specs=[row_blk(A_SLC0),
                  pl.BlockSpec((1, DEC_SEQ, D_MODEL), lambda a, pt: (a, 0, 0)),
                  pl.BlockSpec((1, N_KV * DEC_SEQ, S_SEL), lambda a, pt: (a, 0, 0)),
                  pl.BlockSpec((None, 1, KV_ROW, A_WINDOW), lambda a, pt: (ia, a, 0, 0)),
                  row_blk(A_WIN0), row_blk(A_GATE0),
                  pl.BlockSpec((1, DEC_SEQ, D_MODEL), lambda a, pt: (a, 0, 0)),
                  pl.BlockSpec(memory_space=pl.ANY)],
        out_specs=pl.BlockSpec((1, DEC_SEQ, D_MODEL), lambda a, pt: (a, 0, 0)),
        scratch_shapes=[pltpu.VMEM((2, KV_ROW, PAST_LEN), F32),
                        pltpu.VMEM((KV_ROW, PAST_LEN), BF16),
                        pltpu.SemaphoreType.DMA((2,))])
    return pl.pallas_call(
        functools.partial(_nsa_sample_slc_kernel, ia=ia),
        grid_spec=grid_spec,
        out_shape=jax.ShapeDtypeStruct((n, DEC_SEQ, D_MODEL), F32),
        compiler_params=_cparams(("arbitrary",)),
        name="nsa_sample_slc",
    )(page_table, z, q_rot, sel, win_cache, z, z, o_cmp, cache)


def _swa_sample_kernel(q_ref, new_ref, wbuf_ref, sink_ref, o_ref):
    qbd = (_sample_q_blockdiag(q_ref[0]) * ATTN_SCALE).astype(BF16)
    o = _window_cached_attend(qbd, wbuf_ref[0].astype(BF16), new_ref[0].astype(BF16), B_WINDOW, sink_ref[...])
    for kv in range(N_KV):
        v0 = kv * 2 * HEAD_DIM + HEAD_DIM
        for g in range(N_G):
            h = kv * N_G + g
            o_ref[0, :, h * HEAD_DIM:(h + 1) * HEAD_DIM] = o[h * DEC_SEQ:(h + 1) * DEC_SEQ, v0:v0 + HEAD_DIM]


def _swa_sample_attend(ib, z, win_cache, sinks):
    n = z.shape[0]
    return pl.pallas_call(
        _swa_sample_kernel,
        grid=(n,),
        in_specs=[pl.BlockSpec((1, DEC_SEQ, D_MODEL), lambda a: (a, 0, 0)),
                  pl.BlockSpec((1, DEC_SEQ, KV_ROW), lambda a: (a, 0, D_MODEL // KV_ROW)),
                  pl.BlockSpec((None, 1, KV_ROW, B_WINDOW), lambda a: (ib, a, 0, 0)),
                  pl.BlockSpec((N_HEADS, 1, 1), lambda a: (0, 0, 0))],
        out_specs=pl.BlockSpec((1, DEC_SEQ, D_MODEL), lambda a: (a, 0, 0)),
        out_shape=jax.ShapeDtypeStruct((n, DEC_SEQ, D_MODEL), F32),
        compiler_params=_cparams(("arbitrary",)),
        name="swa_sample_attend",
    )(z, z, win_cache, sinks.reshape(N_HEADS, 1, 1))


def _nsa_col_perm():
    perm = np.full((A_COLS,), -1, np.int64)
    perm[:D_MODEL] = np.arange(D_MODEL)
    kvw = N_KV * HEAD_DIM
    for br, base in enumerate((A_CMP0, A_SLC0, A_WIN0)):
        old0 = D_MODEL + 2 * kvw * br
        for kv in range(N_KV):
            for i in range(2):
                dst = base + kv * 2 * HEAD_DIM + i * HEAD_DIM
                src = old0 + i * kvw + kv * HEAD_DIM
                perm[dst:dst + HEAD_DIM] = np.arange(src, src + HEAD_DIM)
    old_g = D_MODEL + 6 * kvw
    for kv in range(N_KV):
        for br in range(3):
            dst = A_GATE0 + kv * LANES + br * N_G
            src = old_g + br * N_HEADS + kv * N_G
            perm[dst:dst + N_G] = np.arange(src, src + N_G)
    return perm


def _swa_col_perm():
    perm = np.zeros((B_COLS,), np.int64)
    perm[:D_MODEL] = np.arange(D_MODEL)
    kvw = N_KV * HEAD_DIM
    for kv in range(N_KV):
        for i in range(2):
            dst = D_MODEL + kv * 2 * HEAD_DIM + i * HEAD_DIM
            src = D_MODEL + i * kvw + kv * HEAD_DIM
            perm[dst:dst + HEAD_DIM] = np.arange(src, src + HEAD_DIM)
    return perm


def _permute_cols(w, perm):
    parts, start = [], 0
    for j in range(1, len(perm) + 1):
        if j == len(perm) or (perm[j] != perm[j - 1] + 1 if perm[j - 1] >= 0 else perm[j] >= 0):
            if perm[start] < 0:
                parts.append(jnp.zeros(w.shape[:-1] + (j - start,), w.dtype))
            else:
                parts.append(w[..., int(perm[start]):int(perm[start]) + j - start])
            start = j
    return jnp.concatenate(parts, axis=-1)


def _rows_view(z, c0, n, t):
    return z[..., c0:c0 + KV_ROW].reshape(n, t, N_KV, 2, HEAD_DIM)


def _rows_from_t(rows_t):
    n, _, t = rows_t.shape
    return rows_t.reshape(n, N_KV, 2, HEAD_DIM, t).transpose(0, 4, 1, 2, 3)


def _trunk(x, ada, past, p, prompt):
    g, t, _ = x.shape
    tm = min(512, t)
    if prompt:
        pos = jnp.arange(t, dtype=jnp.int32)
        n_seq, t_seq = g, t
    else:
        n_seq, t_seq = t // DEC_SEQ, DEC_SEQ
        pos = jnp.tile(PAST_LEN + jnp.arange(DEC_SEQ, dtype=jnp.int32), n_seq)
    tab = _rope_tables(pos)[None]
    cmp_out, slc_out, nwin_out, swin_out = [], [], [], []
    for l in range(DEPTH):
        sh1, sc1, g1, sh2, sc2, g2 = [ada[l][..., k * D_MODEL:(k + 1) * D_MODEL] for k in range(6)]
        i = l // 2
        if l % 2 == 0:
            z, q_rot, *rows_t = _proj(x, sc1, sh1, p['nsa_w_in'][i], p['zero_a'], tab, tm, (A_CMP0,),
                                      (A_SLC0, A_WIN0), True, prompt)
            if prompt:
                cmp_rows, slc_rows, win_rows = [_rows_from_t(r) for r in rows_t]
            else:
                cmp_rows, slc_rows, win_rows = [_rows_view(z, c0, n_seq, t_seq) for c0 in (A_CMP0, A_SLC0, A_WIN0)]
            w1r, pb, w2 = p['cmp_w1r'][i], p['cmp_pb'][i], p['cmp_w2'][i]
            if prompt:
                nch = t // CMP_STRIDE
                chunks = z[..., A_CMP0:A_CMP0 + KV_ROW].reshape(g, nch, CMP_STRIDE, 2 * N_KV, HEAD_DIM)
                chunks = chunks.transpose(0, 3, 1, 2, 4).reshape(g, 2 * N_KV, nch, CHUNK_W)
                cblocks = _compress_prompt(chunks, w1r, pb, w2)
                o = _nsa_prompt_attend(z, q_rot, cblocks)
                nwin_out.append(win_rows[:, -min(A_WINDOW, t):])
            else:
                cache_cmp, cache_slc, nwin_t, _, page_table, cache_nwin, _ = past
                zs = z.reshape(n_seq, DEC_SEQ, A_COLS)
                qs = zs[..., :D_MODEL]
                o_cmp, sel = _nsa_sample_cmp(i, page_table, cache_cmp, zs[..., A_CMP0:A_CMP0 + KV_ROW], qs, w1r, pb, w2)
                o = _nsa_sample_slc(i, page_table, cache_slc, zs, q_rot.reshape(n_seq, DEC_SEQ, D_MODEL), sel,
                                    nwin_t, o_cmp).reshape(g, t, D_MODEL)
                nwin_out.append(jnp.concatenate([cache_nwin[i, :, DEC_SEQ:], win_rows], axis=1))
            cmp_out.append(cmp_rows)
            slc_out.append(slc_rows)
            w_o, b_o = p['nsa_w_o'][i], p['zero_d']
        else:
            z, *rows_t = _proj(x, sc1, sh1, p['swa_w_in'][i], p['swa_b_in'][i], tab, tm, (), (D_MODEL,), False, prompt)
            rows = _rows_from_t(rows_t[0]) if prompt else _rows_view(z, D_MODEL, n_seq, t_seq)
            if prompt:
                o = _swa_prompt_attend(z, p['swa_sinks'][i])
                swin_out.append(rows[:, -min(B_WINDOW, t):])
            else:
                o = _swa_sample_attend(i, z.reshape(n_seq, DEC_SEQ, B_COLS), past[3], p['swa_sinks'][i]).reshape(g, t, D_MODEL)
                swin_out.append(jnp.concatenate([past[6][i, :, DEC_SEQ:], rows], axis=1))
            w_o, b_o = p['swa_w_o'][i], p['swa_b_o'][i]
        x = _outproj_ln(o, x, g1, w_o, b_o, p['ln_g'][l, 0], p['ln_b'][l, 0], tm)
        x = _moe_ln(x, sc2, sh2, g2, p['router_wt'], p['router_b'], p['moe_w_gate'][l], p['moe_w_up'][l],
                    p['moe_w_down'][l], p['ln_g'][l, 1], p['ln_b'][l, 1], min(MOE_TILE, t))
    return x, (jnp.stack(cmp_out), jnp.stack(slc_out), jnp.stack(nwin_out), jnp.stack(swin_out))


def kernel(x_prompt, x_sample, cache_nsa_cmp, cache_nsa_slc, cache_nsa_win, cache_swa_win, page_table, c_prompt, c_sample, nsa_w_in, nsa_cmp_pe, nsa_cmp_w1, nsa_cmp_b1, nsa_cmp_w2, nsa_w_o, swa_w_in, swa_b_in, swa_sinks, swa_w_o, swa_b_o, ada_w, ada_b, ln_g, ln_b, router_w, router_b, moe_w_gate, moe_w_up, moe_w_down):
    n_a = nsa_w_in.shape[0]
    batch, seq, _ = x_prompt.shape
    n_dec = x_sample.shape[0]
    n_pool = cache_nsa_cmp.shape[1]

    pe_term = jnp.stack([_pe_term(nsa_cmp_pe[i], nsa_cmp_w1[i]) for i in range(n_a)])
    w1r = nsa_cmp_w1.reshape(n_a, 2, CMP_SHIFTS, CHUNK_W, CMP_HIDDEN).transpose(0, 1, 3, 2, 4)
    p = {
        'nsa_w_in': _permute_cols(nsa_w_in, _nsa_col_perm()).astype(BF16),
        'zero_a': jnp.zeros((1, A_COLS), F32),
        'zero_d': jnp.zeros((1, D_MODEL), F32),
        'cmp_w1r': w1r.reshape(n_a, 2, CHUNK_W, CMP_SHIFTS * CMP_HIDDEN).astype(BF16),
        'cmp_pb': (pe_term + nsa_cmp_b1).reshape(n_a, 2, 1, CMP_HIDDEN),
        'cmp_w2': nsa_cmp_w2.astype(BF16),
        'nsa_w_o': nsa_w_o.astype(BF16),
        'swa_w_in': _permute_cols(swa_w_in, _swa_col_perm()).astype(BF16),
        'swa_b_in': _permute_cols(swa_b_in, _swa_col_perm())[:, None, :],
        'swa_sinks': swa_sinks,
        'swa_w_o': swa_w_o.astype(BF16),
        'swa_b_o': swa_b_o[:, None, :],
        'ln_g': ln_g[:, :, None, :],
        'ln_b': ln_b[:, :, None, :],
        'router_wt': router_w.T,
        'router_b': router_b[:, None],
        'moe_w_gate': moe_w_gate.astype(BF16),
        'moe_w_up': moe_w_up.astype(BF16),
        'moe_w_down': moe_w_down.astype(BF16),
    }

    ada = _ada(jnp.concatenate([c_prompt, c_sample], axis=0), ada_w, ada_b)
    ada_p = ada[:, :batch, None, :]
    ada_s = jnp.broadcast_to(ada[:, batch:, None, :], (DEPTH, n_dec, DEC_SEQ, 6 * D_MODEL))
    ada_s = ada_s.reshape(DEPTH, 1, n_dec * DEC_SEQ, 6 * D_MODEL)

    y_p, (cmp_p, slc_p, nwin_p, swin_p) = _trunk(x_prompt, ada_p, None, p, True)
    def feature_major(c):
        return c.transpose(0, 1, 3, 4, 5, 2).reshape(c.shape[0], c.shape[1], KV_ROW, c.shape[2])
    past = (feature_major(cache_nsa_cmp), feature_major(cache_nsa_slc), feature_major(cache_nsa_win),
            feature_major(cache_swa_win), page_table, cache_nsa_win, cache_swa_win)
    y_s, (cmp_s, slc_s, nwin_s, swin_s) = _trunk(x_sample.reshape(1, n_dec * DEC_SEQ, D_MODEL), ada_s, past, p, False)
    return (y_p, y_s.reshape(n_dec, DEC_SEQ, D_MODEL), cmp_p, slc_p, nwin_p, swin_p, cmp_s, slc_s, nwin_s, swin_s)
```

```python
import functools
import math

import numpy as np
import jax
import jax.numpy as jnp
from jax import lax
from jax.experimental import pallas as pl
from jax.experimental.pallas import tpu as pltpu

D_MODEL = 1024
SEQ = 2048
DEPTH = 4
DEC_SEQ = 8
PAST_LEN = 8192
PAGE_SIZE = 128
N_PAGES = PAST_LEN // PAGE_SIZE

HEAD_DIM = 64
ROT_DIM = HEAD_DIM // 4
ROT_HALF = ROT_DIM // 2
ROPE_THETA = 500000.0
ATTN_SCALE = HEAD_DIM ** -0.5

N_HEADS = D_MODEL // HEAD_DIM
N_KV = 2
N_G = N_HEADS // N_KV
KV_ROW = N_KV * 2 * HEAD_DIM
CMP_BLOCK = 32
CMP_STRIDE = 16
CMP_HIDDEN = 4 * HEAD_DIM
CMP_SHIFTS = CMP_BLOCK // CMP_STRIDE
CHUNK_W = CMP_STRIDE * HEAD_DIM
SEL_BLOCK = 64
SEL_TOP = 16
CHUNKS_PER_SEL = SEL_BLOCK // CMP_STRIDE
A_WINDOW = 512
B_WINDOW = 128
FORCE_SCORE = 1e9

N_EXPERTS = 16
N_GROUPS = 4
EXPERTS_PER_GROUP = N_EXPERTS // N_GROUPS
D_FF_EXPERT = D_MODEL // 2

DN_ALPHA = (2.0 * DEPTH) ** 0.25
LN_EPS = 1e-5
NEG_INF = -1e30
TINY = 1e-30
LOG2E = math.log2(math.e)

LANES = 128
SUBLANES = 8
MOE_TILE = 1024
MOE_SUB_TILE = 256
SWA_SEQS_PER_STEP = 8
A_COLS = 2048
A_CMP0, A_SLC0, A_WIN0, A_GATE0 = 1024, 1280, 1536, 1792
B_COLS = D_MODEL + KV_ROW
VMEM_LIMIT = 56 * 1024 * 1024

F32 = jnp.float32
BF16 = jnp.bfloat16


def _cparams(sem):
    return pltpu.CompilerParams(dimension_semantics=sem, vmem_limit_bytes=VMEM_LIMIT)


def _dot(a, b):
    return jnp.dot(a, b, preferred_element_type=F32)


def _dot_nt(a, b, precision=None):
    return lax.dot_general(a, b, (((1,), (1,)), ((), ())), precision=precision,
                           preferred_element_type=F32)


def _masked_probs(s, mask):
    s = jnp.where(mask, s, NEG_INF)
    m = jnp.max(s, axis=-1, keepdims=True)
    e = jnp.where(mask, jnp.exp(s - m), 0.0)
    den = jnp.sum(e, axis=-1, keepdims=True)
    return e / jnp.maximum(den, TINY)


def _layer_norm(r, g, b):
    mu = jnp.mean(r, axis=-1, keepdims=True)
    rc = r - mu
    var = jnp.mean(rc * rc, axis=-1, keepdims=True)
    return rc * lax.rsqrt(var + LN_EPS) * g + b


def _ada_kernel(c_ref, w_ref, b_ref, o_ref):
    c = c_ref[...]
    a = (c * jax.nn.sigmoid(c)).astype(BF16)
    o_ref[0] = _dot(a, w_ref[0].astype(BF16)) + b_ref[0]


def _ada(c_all, ada_w, ada_b):
    m = c_all.shape[0]
    n = ada_w.shape[-1]
    tn = 1536
    return pl.pallas_call(
        _ada_kernel,
        grid=(DEPTH, n // tn),
        in_specs=[pl.BlockSpec((m, D_MODEL), lambda l, j: (0, 0)),
                  pl.BlockSpec((1, D_MODEL, tn), lambda l, j: (l, 0, j)),
                  pl.BlockSpec((1, 1, tn), lambda l, j: (l, 0, j))],
        out_specs=pl.BlockSpec((1, m, tn), lambda l, j: (l, 0, j)),
        out_shape=jax.ShapeDtypeStruct((DEPTH, m, n), F32),
        compiler_params=_cparams(("arbitrary", "arbitrary")),
        name="ada",
    )(c_all, ada_w, ada_b.reshape(DEPTH, 1, n))


def _pe_term_kernel(pe_ref, w_ref, o_ref):
    o_ref[0] = _dot(pe_ref[0].astype(BF16), w_ref[0].astype(BF16))


def _pe_term(pe, w1):
    k = CMP_BLOCK * HEAD_DIM
    pe8 = jnp.broadcast_to(pe.reshape(2, 1, k), (2, 8, k))
    out = pl.pallas_call(
        _pe_term_kernel,
        grid=(2,),
        in_specs=[pl.BlockSpec((1, 8, k), lambda i: (i, 0, 0)),
                  pl.BlockSpec((1, k, CMP_HIDDEN), lambda i: (i, 0, 0))],
        out_specs=pl.BlockSpec((1, 8, CMP_HIDDEN), lambda i: (i, 0, 0)),
        out_shape=jax.ShapeDtypeStruct((2, 8, CMP_HIDDEN), F32),
        compiler_params=_cparams(("arbitrary",)),
        name="pe_term",
    )(pe8, w1)
    return out[:, 0]


def _rope_tables(pos):
    inv = jnp.exp(-math.log(ROPE_THETA) * jnp.arange(ROT_HALF, dtype=F32) / ROT_HALF)
    ang = pos.astype(F32)[:, None] * inv[None, :]
    cos, sin = jnp.cos(ang), jnp.sin(ang)
    t = pos.shape[0]
    one = jnp.ones((t, HEAD_DIM - ROT_DIM), F32)
    zero = jnp.zeros((t, HEAD_DIM - ROT_DIM), F32)
    zh = jnp.zeros((t, ROT_HALF), F32)
    cos_h = jnp.concatenate([cos, cos, one], axis=1)
    sa_h = jnp.concatenate([-sin, zh, zero], axis=1)
    sb_h = jnp.concatenate([zh, sin, zero], axis=1)
    id_h = jnp.ones((t, HEAD_DIM), F32)
    z_h = jnp.zeros((t, HEAD_DIM), F32)
    return jnp.concatenate([cos_h, cos_h, sa_h, sa_h, sb_h, sb_h,
                            cos_h, id_h, sa_h, z_h, sb_h, z_h], axis=1)


def _rope_apply(seg, cos, sa, sb):
    w = seg.shape[1]
    reps = w // LANES
    c = jnp.tile(cos, (1, reps))
    a = jnp.tile(sa, (1, reps))
    b = jnp.tile(sb, (1, reps))
    return seg * c + pltpu.roll(seg, w - ROT_HALF, 1) * a + pltpu.roll(seg, ROT_HALF, 1) * b


def _proj_kernel(x_ref, sc_ref, sh_ref, w_ref, b_ref, tab_ref, z_ref, *more_refs, plain_segs, kv_segs, sep_qrot,
                 rows_t):
    more_refs = list(more_refs)
    x = x_ref[0]
    h = x * (1.0 + sc_ref[0]) + sh_ref[0]
    z = _dot(h.astype(BF16), w_ref[...]) + b_ref[...]
    tab = tab_ref[0]
    tq = [tab[:, i * LANES:(i + 1) * LANES] for i in range(3)]
    tk = [tab[:, i * LANES:(i + 1) * LANES] for i in range(3, 6)]
    q_rot = _rope_apply(z[:, :D_MODEL], *tq)
    z_ref[0] = z
    if sep_qrot:
        more_refs.pop(0)[0] = q_rot
    else:
        z_ref[0, :, :D_MODEL] = q_rot
    for c0 in plain_segs:
        if rows_t:
            more_refs.pop(0)[0] = z[:, c0:c0 + KV_ROW].T
    for c0 in kv_segs:
        seg = _rope_apply(z[:, c0:c0 + KV_ROW], *tk)
        z_ref[0, :, c0:c0 + KV_ROW] = seg
        if rows_t:
            more_refs.pop(0)[0] = seg.T


def _proj(x, sc, sh, w, b, tab, tm, plain_segs, kv_segs, sep_qrot, rows_t):
    g, t, _ = x.shape
    n = w.shape[1]
    tmod = tm if sc.shape[1] == t else 1
    out_shape = [jax.ShapeDtypeStruct((g, t, n), F32)]
    out_specs = [pl.BlockSpec((1, tm, n), lambda a, i: (a, i, 0))]
    if sep_qrot:
        out_shape.append(jax.ShapeDtypeStruct((g, t, D_MODEL), F32))
        out_specs.append(pl.BlockSpec((1, tm, D_MODEL), lambda a, i: (a, i, 0)))
    if rows_t:
        for _ in plain_segs + kv_segs:
            out_shape.append(jax.ShapeDtypeStruct((g, KV_ROW, t), F32))
            out_specs.append(pl.BlockSpec((1, KV_ROW, tm), lambda a, i: (a, 0, i)))
    mod_spec = pl.BlockSpec((1, tmod, D_MODEL), (lambda a, i: (a, i, 0)) if tmod == tm else (lambda a, i: (a, 0, 0)))
    return pl.pallas_call(
        functools.partial(_proj_kernel, plain_segs=plain_segs, kv_segs=kv_segs, sep_qrot=sep_qrot, rows_t=rows_t),
        grid=(g, t // tm),
        in_specs=[pl.BlockSpec((1, tm, D_MODEL), lambda a, i: (a, i, 0)),
                  mod_spec, mod_spec,
                  pl.BlockSpec((D_MODEL, n), lambda a, i: (0, 0)),
                  pl.BlockSpec((1, n), lambda a, i: (0, 0)),
                  pl.BlockSpec((1, tm, 6 * LANES), lambda a, i: (0, i, 0))],
        out_specs=out_specs,
        out_shape=out_shape,
        compiler_params=_cparams(("arbitrary", "arbitrary")),
        name="proj",
    )(x, sc, sh, w, b, tab)


def _outproj_kernel(o_ref, x_ref, g_ref, w_ref, b_ref, lg_ref, lb_ref, out_ref):
    y = _dot(o_ref[0].astype(BF16), w_ref[...]) + b_ref[...]
    r = DN_ALPHA * x_ref[0] + g_ref[0] * y
    out_ref[0] = _layer_norm(r, lg_ref[...], lb_ref[...])


def _outproj_ln(o, x, gate, w, b, lg, lb, tm):
    g, t, _ = x.shape
    tmod = tm if gate.shape[1] == t else 1
    row = pl.BlockSpec((1, tm, D_MODEL), lambda a, i: (a, i, 0))
    vec = pl.BlockSpec((1, D_MODEL), lambda a, i: (0, 0))
    mod_spec = pl.BlockSpec((1, tmod, D_MODEL), (lambda a, i: (a, i, 0)) if tmod == tm else (lambda a, i: (a, 0, 0)))
    return pl.pallas_call(
        _outproj_kernel,
        grid=(g, t // tm),
        in_specs=[row, row, mod_spec,
                  pl.BlockSpec((D_MODEL, D_MODEL), lambda a, i: (0, 0)), vec, vec, vec],
        out_specs=row,
        out_shape=jax.ShapeDtypeStruct((g, t, D_MODEL), F32),
        compiler_params=_cparams(("arbitrary", "arbitrary")),
        name="outproj_ln",
    )(o, x, gate, w, b, lg, lb)


def _router_gates_t(h, rwt_ref, rb_ref):
    logits = _dot_nt(rwt_ref[...], h, precision=lax.Precision.HIGHEST)
    m = jnp.max(logits, axis=0, keepdims=True)
    ex = jnp.exp(logits - m)
    probs = ex / jnp.sum(ex, axis=0, keepdims=True)
    biased = probs + rb_ref[...]
    eidx = lax.broadcasted_iota(jnp.int32, biased.shape, 0)
    best = None
    g_sel = None
    for g in range(N_GROUPS):
        rows = [biased[g * EXPERTS_PER_GROUP + i:g * EXPERTS_PER_GROUP + i + 1] for i in range(EXPERTS_PER_GROUP)]
        gs = None
        for i in range(EXPERTS_PER_GROUP):
            for j in range(i + 1, EXPERTS_PER_GROUP):
                pair = rows[i] + rows[j]
                gs = pair if gs is None else jnp.maximum(gs, pair)
        if best is None:
            best, g_sel = gs, jnp.zeros(gs.shape, jnp.int32)
        else:
            upd = gs > best
            g_sel = jnp.where(upd, g, g_sel)
            best = jnp.maximum(best, gs)
    v = jnp.where(eidx // EXPERTS_PER_GROUP == g_sel, biased, NEG_INF)
    m1 = jnp.max(v, axis=0, keepdims=True)
    i1 = jnp.min(jnp.where(v == m1, eidx, N_EXPERTS), axis=0, keepdims=True)
    v2 = jnp.where(eidx == i1, -jnp.inf, v)
    m2 = jnp.max(v2, axis=0, keepdims=True)
    i2 = jnp.min(jnp.where(v2 == m2, eidx, N_EXPERTS), axis=0, keepdims=True)
    w1 = jnp.sum(jnp.where(eidx == i1, probs, 0.0), axis=0, keepdims=True)
    w2 = jnp.sum(jnp.where(eidx == i2, probs, 0.0), axis=0, keepdims=True)
    tot = w1 + w2
    return jnp.where(eidx == i1, w1 / tot, 0.0) + jnp.where(eidx == i2, w2 / tot, 0.0), g_sel


def _moe_kernel(x_ref, sc_ref, sh_ref, g_ref, rwt_ref, rb_ref, wg_ref, wu_ref, wd_ref, lg_ref, lb_ref,
                out_ref, hs_scr, gate_scr, ys_scr, unperm_scr, seg_smem, *, sub):
    e = pl.program_id(2)
    tm = hs_scr.shape[0]

    @pl.when(e == 0)
    def _():
        h = x_ref[0] * (1.0 + sc_ref[0]) + sh_ref[0]
        gates_t, g_sel = _router_gates_t(h, rwt_ref, rb_ref)
        grp = lax.broadcasted_iota(jnp.int32, (SUBLANES, tm), 0)
        onehot = jnp.where(grp == g_sel, 1.0, 0.0)
        src = lax.broadcasted_iota(jnp.int32, (tm, tm), 0)
        dst = lax.broadcasted_iota(jnp.int32, (tm, tm), 1)
        earlier = jnp.where(src < dst, 1.0, 0.0).astype(BF16)
        cum = _dot(onehot.astype(BF16), earlier)
        pos = jnp.zeros((1, tm), F32)
        off = jnp.int32(0)
        for r in range(N_GROUPS):
            cnt = jnp.sum(onehot[r:r + 1, :]).astype(jnp.int32)
            seg_smem[r] = off
            seg_smem[N_GROUPS + r] = cnt
            pos = pos + onehot[r:r + 1, :] * (cum[r:r + 1, :] + off.astype(F32))
            off = off + cnt
        perm = jnp.where(src == pos.astype(jnp.int32), 1.0, 0.0)
        hs_scr[...] = _dot(perm.astype(BF16), h.astype(BF16)).astype(BF16)
        gs_t = _dot_nt(gates_t, perm, precision=lax.Precision.HIGHEST)
        pad = jnp.zeros((LANES - N_EXPERTS, tm), F32)
        gate_scr[...] = jnp.concatenate([gs_t, pad], axis=0).T
        unperm_scr[...] = perm.T.astype(BF16)
        ys_scr[...] = jnp.zeros_like(ys_scr)

    grp_e = e // EXPERTS_PER_GROUP
    off = seg_smem[grp_e]
    cnt = seg_smem[N_GROUPS + grp_e]
    first = off // sub
    last = (off + cnt - 1) // sub
    for j in range(tm // sub):
        @pl.when((cnt > 0) & (first <= j) & (j <= last))
        def _():
            rows = hs_scr[j * sub:(j + 1) * sub, :]
            a = _dot(rows, wg_ref[0])
            u = _dot(rows, wu_ref[0])
            gates = gate_scr[j * sub:(j + 1) * sub, :]
            lane = lax.broadcasted_iota(jnp.int32, gates.shape, 1)
            gcol = jnp.sum(jnp.where(lane == e, gates, 0.0), axis=1, keepdims=True)
            act = (a * jax.nn.sigmoid(a)) * u * gcol
            ys_scr[j * sub:(j + 1) * sub, :] += _dot(act.astype(BF16), wd_ref[0])

    @pl.when(e == N_EXPERTS - 1)
    def _():
        ys = ys_scr[...]
        hi = ys.astype(BF16)
        lo = (ys - hi.astype(F32)).astype(BF16)
        unperm = unperm_scr[...]
        y = _dot(unperm, hi) + _dot(unperm, lo)
        r = DN_ALPHA * x_ref[0] + g_ref[0] * y
        out_ref[0] = _layer_norm(r, lg_ref[...], lb_ref[...])


def _moe_ln(x, sc, sh, gate, rwt, rb, wg, wu, wd, lg, lb, tm):
    g, t, _ = x.shape
    sub = min(MOE_SUB_TILE, tm)
    tmod = tm if sc.shape[1] == t else 1
    row = pl.BlockSpec((1, tm, D_MODEL), lambda a, i, e: (a, i, 0))
    vec = pl.BlockSpec((1, D_MODEL), lambda a, i, e: (0, 0))
    mod_spec = pl.BlockSpec((1, tmod, D_MODEL),
                            (lambda a, i, e: (a, i, 0)) if tmod == tm else (lambda a, i, e: (a, 0, 0)))
    return pl.pallas_call(
        functools.partial(_moe_kernel, sub=sub),
        grid=(g, t // tm, N_EXPERTS),
        in_specs=[row, mod_spec, mod_spec, mod_spec,
                  pl.BlockSpec((N_EXPERTS, D_MODEL), lambda a, i, e: (0, 0)),
                  pl.BlockSpec((N_EXPERTS, 1), lambda a, i, e: (0, 0)),
                  pl.BlockSpec((1, D_MODEL, D_FF_EXPERT), lambda a, i, e: (e, 0, 0)),
                  pl.BlockSpec((1, D_MODEL, D_FF_EXPERT), lambda a, i, e: (e, 0, 0)),
                  pl.BlockSpec((1, D_FF_EXPERT, D_MODEL), lambda a, i, e: (e, 0, 0)),
                  vec, vec],
        out_specs=row,
        out_shape=jax.ShapeDtypeStruct((g, t, D_MODEL), F32),
        scratch_shapes=[pltpu.VMEM((tm, D_MODEL), BF16),
                        pltpu.VMEM((tm, LANES), F32),
                        pltpu.VMEM((tm, D_MODEL), F32),
                        pltpu.VMEM((tm, tm), BF16),
                        pltpu.SMEM((2 * N_GROUPS,), jnp.int32)],
        compiler_params=_cparams(("arbitrary", "arbitrary", "arbitrary")),
        name="moe_ln",
    )(x, sc, sh, gate, rwt, rb, wg, wu, wd, lg, lb)


def _compress_blocks(chunks, w1r, pb, w2):
    nch = chunks.shape[0]
    part = _dot(chunks, w1r)
    pre = part[:, :CMP_HIDDEN] + pltpu.roll(part[:, CMP_HIDDEN:], nch - 1, 0)
    hid = jax.nn.gelu(pre + pb)
    return _dot(hid.astype(BF16), w2)


def _compress_kernel(ch_ref, w1r_ref, pb_ref, w2_ref, o_ref):
    for j in range(2 * N_KV):
        i = j % 2
        o_ref[0, j] = _compress_blocks(ch_ref[0, j].astype(BF16), w1r_ref[i], pb_ref[i], w2_ref[i])


def _compress_prompt(chunks, w1r, pb, w2):
    b, _, nch, _ = chunks.shape
    return pl.pallas_call(
        _compress_kernel,
        grid=(b,),
        in_specs=[pl.BlockSpec((1, 2 * N_KV, nch, CHUNK_W), lambda a: (a, 0, 0, 0)),
                  pl.BlockSpec((2, CHUNK_W, 2 * CMP_HIDDEN), lambda a: (0, 0, 0)),
                  pl.BlockSpec((2, 1, CMP_HIDDEN), lambda a: (0, 0, 0)),
                  pl.BlockSpec((2, CMP_HIDDEN, HEAD_DIM), lambda a: (0, 0, 0))],
        out_specs=pl.BlockSpec((1, 2 * N_KV, nch, HEAD_DIM), lambda a: (a, 0, 0, 0)),
        out_shape=jax.ShapeDtypeStruct((b, 2 * N_KV, nch, HEAD_DIM), F32),
        compiler_params=_cparams(("arbitrary",)),
        name="compress_prompt",
    )(chunks, w1r, pb, w2)


def _stack_heads(q):
    return jnp.concatenate([q[:, g * HEAD_DIM:(g + 1) * HEAD_DIM] for g in range(N_G)], axis=0)


def _chunk_score_matrix_t(n_sel, n_cmp):
    j = lax.broadcasted_iota(jnp.int32, (n_sel, n_cmp), 0)
    c = lax.broadcasted_iota(jnp.int32, (n_sel, n_cmp), 1)
    lo = j * CHUNKS_PER_SEL
    hi = lo + CHUNKS_PER_SEL - 1
    m = jnp.zeros((n_sel, n_cmp), F32)
    for r in range(CMP_SHIFTS):
        m = m + jnp.where((c + r >= lo) & (c + r <= hi), 1.0, 0.0)
    return m


def _nsa_prompt_kernel(q_ref, qr_ref, cb_ref, slc_ref, win_ref, gate_ref, o_ref, m_scr, acc_scr, *, tq, kc):
    qi = pl.program_id(2)
    t0 = qi * tq
    n_cmp = cb_ref.shape[2]
    n_sel = SEQ // SEL_BLOCK
    tpos = t0 + lax.broadcasted_iota(jnp.int32, (tq, 1), 0)

    qs = _stack_heads(q_ref[0]).astype(BF16)
    k_c = cb_ref[0, 0].astype(BF16)
    v_c = cb_ref[0, 1].astype(BF16)
    s = (_dot_nt(qs, k_c) * ATTN_SCALE).reshape(N_G, tq, n_cmp)
    cend = lax.broadcasted_iota(jnp.int32, (tq, n_cmp), 1) * CMP_STRIDE + CMP_BLOCK - 1
    p = _masked_probs(s, (cend <= tpos)[None])
    o_cmp = _dot(p.reshape(N_G * tq, n_cmp).astype(BF16), v_c)
    imp = jnp.sum(p, axis=0)

    score_t = _dot_nt(_chunk_score_matrix_t(n_sel, n_cmp), imp, precision=lax.Precision.HIGHEST)
    jj = lax.broadcasted_iota(jnp.int32, (n_sel, tq), 0)
    tt = t0 + lax.broadcasted_iota(jnp.int32, (n_sel, tq), 1)
    cur = tt // SEL_BLOCK
    forced = (jj == 0) | (jj == cur) | (jj == cur - 1)
    sc = jnp.where(jj * SEL_BLOCK <= tt, jnp.where(forced, FORCE_SCORE, score_t), NEG_INF)
    rank = jnp.zeros((n_sel, tq), F32)
    for i in range(n_sel):
        ri = sc[i:i + 1]
        rank = rank + jnp.where(ri > sc, 1.0, jnp.where((ri == sc) & (jj > i), 1.0, 0.0))
    sel_t = jnp.where(rank < SEL_TOP, 1.0, 0.0)
    sel = jnp.concatenate([sel_t, jnp.zeros((LANES - n_sel, tq), F32)], axis=0).T.astype(BF16)

    qrs = (_stack_heads(qr_ref[0]) * (ATTN_SCALE * LOG2E)).astype(BF16)

    m_scr[...] = jnp.full(m_scr.shape, NEG_INF, F32)
    acc_scr[...] = jnp.zeros(acc_scr.shape, F32)

    def body(c, carry):
        k0 = pl.multiple_of(c * kc, kc)
        rows = slc_ref[0, pl.ds(k0, kc), :]
        k = rows[:, :HEAD_DIM].astype(BF16)
        sk = _dot_nt(qrs, k).reshape(N_G, tq, kc)
        kpos = k0 + lax.broadcasted_iota(jnp.int32, (LANES, kc), 1)
        expand = jnp.where(kpos // SEL_BLOCK == lax.broadcasted_iota(jnp.int32, (LANES, kc), 0), 1.0, 0.0)
        chosen = _dot(sel, expand.astype(BF16))
        sm = sk + jnp.where((chosen > 0.5) & (kpos[:1] <= tpos), 0.0, NEG_INF)[None]
        m = m_scr[...]
        m_new = jnp.maximum(m, jnp.max(sm, axis=-1, keepdims=True))
        e = jnp.exp2(sm - m_new)
        pv = _dot(e.reshape(N_G * tq, kc).astype(BF16), _values_and_ones(rows)).reshape(N_G, tq, LANES)
        acc_scr[...] = jnp.exp2(m - m_new) * acc_scr[...] + pv
        m_scr[...] = m_new
        return carry

    lax.fori_loop(0, (t0 + tq + kc - 1) // kc, body, 0)
    acc = acc_scr[...]
    o_slc = acc[:, :, :HEAD_DIM] / jnp.maximum(acc[:, :, HEAD_DIM:HEAD_DIM + 1], TINY)

    o_win = _band_attend(qrs, win_ref, t0, tq, A_WINDOW, None)

    gt = jax.nn.sigmoid(gate_ref[0])
    o_cmp = o_cmp.reshape(N_G, tq, HEAD_DIM)
    for g in range(N_G):
        og = (gt[:, g:g + 1] * o_cmp[g] + gt[:, N_G + g:N_G + g + 1] * o_slc[g]
              + gt[:, 2 * N_G + g:2 * N_G + g + 1] * o_win[g])
        o_ref[0, :, g * HEAD_DIM:(g + 1) * HEAD_DIM] = og


def _band_attend(qrs, rows_ref, t0, tq, window, sinks):
    span = window + tq
    start = pl.multiple_of(jnp.maximum(t0 - window, 0), tq)
    rows = rows_ref[0, pl.ds(start, span), :]
    k = rows[:, :HEAD_DIM].astype(BF16)
    d = (t0 + lax.broadcasted_iota(jnp.int32, (tq, span), 0)) - (start + lax.broadcasted_iota(jnp.int32, (tq, span), 1))
    s = _dot_nt(qrs, k).reshape(N_G, tq, span) + jnp.where((d >= 0) & (d <= window), 0.0, NEG_INF)[None]
    m = jnp.max(s, axis=-1, keepdims=True)
    if sinks is not None:
        sink = jnp.concatenate([jnp.full((1, tq, 1), sk, F32) for sk in sinks], axis=0)
        m = jnp.maximum(m, sink)
    e = jnp.exp2(s - m)
    den = jnp.sum(e, axis=-1, keepdims=True)
    if sinks is not None:
        den = den + jnp.exp2(sink - m)
    o = _dot(e.reshape(N_G * tq, span).astype(BF16), rows[:, HEAD_DIM:].astype(BF16)).reshape(N_G, tq, HEAD_DIM)
    return o / jnp.maximum(den, TINY)


def _values_and_ones(rows):
    lane = lax.broadcasted_iota(jnp.int32, rows.shape, 1)
    return jnp.where(lane < HEAD_DIM, pltpu.roll(rows, HEAD_DIM, 1), 1.0).astype(BF16)


def _nsa_prompt_attend(z, q_rot, cblocks, tq=128, kc=512):
    b, t, _ = z.shape
    n_cmp = cblocks.shape[2]
    hw = N_G * HEAD_DIM
    blk = lambda base: pl.BlockSpec((1, t, LANES), lambda a, k, i: (a, 0, base // LANES + k))
    return pl.pallas_call(
        functools.partial(_nsa_prompt_kernel, tq=tq, kc=kc),
        grid=(b, N_KV, t // tq),
        in_specs=[pl.BlockSpec((1, tq, hw), lambda a, k, i: (a, i, k)),
                  pl.BlockSpec((1, tq, hw), lambda a, k, i: (a, i, k)),
                  pl.BlockSpec((1, 2, n_cmp, HEAD_DIM), lambda a, k, i: (a, k, 0, 0)),
                  blk(A_SLC0), blk(A_WIN0),
                  pl.BlockSpec((1, tq, LANES), lambda a, k, i: (a, i, A_GATE0 // LANES + k))],
        out_specs=pl.BlockSpec((1, tq, hw), lambda a, k, i: (a, i, k)),
        out_shape=jax.ShapeDtypeStruct((b, t, D_MODEL), F32),
        scratch_shapes=[pltpu.VMEM((N_G, tq, 1), F32), pltpu.VMEM((N_G, tq, LANES), F32)],
        compiler_params=_cparams(("arbitrary", "arbitrary", "arbitrary")),
        name="nsa_prompt_attend",
    )(z, q_rot, cblocks, z, z, z)


def _swa_prompt_kernel(sink_ref, q_ref, rows_ref, o_ref, *, tq):
    kv = pl.program_id(1)
    t0 = pl.program_id(2) * tq
    qrs = (_stack_heads(q_ref[0]) * (ATTN_SCALE * LOG2E)).astype(BF16)
    sinks = [sink_ref[kv, g] * LOG2E for g in range(N_G)]
    o = _band_attend(qrs, rows_ref, t0, tq, B_WINDOW, sinks)
    for g in range(N_G):
        o_ref[0, :, g * HEAD_DIM:(g + 1) * HEAD_DIM] = o[g]


def _swa_prompt_attend(z, sinks, tq=128):
    b, t, _ = z.shape
    hw = N_G * HEAD_DIM
    return pl.pallas_call(
        functools.partial(_swa_prompt_kernel, tq=tq),
        grid=(b, N_KV, t // tq),
        in_specs=[pl.BlockSpec(memory_space=pltpu.SMEM),
                  pl.BlockSpec((1, tq, hw), lambda a, k, i: (a, i, k)),
                  pl.BlockSpec((1, t, LANES), lambda a, k, i: (a, 0, D_MODEL // LANES + k))],
        out_specs=pl.BlockSpec((1, tq, hw), lambda a, k, i: (a, i, k)),
        out_shape=jax.ShapeDtypeStruct((b, t, D_MODEL), F32),
        compiler_params=_cparams(("arbitrary", "arbitrary", "arbitrary")),
        name="swa_prompt_attend",
    )(sinks.reshape(N_KV, N_G), z, z)


S_CHUNKS = 520
S_ROWS = S_CHUNKS * CMP_STRIDE
S_SEL = 256
N_SEL_SAMPLE = PAST_LEN // SEL_BLOCK + 1
SLABS = CMP_STRIDE // SUBLANES
PAGES_PER_ITER = 8
assert N_SEL_SAMPLE == LANES + 1 and DEC_SEQ == SUBLANES and N_PAGES % PAGES_PER_ITER == 0


def _page_copy(cache_hbm, ia, pt_ref, n, p, buf, slot, sem):
    return pltpu.make_async_copy(cache_hbm.at[ia, pt_ref[n, p]],
                                 buf.at[slot, :, pl.ds(pl.multiple_of(p * PAGE_SIZE, PAGE_SIZE), PAGE_SIZE)],
                                 sem.at[slot])


def _gather_pages(cache_hbm, ia, pt_ref, buf, sem):
    n = pl.program_id(0)
    nn = pl.num_programs(0)
    slot = n % 2

    def start(seq, sl):
        def go(p, c):
            _page_copy(cache_hbm, ia, pt_ref, seq, p, buf, sl, sem).start()
            return c
        lax.fori_loop(0, N_PAGES, go, 0)

    @pl.when(n == 0)
    def _():
        start(0, 0)

    @pl.when(n + 1 < nn)
    def _():
        start(n + 1, 1 - slot)

    def wait(p, c):
        _page_copy(cache_hbm, ia, pt_ref, n, p, buf, slot, sem).wait()
        return c
    lax.fori_loop(0, N_PAGES, wait, 0)
    return slot


def _sample_q_rows(q, kv):
    return jnp.concatenate([q[:, (kv * N_G + g) * HEAD_DIM:(kv * N_G + g + 1) * HEAD_DIM] for g in range(N_G)], axis=0)


def _sample_q_blockdiag(q):
    parts = []
    for kv in range(N_KV):
        qs = _sample_q_rows(q, kv)
        left = jnp.zeros((qs.shape[0], kv * 2 * HEAD_DIM), F32)
        right = jnp.zeros((qs.shape[0], KV_ROW - kv * 2 * HEAD_DIM - HEAD_DIM), F32)
        parts.append(jnp.concatenate([left, qs, right], axis=1) if kv else jnp.concatenate([qs, right], axis=1))
    return jnp.concatenate(parts, axis=0)


def _nsa_sample_cmp_kernel(pt_ref, new_ref, q_ref, w1r_ref, pb_ref, w2_ref, smat_ref, cache_hbm,
                           ocmp_ref, sel_ref, pages, rows, chunks, sem, *, ia):
    slot = _gather_pages(cache_hbm, ia, pt_ref, pages, sem)

    def to_rows(it, c):
        for u in range(PAGES_PER_ITER):
            p = it * PAGES_PER_ITER + u
            r0 = pl.multiple_of(p * PAGE_SIZE, PAGE_SIZE)
            c0 = pl.multiple_of(p * (PAGE_SIZE // SLABS), PAGE_SIZE // SLABS)
            for kv in range(N_KV):
                t = pages[slot, kv * LANES:(kv + 1) * LANES, pl.ds(r0, PAGE_SIZE)].T
                for v in range(PAGE_SIZE // SUBLANES):
                    rows[kv, v % SLABS, pl.ds(c0 + (v // SLABS) * SUBLANES, SUBLANES), :] = t[v * SUBLANES:(v + 1) * SUBLANES]
        return c
    lax.fori_loop(0, N_PAGES // PAGES_PER_ITER, to_rows, 0)
    new_rows = new_ref[0]
    tail = (PAST_LEN // CMP_STRIDE) * SUBLANES
    for kv in range(N_KV):
        rows[kv, 0, tail:tail + DEC_SEQ, :] = new_rows[:, kv * LANES:(kv + 1) * LANES]
        rows[kv, 0, tail + DEC_SEQ:, :] = jnp.zeros((S_CHUNKS * SUBLANES - tail - DEC_SEQ, LANES), F32)
        for slab in range(1, SLABS):
            rows[kv, slab, tail:, :] = jnp.zeros((S_CHUNKS * SUBLANES - tail, LANES), F32)

    lo = lax.broadcasted_iota(jnp.int32, (S_CHUNKS, LANES), 1) < HEAD_DIM
    for k in range(CMP_STRIDE // 2):
        for kv in range(N_KV):
            sa, sb = 2 * k, 2 * k + 1
            a = rows[kv, sa // SUBLANES, pl.ds(sa % SUBLANES, S_CHUNKS, stride=SUBLANES), :]
            b = rows[kv, sb // SUBLANES, pl.ds(sb % SUBLANES, S_CHUNKS, stride=SUBLANES), :]
            chunks[2 * kv, :, k * LANES:(k + 1) * LANES] = jnp.where(lo, a, pltpu.roll(b, HEAD_DIM, 1)).astype(BF16)
            chunks[2 * kv + 1, :, k * LANES:(k + 1) * LANES] = jnp.where(lo, pltpu.roll(a, HEAD_DIM, 1), b).astype(BF16)

    q = q_ref[0]
    qpos = PAST_LEN + lax.broadcasted_iota(jnp.int32, (DEC_SEQ, 1), 0)
    cend = lax.broadcasted_iota(jnp.int32, (DEC_SEQ, S_CHUNKS), 1) * CMP_STRIDE + CMP_BLOCK - 1
    cmask = (cend <= qpos)[None]
    imps = []
    for kv in range(N_KV):
        k_c = _compress_blocks(chunks[2 * kv], w1r_ref[0], pb_ref[0], w2_ref[0]).astype(BF16)
        v_c = _compress_blocks(chunks[2 * kv + 1], w1r_ref[1], pb_ref[1], w2_ref[1]).astype(BF16)
        qs = _sample_q_rows(q, kv).astype(BF16)
        s = (_dot_nt(qs, k_c) * ATTN_SCALE).reshape(N_G, DEC_SEQ, S_CHUNKS)
        p = _masked_probs(s, cmask)
        o = _dot(p.reshape(N_G * DEC_SEQ, S_CHUNKS).astype(BF16), v_c)
        for g in range(N_G):
            h = kv * N_G + g
            ocmp_ref[0, :, h * HEAD_DIM:(h + 1) * HEAD_DIM] = o[g * DEC_SEQ:(g + 1) * DEC_SEQ]
        imps.append(jnp.sum(p, axis=0))
    imp = jnp.concatenate(imps, axis=0)

    score = _dot_nt(imp, smat_ref[...], precision=lax.Precision.HIGHEST)
    shape = score.shape
    jj = lax.broadcasted_iota(jnp.int32, shape, 1)
    tt = PAST_LEN + lax.broadcasted_iota(jnp.int32, shape, 0) % DEC_SEQ
    cur = tt // SEL_BLOCK
    forced = (jj == 0) | (jj == cur) | (jj == cur - 1)
    sc = jnp.where(jj * SEL_BLOCK <= tt, jnp.where(forced, FORCE_SCORE, score), NEG_INF)

    sc_a = sc[:, :LANES]
    extra = jnp.sum(jnp.where(jj == LANES, sc, 0.0), axis=1, keepdims=True)
    lane = lax.broadcasted_iota(jnp.int32, sc_a.shape, 1)
    rank = jnp.where(extra > sc_a, 1.0, 0.0)
    for k in range(1, LANES):
        r = pltpu.roll(sc_a, k, 1)
        rank = rank + jnp.where(r > sc_a, 1.0, jnp.where(r == sc_a, jnp.where(lane >= k, 1.0, 0.0), 0.0))
    rank_x = jnp.sum(jnp.where(sc_a >= extra, 1.0, 0.0), axis=1, keepdims=True)
    sel_ref[0, :, :LANES] = jnp.where(rank < SEL_TOP, 1.0, 0.0)
    sel_ref[0, :, LANES:] = jnp.where((lane == 0) & (rank_x < SEL_TOP), 1.0, 0.0)


def _nsa_sample_cmp(ia, page_table, cache, new_rows, q, w1r, pb, w2):
    n = q.shape[0]
    grid_spec = pltpu.PrefetchScalarGridSpec(
        num_scalar_prefetch=1,
        grid=(n,),
        in_specs=[pl.BlockSpec((1, DEC_SEQ, KV_ROW), lambda a, pt: (a, 0, 0)),
                  pl.BlockSpec((1, DEC_SEQ, D_MODEL), lambda a, pt: (a, 0, 0)),
                  pl.BlockSpec((2, CHUNK_W, 2 * CMP_HIDDEN), lambda a, pt: (0, 0, 0)),
                  pl.BlockSpec((2, 1, CMP_HIDDEN), lambda a, pt: (0, 0, 0)),
                  pl.BlockSpec((2, CMP_HIDDEN, HEAD_DIM), lambda a, pt: (0, 0, 0)),
                  pl.BlockSpec((S_SEL, S_CHUNKS), lambda a, pt: (0, 0)),
                  pl.BlockSpec(memory_space=pl.ANY)],
        out_specs=[pl.BlockSpec((1, DEC_SEQ, D_MODEL), lambda a, pt: (a, 0, 0)),
                   pl.BlockSpec((1, N_KV * DEC_SEQ, S_SEL), lambda a, pt: (a, 0, 0))],
        scratch_shapes=[pltpu.VMEM((2, KV_ROW, PAST_LEN), F32),
                        pltpu.VMEM((N_KV, SLABS, S_CHUNKS * SUBLANES, LANES), F32),
                        pltpu.VMEM((2 * N_KV, S_CHUNKS, CHUNK_W), BF16),
                        pltpu.SemaphoreType.DMA((2,))])
    return pl.pallas_call(
        functools.partial(_nsa_sample_cmp_kernel, ia=ia),
        grid_spec=grid_spec,
        out_shape=[jax.ShapeDtypeStruct((n, DEC_SEQ, D_MODEL), F32),
                   jax.ShapeDtypeStruct((n, N_KV * DEC_SEQ, S_SEL), F32)],
        compiler_params=_cparams(("arbitrary",)),
        name="nsa_sample_cmp",
    )(page_table, new_rows, q, w1r, pb, w2, _chunk_score_matrix_t(S_SEL, S_CHUNKS), cache)


def _window_cached_attend(qbd, wt, new, window, sink):
    nq = qbd.shape[0]
    heads = nq // DEC_SEQ
    qpos = PAST_LEN + lax.broadcasted_iota(jnp.int32, (DEC_SEQ, 1), 0)
    d_p = qpos - (PAST_LEN - window + lax.broadcasted_iota(jnp.int32, (DEC_SEQ, window), 1))
    d_n = qpos - (PAST_LEN + lax.broadcasted_iota(jnp.int32, (DEC_SEQ, DEC_SEQ), 1))
    s_p = _dot(qbd, wt).reshape(heads, DEC_SEQ, window) + jnp.where((d_p >= 0) & (d_p <= window), 0.0, NEG_INF)[None]
    s_n = _dot_nt(qbd, new).reshape(heads, DEC_SEQ, DEC_SEQ) + jnp.where((d_n >= 0) & (d_n <= window), 0.0, NEG_INF)[None]
    m = jnp.maximum(jnp.max(s_p, axis=-1, keepdims=True), jnp.max(s_n, axis=-1, keepdims=True))
    if sink is not None:
        m = jnp.maximum(m, sink)
    e_p = jnp.exp(s_p - m)
    e_n = jnp.exp(s_n - m)
    den = jnp.sum(e_p, axis=-1, keepdims=True) + jnp.sum(e_n, axis=-1, keepdims=True)
    if sink is not None:
        den = den + jnp.exp(sink - m)
    o = _dot_nt(e_p.reshape(nq, window).astype(BF16), wt) + _dot(e_n.reshape(nq, DEC_SEQ).astype(BF16), new)
    return o / jnp.maximum(den.reshape(nq, 1), TINY)


def _advance_window(wt, new):
    w = wt.shape[1]
    shifted = pltpu.roll(wt, w - DEC_SEQ, 1)
    new_t = jnp.concatenate([new, jnp.zeros((LANES - DEC_SEQ, KV_ROW), F32)], axis=0).T
    new_t = pltpu.roll(new_t, LANES - DEC_SEQ, 1)
    lane = lax.broadcasted_iota(jnp.int32, (KV_ROW, LANES), 1)
    tail = jnp.where(lane >= LANES - DEC_SEQ, new_t, shifted[:, w - LANES:])
    return tail if w == LANES else jnp.concatenate([shifted[:, :w - LANES], tail], axis=1)


def _nsa_sample_slc_kernel(pt_ref, new_ref, qr_ref, sel_ref, wbuf_ref, wnew_ref, gate_ref, ocmp_ref, expand_ref,
                           cache_hbm, o_ref, wout_ref, pages, pages_bf, sem, *, ia):
    slot = _gather_pages(cache_hbm, ia, pt_ref, pages, sem)
    wout_ref[0] = _advance_window(wbuf_ref[0], wnew_ref[0])
    pages_bf[...] = pages[slot].astype(BF16)

    nq = N_KV * N_G * DEC_SEQ
    qbd = (_sample_q_blockdiag(qr_ref[0]) * ATTN_SCALE).astype(BF16)
    qpos = PAST_LEN + lax.broadcasted_iota(jnp.int32, (DEC_SEQ, 1), 0)

    past = pages_bf[...]
    new = new_ref[0].astype(BF16)
    s_p = _dot(qbd, past).reshape(N_KV, N_G, DEC_SEQ, PAST_LEN)
    s_n = _dot_nt(qbd, new).reshape(N_KV, N_G, DEC_SEQ, DEC_SEQ)
    sel = sel_ref[0]
    sel_bf = sel.astype(BF16)
    chosen = _dot(sel_bf, expand_ref[...]).reshape(N_KV, 1, DEC_SEQ, PAST_LEN)
    bias_p = jnp.where(chosen > 0.5, 0.0, NEG_INF)
    new_blk = sel[:, PAST_LEN // SEL_BLOCK:PAST_LEN // SEL_BLOCK + 1].reshape(N_KV, 1, DEC_SEQ, 1)
    npos = PAST_LEN + lax.broadcasted_iota(jnp.int32, (DEC_SEQ, DEC_SEQ), 1)
    bias_n = jnp.where((new_blk > 0.5) & (npos <= qpos)[None, None], 0.0, NEG_INF)
    s_p = s_p + bias_p
    s_n = s_n + bias_n
    m = jnp.maximum(jnp.max(s_p, axis=-1, keepdims=True), jnp.max(s_n, axis=-1, keepdims=True))
    e_p = jnp.exp(s_p - m)
    e_n = jnp.exp(s_n - m)
    den = jnp.sum(e_p, axis=-1, keepdims=True) + jnp.sum(e_n, axis=-1, keepdims=True)
    o_slc = (_dot_nt(e_p.reshape(nq, PAST_LEN).astype(BF16), past)
             + _dot(e_n.reshape(nq, DEC_SEQ).astype(BF16), new)) / jnp.maximum(den.reshape(nq, 1), TINY)

    o_win = _window_cached_attend(qbd, wbuf_ref[0].astype(BF16), wnew_ref[0].astype(BF16), A_WINDOW, None)

    gt = jax.nn.sigmoid(gate_ref[0])
    o_cmp = ocmp_ref[0]
    for kv in range(N_KV):
        v0 = kv * 2 * HEAD_DIM + HEAD_DIM
        for g in range(N_G):
            h = kv * N_G + g
            r0 = h * DEC_SEQ
            c0 = kv * LANES + g
            og = (gt[:, c0:c0 + 1] * o_cmp[:, h * HEAD_DIM:(h + 1) * HEAD_DIM]
                  + gt[:, c0 + N_G:c0 + N_G + 1] * o_slc[r0:r0 + DEC_SEQ, v0:v0 + HEAD_DIM]
                  + gt[:, c0 + 2 * N_G:c0 + 2 * N_G + 1] * o_win[r0:r0 + DEC_SEQ, v0:v0 + HEAD_DIM])
            o_ref[0, :, h * HEAD_DIM:(h + 1) * HEAD_DIM] = og


def _nsa_sample_slc(ia, page_table, cache, z, q_rot, sel, win_cache, o_cmp):
    n = z.shape[0]
    expand = (jnp.arange(PAST_LEN)[None, :] // SEL_BLOCK == jnp.arange(S_SEL)[:, None]).astype(BF16)
    row_blk = lambda base: pl.BlockSpec((1, DEC_SEQ, KV_ROW), lambda a, pt: (a, 0, base // KV_ROW))
    grid_spec = pltpu.PrefetchScalarGridSpec(
        num_scalar_prefetch=1,
        grid=(n,),
        in_specs=[row_blk(A_SLC0),
                  pl.BlockSpec((1, DEC_SEQ, D_MODEL), lambda a, pt: (a, 0, 0)),
                  pl.BlockSpec((1, N_KV * DEC_SEQ, S_SEL), lambda a, pt: (a, 0, 0)),
                  pl.BlockSpec((None, 1, KV_ROW, A_WINDOW), lambda a, pt: (ia, a, 0, 0)),
                  row_blk(A_WIN0), row_blk(A_GATE0),
                  pl.BlockSpec((1, DEC_SEQ, D_MODEL), lambda a, pt: (a, 0, 0)),
                  pl.BlockSpec((S_SEL, PAST_LEN), lambda a, pt: (0, 0)),
                  pl.BlockSpec(memory_space=pl.ANY)],
        out_specs=[pl.BlockSpec((1, DEC_SEQ, D_MODEL), lambda a, pt: (a, 0, 0)),
                   pl.BlockSpec((1, KV_ROW, A_WINDOW), lambda a, pt: (a, 0, 0))],
        scratch_shapes=[pltpu.VMEM((2, KV_ROW, PAST_LEN), F32),
                        pltpu.VMEM((KV_ROW, PAST_LEN), BF16),
                        pltpu.SemaphoreType.DMA((2,))])
    return pl.pallas_call(
        functools.partial(_nsa_sample_slc_kernel, ia=ia),
        grid_spec=grid_spec,
        out_shape=[jax.ShapeDtypeStruct((n, DEC_SEQ, D_MODEL), F32),
                   jax.ShapeDtypeStruct((n, KV_ROW, A_WINDOW), F32)],
        compiler_params=_cparams(("arbitrary",)),
        name="nsa_sample_slc",
    )(page_table, z, q_rot, sel, win_cache, z, z, o_cmp, expand, cache)


def _swa_sample_kernel(q_ref, new_ref, wbuf_ref, sink_ref, o_ref, wout_ref):
    def one(i, c):
        new = new_ref[i]
        qbd = (_sample_q_blockdiag(q_ref[i]) * ATTN_SCALE).astype(BF16)
        o = _window_cached_attend(qbd, wbuf_ref[i].astype(BF16), new.astype(BF16), B_WINDOW, sink_ref[...])
        for kv in range(N_KV):
            v0 = kv * 2 * HEAD_DIM + HEAD_DIM
            for g in range(N_G):
                h = kv * N_G + g
                o_ref[i, :, h * HEAD_DIM:(h + 1) * HEAD_DIM] = o[h * DEC_SEQ:(h + 1) * DEC_SEQ, v0:v0 + HEAD_DIM]
        wout_ref[i] = _advance_window(wbuf_ref[i], new)
        return c
    lax.fori_loop(0, q_ref.shape[0], one, 0)


def _swa_sample_attend(ib, z, win_cache, sinks):
    n = z.shape[0]
    nb = math.gcd(n, SWA_SEQS_PER_STEP)
    return pl.pallas_call(
        _swa_sample_kernel,
        grid=(n // nb,),
        in_specs=[pl.BlockSpec((nb, DEC_SEQ, D_MODEL), lambda a: (a, 0, 0)),
                  pl.BlockSpec((nb, DEC_SEQ, KV_ROW), lambda a: (a, 0, D_MODEL // KV_ROW)),
                  pl.BlockSpec((None, nb, KV_ROW, B_WINDOW), lambda a: (ib, a, 0, 0)),
                  pl.BlockSpec((N_HEADS, 1, 1), lambda a: (0, 0, 0))],
        out_specs=[pl.BlockSpec((nb, DEC_SEQ, D_MODEL), lambda a: (a, 0, 0)),
                   pl.BlockSpec((nb, KV_ROW, B_WINDOW), lambda a: (a, 0, 0))],
        out_shape=[jax.ShapeDtypeStruct((n, DEC_SEQ, D_MODEL), F32),
                   jax.ShapeDtypeStruct((n, KV_ROW, B_WINDOW), F32)],
        compiler_params=_cparams(("arbitrary",)),
        name="swa_sample_attend",
    )(z, z, win_cache, sinks.reshape(N_HEADS, 1, 1))


def _nsa_col_perm():
    perm = np.full((A_COLS,), -1, np.int64)
    perm[:D_MODEL] = np.arange(D_MODEL)
    kvw = N_KV * HEAD_DIM
    for br, base in enumerate((A_CMP0, A_SLC0, A_WIN0)):
        old0 = D_MODEL + 2 * kvw * br
        for kv in range(N_KV):
            for i in range(2):
                dst = base + kv * 2 * HEAD_DIM + i * HEAD_DIM
                src = old0 + i * kvw + kv * HEAD_DIM
                perm[dst:dst + HEAD_DIM] = np.arange(src, src + HEAD_DIM)
    old_g = D_MODEL + 6 * kvw
    for kv in range(N_KV):
        for br in range(3):
            dst = A_GATE0 + kv * LANES + br * N_G
            src = old_g + br * N_HEADS + kv * N_G
            perm[dst:dst + N_G] = np.arange(src, src + N_G)
    return perm


def _swa_col_perm():
    perm = np.zeros((B_COLS,), np.int64)
    perm[:D_MODEL] = np.arange(D_MODEL)
    kvw = N_KV * HEAD_DIM
    for kv in range(N_KV):
        for i in range(2):
            dst = D_MODEL + kv * 2 * HEAD_DIM + i * HEAD_DIM
            src = D_MODEL + i * kvw + kv * HEAD_DIM
            perm[dst:dst + HEAD_DIM] = np.arange(src, src + HEAD_DIM)
    return perm


def _permute_cols(w, perm):
    parts, start = [], 0
    for j in range(1, len(perm) + 1):
        if j == len(perm) or (perm[j] != perm[j - 1] + 1 if perm[j - 1] >= 0 else perm[j] >= 0):
            if perm[start] < 0:
                parts.append(jnp.zeros(w.shape[:-1] + (j - start,), w.dtype))
            else:
                parts.append(w[..., int(perm[start]):int(perm[start]) + j - start])
            start = j
    return jnp.concatenate(parts, axis=-1)


def _rows_view(z, c0, n, t):
    return z[..., c0:c0 + KV_ROW].reshape(n, t, N_KV, 2, HEAD_DIM)


def _rows_from_t(rows_t):
    n, _, t = rows_t.shape
    return rows_t.reshape(n, N_KV, 2, HEAD_DIM, t).transpose(0, 4, 1, 2, 3)


def _trunk(x, ada, past, p, prompt):
    g, t, _ = x.shape
    tm = min(512, t)
    if prompt:
        pos = jnp.arange(t, dtype=jnp.int32)
        n_seq, t_seq = g, t
    else:
        n_seq, t_seq = t // DEC_SEQ, DEC_SEQ
        pos = jnp.tile(PAST_LEN + jnp.arange(DEC_SEQ, dtype=jnp.int32), n_seq)
    tab = _rope_tables(pos)[None]
    cmp_out, slc_out, nwin_out, swin_out = [], [], [], []
    for l in range(DEPTH):
        sh1, sc1, g1, sh2, sc2, g2 = [ada[l][..., k * D_MODEL:(k + 1) * D_MODEL] for k in range(6)]
        i = l // 2
        if l % 2 == 0:
            z, q_rot, *rows_t = _proj(x, sc1, sh1, p['nsa_w_in'][i], p['zero_a'], tab, tm, (A_CMP0,),
                                      (A_SLC0, A_WIN0), True, prompt)
            if prompt:
                cmp_rows, slc_rows, win_rows = [_rows_from_t(r) for r in rows_t]
            else:
                cmp_rows, slc_rows, win_rows = [_rows_view(z, c0, n_seq, t_seq) for c0 in (A_CMP0, A_SLC0, A_WIN0)]
            w1r, pb, w2 = p['cmp_w1r'][i], p['cmp_pb'][i], p['cmp_w2'][i]
            if prompt:
                nch = t // CMP_STRIDE
                chunks = z[..., A_CMP0:A_CMP0 + KV_ROW].reshape(g, nch, CMP_STRIDE, 2 * N_KV, HEAD_DIM)
                chunks = chunks.transpose(0, 3, 1, 2, 4).reshape(g, 2 * N_KV, nch, CHUNK_W)
                cblocks = _compress_prompt(chunks, w1r, pb, w2)
                o = _nsa_prompt_attend(z, q_rot, cblocks)
                nwin_out.append(win_rows[:, -min(A_WINDOW, t):])
            else:
                cache_cmp, cache_slc, nwin_t, _, page_table = past
                zs = z.reshape(n_seq, DEC_SEQ, A_COLS)
                qs = zs[..., :D_MODEL]
                o_cmp, sel = _nsa_sample_cmp(i, page_table, cache_cmp, zs[..., A_CMP0:A_CMP0 + KV_ROW], qs, w1r, pb, w2)
                o, win_next = _nsa_sample_slc(i, page_table, cache_slc, zs, q_rot.reshape(n_seq, DEC_SEQ, D_MODEL), sel,
                                              nwin_t, o_cmp)
                o = o.reshape(g, t, D_MODEL)
                nwin_out.append(_rows_from_t(win_next))
            cmp_out.append(cmp_rows)
            slc_out.append(slc_rows)
            w_o, b_o = p['nsa_w_o'][i], p['zero_d']
        else:
            z, *rows_t = _proj(x, sc1, sh1, p['swa_w_in'][i], p['swa_b_in'][i], tab, tm, (), (D_MODEL,), False, prompt)
            rows = _rows_from_t(rows_t[0]) if prompt else _rows_view(z, D_MODEL, n_seq, t_seq)
            if prompt:
                o = _swa_prompt_attend(z, p['swa_sinks'][i])
                swin_out.append(rows[:, -min(B_WINDOW, t):])
            else:
                o, win_next = _swa_sample_attend(i, z.reshape(n_seq, DEC_SEQ, B_COLS), past[3], p['swa_sinks'][i])
                o = o.reshape(g, t, D_MODEL)
                swin_out.append(_rows_from_t(win_next))
            w_o, b_o = p['swa_w_o'][i], p['swa_b_o'][i]
        x = _outproj_ln(o, x, g1, w_o, b_o, p['ln_g'][l, 0], p['ln_b'][l, 0], tm)
        x = _moe_ln(x, sc2, sh2, g2, p['router_wt'], p['router_b'], p['moe_w_gate'][l], p['moe_w_up'][l],
                    p['moe_w_down'][l], p['ln_g'][l, 1], p['ln_b'][l, 1], min(MOE_TILE, t))
    return x, (jnp.stack(cmp_out), jnp.stack(slc_out), jnp.stack(nwin_out), jnp.stack(swin_out))


def kernel(x_prompt, x_sample, cache_nsa_cmp, cache_nsa_slc, cache_nsa_win, cache_swa_win, page_table, c_prompt, c_sample, nsa_w_in, nsa_cmp_pe, nsa_cmp_w1, nsa_cmp_b1, nsa_cmp_w2, nsa_w_o, swa_w_in, swa_b_in, swa_sinks, swa_w_o, swa_b_o, ada_w, ada_b, ln_g, ln_b, router_w, router_b, moe_w_gate, moe_w_up, moe_w_down):
    n_a = nsa_w_in.shape[0]
    batch, seq, _ = x_prompt.shape
    n_dec = x_sample.shape[0]

    pe_term = jnp.stack([_pe_term(nsa_cmp_pe[i], nsa_cmp_w1[i]) for i in range(n_a)])
    w1r = nsa_cmp_w1.reshape(n_a, 2, CMP_SHIFTS, CHUNK_W, CMP_HIDDEN).transpose(0, 1, 3, 2, 4)
    p = {
        'nsa_w_in': _permute_cols(nsa_w_in, _nsa_col_perm()).astype(BF16),
        'zero_a': jnp.zeros((1, A_COLS), F32),
        'zero_d': jnp.zeros((1, D_MODEL), F32),
        'cmp_w1r': w1r.reshape(n_a, 2, CHUNK_W, CMP_SHIFTS * CMP_HIDDEN).astype(BF16),
        'cmp_pb': (pe_term + nsa_cmp_b1).reshape(n_a, 2, 1, CMP_HIDDEN),
        'cmp_w2': nsa_cmp_w2.astype(BF16),
        'nsa_w_o': nsa_w_o.astype(BF16),
        'swa_w_in': _permute_cols(swa_w_in, _swa_col_perm()).astype(BF16),
        'swa_b_in': _permute_cols(swa_b_in, _swa_col_perm())[:, None, :],
        'swa_sinks': swa_sinks,
        'swa_w_o': swa_w_o.astype(BF16),
        'swa_b_o': swa_b_o[:, None, :],
        'ln_g': ln_g[:, :, None, :],
        'ln_b': ln_b[:, :, None, :],
        'router_wt': router_w.T,
        'router_b': router_b[:, None],
        'moe_w_gate': moe_w_gate.astype(BF16),
        'moe_w_up': moe_w_up.astype(BF16),
        'moe_w_down': moe_w_down.astype(BF16),
    }

    ada = _ada(jnp.concatenate([c_prompt, c_sample], axis=0), ada_w, ada_b)
    ada_p = ada[:, :batch, None, :]
    ada_s = jnp.broadcast_to(ada[:, batch:, None, :], (DEPTH, n_dec, DEC_SEQ, 6 * D_MODEL))
    ada_s = ada_s.reshape(DEPTH, 1, n_dec * DEC_SEQ, 6 * D_MODEL)

    y_p, (cmp_p, slc_p, nwin_p, swin_p) = _trunk(x_prompt, ada_p, None, p, True)
    def feature_major(c):
        return c.transpose(0, 1, 3, 4, 5, 2).reshape(c.shape[0], c.shape[1], KV_ROW, c.shape[2])
    past = (feature_major(cache_nsa_cmp), feature_major(cache_nsa_slc), feature_major(cache_nsa_win),
            feature_major(cache_swa_win), page_table)
    y_s, (cmp_s, slc_s, nwin_s, swin_s) = _trunk(x_sample.reshape(1, n_dec * DEC_SEQ, D_MODEL), ada_s, past, p, False)
    return (y_p, y_s.reshape(n_dec, DEC_SEQ, D_MODEL), cmp_p, slc_p, nwin_p, swin_p, cmp_s, slc_s, nwin_s, swin_s)
```

```python
import functools
import math

import numpy as np
import jax
import jax.numpy as jnp
from jax import lax
from jax.experimental import pallas as pl
from jax.experimental.pallas import tpu as pltpu

D_MODEL = 1024
SEQ = 2048
DEPTH = 4
DEC_SEQ = 8
PAST_LEN = 8192
PAGE_SIZE = 128
N_PAGES = PAST_LEN // PAGE_SIZE

HEAD_DIM = 64
ROT_DIM = HEAD_DIM // 4
ROT_HALF = ROT_DIM // 2
ROPE_THETA = 500000.0
ATTN_SCALE = HEAD_DIM ** -0.5

N_HEADS = D_MODEL // HEAD_DIM
N_KV = 2
N_G = N_HEADS // N_KV
KV_ROW = N_KV * 2 * HEAD_DIM
CMP_BLOCK = 32
CMP_STRIDE = 16
CMP_HIDDEN = 4 * HEAD_DIM
CMP_SHIFTS = CMP_BLOCK // CMP_STRIDE
CHUNK_W = CMP_STRIDE * HEAD_DIM
SEL_BLOCK = 64
SEL_TOP = 16
CHUNKS_PER_SEL = SEL_BLOCK // CMP_STRIDE
A_WINDOW = 512
B_WINDOW = 128
FORCE_SCORE = 1e9

N_EXPERTS = 16
N_GROUPS = 4
EXPERTS_PER_GROUP = N_EXPERTS // N_GROUPS
D_FF_EXPERT = D_MODEL // 2

DN_ALPHA = (2.0 * DEPTH) ** 0.25
LN_EPS = 1e-5
NEG_INF = -1e30
TINY = 1e-30
LOG2E = math.log2(math.e)

LANES = 128
SUBLANES = 8
MOE_TILE = 1024
MOE_SUB_TILE = 256
SWA_SEQS_PER_STEP = 8
A_COLS = 2048
A_CMP0, A_SLC0, A_WIN0, A_GATE0 = 1024, 1280, 1536, 1792
B_COLS = D_MODEL + KV_ROW
VMEM_LIMIT = 56 * 1024 * 1024

F32 = jnp.float32
BF16 = jnp.bfloat16


def _cparams(sem):
    return pltpu.CompilerParams(dimension_semantics=sem, vmem_limit_bytes=VMEM_LIMIT)


def _dot(a, b):
    return jnp.dot(a, b, preferred_element_type=F32)


def _dot_nt(a, b, precision=None):
    return lax.dot_general(a, b, (((1,), (1,)), ((), ())), precision=precision,
                           preferred_element_type=F32)


def _masked_probs(s, mask):
    s = jnp.where(mask, s, NEG_INF)
    m = jnp.max(s, axis=-1, keepdims=True)
    e = jnp.where(mask, jnp.exp(s - m), 0.0)
    den = jnp.sum(e, axis=-1, keepdims=True)
    return e / jnp.maximum(den, TINY)


def _layer_norm(r, g, b):
    mu = jnp.mean(r, axis=-1, keepdims=True)
    rc = r - mu
    var = jnp.mean(rc * rc, axis=-1, keepdims=True)
    return rc * lax.rsqrt(var + LN_EPS) * g + b


def _ada_kernel(c_ref, w_ref, b_ref, o_ref):
    c = c_ref[...]
    a = (c * jax.nn.sigmoid(c)).astype(BF16)
    o_ref[0] = _dot(a, w_ref[0].astype(BF16)) + b_ref[0]


def _ada(c_all, ada_w, ada_b):
    m = c_all.shape[0]
    n = ada_w.shape[-1]
    tn = 1536
    return pl.pallas_call(
        _ada_kernel,
        grid=(DEPTH, n // tn),
        in_specs=[pl.BlockSpec((m, D_MODEL), lambda l, j: (0, 0)),
                  pl.BlockSpec((1, D_MODEL, tn), lambda l, j: (l, 0, j)),
                  pl.BlockSpec((1, 1, tn), lambda l, j: (l, 0, j))],
        out_specs=pl.BlockSpec((1, m, tn), lambda l, j: (l, 0, j)),
        out_shape=jax.ShapeDtypeStruct((DEPTH, m, n), F32),
        compiler_params=_cparams(("arbitrary", "arbitrary")),
        name="ada",
    )(c_all, ada_w, ada_b.reshape(DEPTH, 1, n))


def _pe_term_kernel(pe_ref, w_ref, o_ref):
    o_ref[0] = _dot(pe_ref[0].astype(BF16), w_ref[0].astype(BF16))


def _pe_term(pe, w1):
    k = CMP_BLOCK * HEAD_DIM
    pe8 = jnp.broadcast_to(pe.reshape(2, 1, k), (2, 8, k))
    out = pl.pallas_call(
        _pe_term_kernel,
        grid=(2,),
        in_specs=[pl.BlockSpec((1, 8, k), lambda i: (i, 0, 0)),
                  pl.BlockSpec((1, k, CMP_HIDDEN), lambda i: (i, 0, 0))],
        out_specs=pl.BlockSpec((1, 8, CMP_HIDDEN), lambda i: (i, 0, 0)),
        out_shape=jax.ShapeDtypeStruct((2, 8, CMP_HIDDEN), F32),
        compiler_params=_cparams(("arbitrary",)),
        name="pe_term",
    )(pe8, w1)
    return out[:, 0]


def _rope_tables(pos):
    inv = jnp.exp(-math.log(ROPE_THETA) * jnp.arange(ROT_HALF, dtype=F32) / ROT_HALF)
    ang = pos.astype(F32)[:, None] * inv[None, :]
    cos, sin = jnp.cos(ang), jnp.sin(ang)
    t = pos.shape[0]
    one = jnp.ones((t, HEAD_DIM - ROT_DIM), F32)
    zero = jnp.zeros((t, HEAD_DIM - ROT_DIM), F32)
    zh = jnp.zeros((t, ROT_HALF), F32)
    cos_h = jnp.concatenate([cos, cos, one], axis=1)
    sa_h = jnp.concatenate([-sin, zh, zero], axis=1)
    sb_h = jnp.concatenate([zh, sin, zero], axis=1)
    id_h = jnp.ones((t, HEAD_DIM), F32)
    z_h = jnp.zeros((t, HEAD_DIM), F32)
    return jnp.concatenate([cos_h, cos_h, sa_h, sa_h, sb_h, sb_h,
                            cos_h, id_h, sa_h, z_h, sb_h, z_h], axis=1)


def _rope_apply(seg, cos, sa, sb):
    w = seg.shape[1]
    reps = w // LANES
    c = jnp.tile(cos, (1, reps))
    a = jnp.tile(sa, (1, reps))
    b = jnp.tile(sb, (1, reps))
    return seg * c + pltpu.roll(seg, w - ROT_HALF, 1) * a + pltpu.roll(seg, ROT_HALF, 1) * b


def _mod_operand(mod, tm, n_grid):
    ada, l, k = mod
    per_token = ada.shape[2] > 1
    if n_grid == 2:
        imap = (lambda a, i: (l, a, i, k)) if per_token else (lambda a, i: (l, a, 0, k))
    else:
        imap = (lambda a, i, e: (l, a, i, k)) if per_token else (lambda a, i, e: (l, a, 0, k))
    return ada, pl.BlockSpec((None, 1, tm if per_token else 1, D_MODEL), imap)


def _proj_kernel(x_ref, sc_ref, sh_ref, w_ref, b_ref, tab_ref, z_ref, *more_refs, plain_segs, kv_segs, sep_qrot,
                 rows_t):
    more_refs = list(more_refs)
    x = x_ref[0]
    h = x * (1.0 + sc_ref[0]) + sh_ref[0]
    z = _dot(h.astype(BF16), w_ref[...]) + b_ref[...]
    tab = tab_ref[0]
    tq = [tab[:, i * LANES:(i + 1) * LANES] for i in range(3)]
    tk = [tab[:, i * LANES:(i + 1) * LANES] for i in range(3, 6)]
    q_rot = _rope_apply(z[:, :D_MODEL], *tq)
    z_ref[0] = z
    if sep_qrot:
        more_refs.pop(0)[0] = q_rot
    else:
        z_ref[0, :, :D_MODEL] = q_rot
    for c0 in plain_segs:
        if rows_t:
            more_refs.pop(0)[0] = z[:, c0:c0 + KV_ROW].T
    for c0 in kv_segs:
        seg = _rope_apply(z[:, c0:c0 + KV_ROW], *tk)
        z_ref[0, :, c0:c0 + KV_ROW] = seg
        if rows_t:
            more_refs.pop(0)[0] = seg.T


def _proj(x, sc, sh, w, b, tab, tm, plain_segs, kv_segs, sep_qrot, rows_t):
    g, t, _ = x.shape
    n = w.shape[1]
    sc, sc_spec = _mod_operand(sc, tm, 2)
    sh, sh_spec = _mod_operand(sh, tm, 2)
    out_shape = [jax.ShapeDtypeStruct((g, t, n), F32)]
    out_specs = [pl.BlockSpec((1, tm, n), lambda a, i: (a, i, 0))]
    if sep_qrot:
        out_shape.append(jax.ShapeDtypeStruct((g, t, D_MODEL), F32))
        out_specs.append(pl.BlockSpec((1, tm, D_MODEL), lambda a, i: (a, i, 0)))
    if rows_t:
        for _ in plain_segs + kv_segs:
            out_shape.append(jax.ShapeDtypeStruct((g, KV_ROW, t), F32))
            out_specs.append(pl.BlockSpec((1, KV_ROW, tm), lambda a, i: (a, 0, i)))
    return pl.pallas_call(
        functools.partial(_proj_kernel, plain_segs=plain_segs, kv_segs=kv_segs, sep_qrot=sep_qrot, rows_t=rows_t),
        grid=(g, t // tm),
        in_specs=[pl.BlockSpec((1, tm, D_MODEL), lambda a, i: (a, i, 0)),
                  sc_spec, sh_spec,
                  pl.BlockSpec((D_MODEL, n), lambda a, i: (0, 0)),
                  pl.BlockSpec((1, n), lambda a, i: (0, 0)),
                  pl.BlockSpec((1, tm, 6 * LANES), lambda a, i: (0, i, 0))],
        out_specs=out_specs,
        out_shape=out_shape,
        compiler_params=_cparams(("arbitrary", "arbitrary")),
        name="proj",
    )(x, sc, sh, w, b, tab)


def _outproj_kernel(o_ref, x_ref, g_ref, w_ref, b_ref, lg_ref, lb_ref, out_ref):
    y = _dot(o_ref[0].astype(BF16), w_ref[...]) + b_ref[...]
    r = DN_ALPHA * x_ref[0] + g_ref[0] * y
    out_ref[0] = _layer_norm(r, lg_ref[...], lb_ref[...])


def _outproj_ln(o, x, gate, w, b, lg, lb, tm):
    g, t, _ = x.shape
    gate, mod_spec = _mod_operand(gate, tm, 2)
    row = pl.BlockSpec((1, tm, D_MODEL), lambda a, i: (a, i, 0))
    vec = pl.BlockSpec((1, D_MODEL), lambda a, i: (0, 0))
    return pl.pallas_call(
        _outproj_kernel,
        grid=(g, t // tm),
        in_specs=[row, row, mod_spec,
                  pl.BlockSpec((D_MODEL, D_MODEL), lambda a, i: (0, 0)), vec, vec, vec],
        out_specs=row,
        out_shape=jax.ShapeDtypeStruct((g, t, D_MODEL), F32),
        compiler_params=_cparams(("arbitrary", "arbitrary")),
        name="outproj_ln",
    )(o, x, gate, w, b, lg, lb)


def _router_gates_t(h, rwt_ref, rb_ref):
    logits = _dot_nt(rwt_ref[...], h, precision=lax.Precision.HIGHEST)
    m = jnp.max(logits, axis=0, keepdims=True)
    ex = jnp.exp(logits - m)
    probs = ex / jnp.sum(ex, axis=0, keepdims=True)
    biased = probs + rb_ref[...]
    eidx = lax.broadcasted_iota(jnp.int32, biased.shape, 0)
    best = None
    g_sel = None
    for g in range(N_GROUPS):
        rows = [biased[g * EXPERTS_PER_GROUP + i:g * EXPERTS_PER_GROUP + i + 1] for i in range(EXPERTS_PER_GROUP)]
        gs = None
        for i in range(EXPERTS_PER_GROUP):
            for j in range(i + 1, EXPERTS_PER_GROUP):
                pair = rows[i] + rows[j]
                gs = pair if gs is None else jnp.maximum(gs, pair)
        if best is None:
            best, g_sel = gs, jnp.zeros(gs.shape, jnp.int32)
        else:
            upd = gs > best
            g_sel = jnp.where(upd, g, g_sel)
            best = jnp.maximum(best, gs)
    v = jnp.where(eidx // EXPERTS_PER_GROUP == g_sel, biased, NEG_INF)
    m1 = jnp.max(v, axis=0, keepdims=True)
    i1 = jnp.min(jnp.where(v == m1, eidx, N_EXPERTS), axis=0, keepdims=True)
    v2 = jnp.where(eidx == i1, -jnp.inf, v)
    m2 = jnp.max(v2, axis=0, keepdims=True)
    i2 = jnp.min(jnp.where(v2 == m2, eidx, N_EXPERTS), axis=0, keepdims=True)
    w1 = jnp.sum(jnp.where(eidx == i1, probs, 0.0), axis=0, keepdims=True)
    w2 = jnp.sum(jnp.where(eidx == i2, probs, 0.0), axis=0, keepdims=True)
    tot = w1 + w2
    return jnp.where(eidx == i1, w1 / tot, 0.0) + jnp.where(eidx == i2, w2 / tot, 0.0), g_sel


def _moe_kernel(x_ref, sc_ref, sh_ref, g_ref, rwt_ref, rb_ref, wg_ref, wu_ref, wd_ref, lg_ref, lb_ref,
                out_ref, hs_scr, gate_scr, ys_scr, unperm_scr, seg_smem, *, sub):
    e = pl.program_id(2)
    tm = hs_scr.shape[0]

    @pl.when(e == 0)
    def _():
        h = x_ref[0] * (1.0 + sc_ref[0]) + sh_ref[0]
        gates_t, g_sel = _router_gates_t(h, rwt_ref, rb_ref)
        grp = lax.broadcasted_iota(jnp.int32, (SUBLANES, tm), 0)
        onehot = jnp.where(grp == g_sel, 1.0, 0.0)
        src = lax.broadcasted_iota(jnp.int32, (tm, tm), 0)
        dst = lax.broadcasted_iota(jnp.int32, (tm, tm), 1)
        earlier = jnp.where(src < dst, 1.0, 0.0).astype(BF16)
        cum = _dot(onehot.astype(BF16), earlier)
        pos = jnp.zeros((1, tm), F32)
        off = jnp.int32(0)
        for r in range(N_GROUPS):
            cnt = jnp.sum(onehot[r:r + 1, :]).astype(jnp.int32)
            seg_smem[r] = off
            seg_smem[N_GROUPS + r] = cnt
            pos = pos + onehot[r:r + 1, :] * (cum[r:r + 1, :] + off.astype(F32))
            off = off + cnt
        perm = jnp.where(src == pos.astype(jnp.int32), 1.0, 0.0)
        hs_scr[...] = _dot(perm.astype(BF16), h.astype(BF16)).astype(BF16)
        gs_t = _dot_nt(gates_t, perm, precision=lax.Precision.HIGHEST)
        pad = jnp.zeros((LANES - N_EXPERTS, tm), F32)
        gate_scr[...] = jnp.concatenate([gs_t, pad], axis=0).T
        unperm_scr[...] = perm.T.astype(BF16)
        ys_scr[...] = jnp.zeros_like(ys_scr)

    grp_e = e // EXPERTS_PER_GROUP
    off = seg_smem[grp_e]
    cnt = seg_smem[N_GROUPS + grp_e]
    first = off // sub
    last = (off + cnt - 1) // sub
    for j in range(tm // sub):
        @pl.when((cnt > 0) & (first <= j) & (j <= last))
        def _():
            rows = hs_scr[j * sub:(j + 1) * sub, :]
            a = _dot(rows, wg_ref[0])
            u = _dot(rows, wu_ref[0])
            gates = gate_scr[j * sub:(j + 1) * sub, :]
            lane = lax.broadcasted_iota(jnp.int32, gates.shape, 1)
            gcol = jnp.sum(jnp.where(lane == e, gates, 0.0), axis=1, keepdims=True)
            act = (a * jax.nn.sigmoid(a)) * u * gcol
            ys_scr[j * sub:(j + 1) * sub, :] += _dot(act.astype(BF16), wd_ref[0])

    @pl.when(e == N_EXPERTS - 1)
    def _():
        ys = ys_scr[...]
        hi = ys.astype(BF16)
        lo = (ys - hi.astype(F32)).astype(BF16)
        unperm = unperm_scr[...]
        y = _dot(unperm, hi) + _dot(unperm, lo)
        r = DN_ALPHA * x_ref[0] + g_ref[0] * y
        out_ref[0] = _layer_norm(r, lg_ref[...], lb_ref[...])


def _moe_ln(x, sc, sh, gate, rwt, rb, layer, wg, wu, wd, lg, lb, tm):
    g, t, _ = x.shape
    sub = min(MOE_SUB_TILE, tm)
    sc, sc_spec = _mod_operand(sc, tm, 3)
    sh, sh_spec = _mod_operand(sh, tm, 3)
    gate, gate_spec = _mod_operand(gate, tm, 3)
    row = pl.BlockSpec((1, tm, D_MODEL), lambda a, i, e: (a, i, 0))
    vec = pl.BlockSpec((1, D_MODEL), lambda a, i, e: (0, 0))
    return pl.pallas_call(
        functools.partial(_moe_kernel, sub=sub),
        grid=(g, t // tm, N_EXPERTS),
        in_specs=[row, sc_spec, sh_spec, gate_spec,
                  pl.BlockSpec((N_EXPERTS, D_MODEL), lambda a, i, e: (0, 0)),
                  pl.BlockSpec((N_EXPERTS, 1), lambda a, i, e: (0, 0)),
                  pl.BlockSpec((None, 1, D_MODEL, D_FF_EXPERT), lambda a, i, e: (layer, e, 0, 0)),
                  pl.BlockSpec((None, 1, D_MODEL, D_FF_EXPERT), lambda a, i, e: (layer, e, 0, 0)),
                  pl.BlockSpec((None, 1, D_FF_EXPERT, D_MODEL), lambda a, i, e: (layer, e, 0, 0)),
                  vec, vec],
        out_specs=row,
        out_shape=jax.ShapeDtypeStruct((g, t, D_MODEL), F32),
        scratch_shapes=[pltpu.VMEM((tm, D_MODEL), BF16),
                        pltpu.VMEM((tm, LANES), F32),
                        pltpu.VMEM((tm, D_MODEL), F32),
                        pltpu.VMEM((tm, tm), BF16),
                        pltpu.SMEM((2 * N_GROUPS,), jnp.int32)],
        compiler_params=_cparams(("arbitrary", "arbitrary", "arbitrary")),
        name="moe_ln",
    )(x, sc, sh, gate, rwt, rb, wg, wu, wd, lg, lb)


def _compress_blocks(chunks, w1r, pb, w2):
    nch = chunks.shape[0]
    part = _dot(chunks, w1r)
    pre = part[:, :CMP_HIDDEN] + pltpu.roll(part[:, CMP_HIDDEN:], nch - 1, 0)
    hid = jax.nn.gelu(pre + pb)
    return _dot(hid.astype(BF16), w2)


def _compress_kernel(ch_ref, w1r_ref, pb_ref, w2_ref, o_ref):
    for j in range(2 * N_KV):
        i = j % 2
        o_ref[0, j] = _compress_blocks(ch_ref[0, j].astype(BF16), w1r_ref[i], pb_ref[i], w2_ref[i])


def _compress_prompt(chunks, w1r, pb, w2):
    b, _, nch, _ = chunks.shape
    return pl.pallas_call(
        _compress_kernel,
        grid=(b,),
        in_specs=[pl.BlockSpec((1, 2 * N_KV, nch, CHUNK_W), lambda a: (a, 0, 0, 0)),
                  pl.BlockSpec((2, CHUNK_W, 2 * CMP_HIDDEN), lambda a: (0, 0, 0)),
                  pl.BlockSpec((2, 1, CMP_HIDDEN), lambda a: (0, 0, 0)),
                  pl.BlockSpec((2, CMP_HIDDEN, HEAD_DIM), lambda a: (0, 0, 0))],
        out_specs=pl.BlockSpec((1, 2 * N_KV, nch, HEAD_DIM), lambda a: (a, 0, 0, 0)),
        out_shape=jax.ShapeDtypeStruct((b, 2 * N_KV, nch, HEAD_DIM), F32),
        compiler_params=_cparams(("arbitrary",)),
        name="compress_prompt",
    )(chunks, w1r, pb, w2)


def _stack_heads(q):
    return jnp.concatenate([q[:, g * HEAD_DIM:(g + 1) * HEAD_DIM] for g in range(N_G)], axis=0)


def _chunk_score_matrix_t(n_sel, n_cmp):
    j = lax.broadcasted_iota(jnp.int32, (n_sel, n_cmp), 0)
    c = lax.broadcasted_iota(jnp.int32, (n_sel, n_cmp), 1)
    lo = j * CHUNKS_PER_SEL
    hi = lo + CHUNKS_PER_SEL - 1
    m = jnp.zeros((n_sel, n_cmp), F32)
    for r in range(CMP_SHIFTS):
        m = m + jnp.where((c + r >= lo) & (c + r <= hi), 1.0, 0.0)
    return m


def _nsa_prompt_kernel(q_ref, qr_ref, cb_ref, slc_ref, win_ref, gate_ref, o_ref, m_scr, acc_scr, *, tq, kc):
    qi = pl.program_id(2)
    t0 = qi * tq
    n_cmp = cb_ref.shape[2]
    n_sel = SEQ // SEL_BLOCK
    tpos = t0 + lax.broadcasted_iota(jnp.int32, (tq, 1), 0)

    qs = _stack_heads(q_ref[0]).astype(BF16)
    k_c = cb_ref[0, 0].astype(BF16)
    v_c = cb_ref[0, 1].astype(BF16)
    s = (_dot_nt(qs, k_c) * ATTN_SCALE).reshape(N_G, tq, n_cmp)
    cend = lax.broadcasted_iota(jnp.int32, (tq, n_cmp), 1) * CMP_STRIDE + CMP_BLOCK - 1
    p = _masked_probs(s, (cend <= tpos)[None])
    o_cmp = _dot(p.reshape(N_G * tq, n_cmp).astype(BF16), v_c)
    imp = jnp.sum(p, axis=0)

    score_t = _dot_nt(_chunk_score_matrix_t(n_sel, n_cmp), imp, precision=lax.Precision.HIGHEST)
    jj = lax.broadcasted_iota(jnp.int32, (n_sel, tq), 0)
    tt = t0 + lax.broadcasted_iota(jnp.int32, (n_sel, tq), 1)
    cur = tt // SEL_BLOCK
    forced = (jj == 0) | (jj == cur) | (jj == cur - 1)
    sc = jnp.where(jj * SEL_BLOCK <= tt, jnp.where(forced, FORCE_SCORE, score_t), NEG_INF)
    rank = jnp.zeros((n_sel, tq), F32)
    for i in range(n_sel):
        ri = sc[i:i + 1]
        rank = rank + jnp.where(ri > sc, 1.0, jnp.where((ri == sc) & (jj > i), 1.0, 0.0))
    sel_t = jnp.where(rank < SEL_TOP, 1.0, 0.0)
    sel = jnp.concatenate([sel_t, jnp.zeros((LANES - n_sel, tq), F32)], axis=0).T.astype(BF16)

    qrs = (_stack_heads(qr_ref[0]) * (ATTN_SCALE * LOG2E)).astype(BF16)

    m_scr[...] = jnp.full(m_scr.shape, NEG_INF, F32)
    acc_scr[...] = jnp.zeros(acc_scr.shape, F32)

    def body(c, carry):
        k0 = pl.multiple_of(c * kc, kc)
        rows = slc_ref[0, pl.ds(k0, kc), :]
        k = rows[:, :HEAD_DIM].astype(BF16)
        sk = _dot_nt(qrs, k).reshape(N_G, tq, kc)
        kpos = k0 + lax.broadcasted_iota(jnp.int32, (LANES, kc), 1)
        expand = jnp.where(kpos // SEL_BLOCK == lax.broadcasted_iota(jnp.int32, (LANES, kc), 0), 1.0, 0.0)
        chosen = _dot(sel, expand.astype(BF16))
        sm = sk + jnp.where((chosen > 0.5) & (kpos[:1] <= tpos), 0.0, NEG_INF)[None]
        m = m_scr[...]
        m_new = jnp.maximum(m, jnp.max(sm, axis=-1, keepdims=True))
        e = jnp.exp2(sm - m_new)
        pv = _dot(e.reshape(N_G * tq, kc).astype(BF16), _values_and_ones(rows)).reshape(N_G, tq, LANES)
        acc_scr[...] = jnp.exp2(m - m_new) * acc_scr[...] + pv
        m_scr[...] = m_new
        return carry

    lax.fori_loop(0, (t0 + tq + kc - 1) // kc, body, 0)
    acc = acc_scr[...]
    o_slc = acc[:, :, :HEAD_DIM] / jnp.maximum(acc[:, :, HEAD_DIM:HEAD_DIM + 1], TINY)

    o_win = _band_attend(qrs, win_ref, t0, tq, A_WINDOW, None)

    gt = jax.nn.sigmoid(gate_ref[0])
    o_cmp = o_cmp.reshape(N_G, tq, HEAD_DIM)
    for g in range(N_G):
        og = (gt[:, g:g + 1] * o_cmp[g] + gt[:, N_G + g:N_G + g + 1] * o_slc[g]
              + gt[:, 2 * N_G + g:2 * N_G + g + 1] * o_win[g])
        o_ref[0, :, g * HEAD_DIM:(g + 1) * HEAD_DIM] = og


def _band_attend(qrs, rows_ref, t0, tq, window, sinks):
    span = window + tq
    start = pl.multiple_of(jnp.maximum(t0 - window, 0), tq)
    rows = rows_ref[0, pl.ds(start, span), :]
    k = rows[:, :HEAD_DIM].astype(BF16)
    d = (t0 + lax.broadcasted_iota(jnp.int32, (tq, span), 0)) - (start + lax.broadcasted_iota(jnp.int32, (tq, span), 1))
    s = _dot_nt(qrs, k).reshape(N_G, tq, span) + jnp.where((d >= 0) & (d <= window), 0.0, NEG_INF)[None]
    m = jnp.max(s, axis=-1, keepdims=True)
    if sinks is not None:
        sink = jnp.concatenate([jnp.full((1, tq, 1), sk, F32) for sk in sinks], axis=0)
        m = jnp.maximum(m, sink)
    e = jnp.exp2(s - m)
    den = jnp.sum(e, axis=-1, keepdims=True)
    if sinks is not None:
        den = den + jnp.exp2(sink - m)
    o = _dot(e.reshape(N_G * tq, span).astype(BF16), rows[:, HEAD_DIM:].astype(BF16)).reshape(N_G, tq, HEAD_DIM)
    return o / jnp.maximum(den, TINY)


def _values_and_ones(rows):
    lane = lax.broadcasted_iota(jnp.int32, rows.shape, 1)
    return jnp.where(lane < HEAD_DIM, pltpu.roll(rows, HEAD_DIM, 1), 1.0).astype(BF16)


def _nsa_prompt_attend(z, q_rot, cblocks, tq=128, kc=512):
    b, t, _ = z.shape
    n_cmp = cblocks.shape[2]
    hw = N_G * HEAD_DIM
    blk = lambda base: pl.BlockSpec((1, t, LANES), lambda a, k, i: (a, 0, base // LANES + k))
    return pl.pallas_call(
        functools.partial(_nsa_prompt_kernel, tq=tq, kc=kc),
        grid=(b, N_KV, t // tq),
        in_specs=[pl.BlockSpec((1, tq, hw), lambda a, k, i: (a, i, k)),
                  pl.BlockSpec((1, tq, hw), lambda a, k, i: (a, i, k)),
                  pl.BlockSpec((1, 2, n_cmp, HEAD_DIM), lambda a, k, i: (a, k, 0, 0)),
                  blk(A_SLC0), blk(A_WIN0),
                  pl.BlockSpec((1, tq, LANES), lambda a, k, i: (a, i, A_GATE0 // LANES + k))],
        out_specs=pl.BlockSpec((1, tq, hw), lambda a, k, i: (a, i, k)),
        out_shape=jax.ShapeDtypeStruct((b, t, D_MODEL), F32),
        scratch_shapes=[pltpu.VMEM((N_G, tq, 1), F32), pltpu.VMEM((N_G, tq, LANES), F32)],
        compiler_params=_cparams(("arbitrary", "arbitrary", "arbitrary")),
        name="nsa_prompt_attend",
    )(z, q_rot, cblocks, z, z, z)


def _swa_prompt_kernel(sink_ref, q_ref, rows_ref, o_ref, *, tq):
    kv = pl.program_id(1)
    t0 = pl.program_id(2) * tq
    qrs = (_stack_heads(q_ref[0]) * (ATTN_SCALE * LOG2E)).astype(BF16)
    sinks = [sink_ref[kv, g] * LOG2E for g in range(N_G)]
    o = _band_attend(qrs, rows_ref, t0, tq, B_WINDOW, sinks)
    for g in range(N_G):
        o_ref[0, :, g * HEAD_DIM:(g + 1) * HEAD_DIM] = o[g]


def _swa_prompt_attend(z, sinks, tq=128):
    b, t, _ = z.shape
    hw = N_G * HEAD_DIM
    return pl.pallas_call(
        functools.partial(_swa_prompt_kernel, tq=tq),
        grid=(b, N_KV, t // tq),
        in_specs=[pl.BlockSpec(memory_space=pltpu.SMEM),
                  pl.BlockSpec((1, tq, hw), lambda a, k, i: (a, i, k)),
                  pl.BlockSpec((1, t, LANES), lambda a, k, i: (a, 0, D_MODEL // LANES + k))],
        out_specs=pl.BlockSpec((1, tq, hw), lambda a, k, i: (a, i, k)),
        out_shape=jax.ShapeDtypeStruct((b, t, D_MODEL), F32),
        compiler_params=_cparams(("arbitrary", "arbitrary", "arbitrary")),
        name="swa_prompt_attend",
    )(sinks.reshape(N_KV, N_G), z, z)


S_CHUNKS = 520
S_ROWS = S_CHUNKS * CMP_STRIDE
S_SEL = 256
N_SEL_SAMPLE = PAST_LEN // SEL_BLOCK + 1
SLABS = CMP_STRIDE // SUBLANES
PAGES_PER_ITER = 8
assert N_SEL_SAMPLE == LANES + 1 and DEC_SEQ == SUBLANES and N_PAGES % PAGES_PER_ITER == 0


def _page_copy(cache_hbm, ia, pt_ref, n, p, buf, slot, sem):
    return pltpu.make_async_copy(cache_hbm.at[ia, pt_ref[n, p]],
                                 buf.at[slot, :, pl.ds(pl.multiple_of(p * PAGE_SIZE, PAGE_SIZE), PAGE_SIZE)],
                                 sem.at[slot])


def _gather_pages(cache_hbm, ia, pt_ref, buf, sem):
    n = pl.program_id(0)
    nn = pl.num_programs(0)
    slot = n % 2

    def start(seq, sl):
        def go(p, c):
            _page_copy(cache_hbm, ia, pt_ref, seq, p, buf, sl, sem).start()
            return c
        lax.fori_loop(0, N_PAGES, go, 0)

    @pl.when(n == 0)
    def _():
        start(0, 0)

    @pl.when(n + 1 < nn)
    def _():
        start(n + 1, 1 - slot)

    def wait(p, c):
        _page_copy(cache_hbm, ia, pt_ref, n, p, buf, slot, sem).wait()
        return c
    lax.fori_loop(0, N_PAGES, wait, 0)
    return slot


def _sample_q_rows(q, kv):
    return jnp.concatenate([q[:, (kv * N_G + g) * HEAD_DIM:(kv * N_G + g + 1) * HEAD_DIM] for g in range(N_G)], axis=0)


def _sample_q_blockdiag(q):
    parts = []
    for kv in range(N_KV):
        qs = _sample_q_rows(q, kv)
        left = jnp.zeros((qs.shape[0], kv * 2 * HEAD_DIM), F32)
        right = jnp.zeros((qs.shape[0], KV_ROW - kv * 2 * HEAD_DIM - HEAD_DIM), F32)
        parts.append(jnp.concatenate([left, qs, right], axis=1) if kv else jnp.concatenate([qs, right], axis=1))
    return jnp.concatenate(parts, axis=0)


def _nsa_sample_cmp_kernel(pt_ref, new_ref, q_ref, w1r_ref, pb_ref, w2_ref, smat_ref, cache_hbm,
                           ocmp_ref, sel_ref, pages, rows, chunks, sem, *, ia):
    slot = _gather_pages(cache_hbm, ia, pt_ref, pages, sem)

    def to_rows(it, c):
        for u in range(PAGES_PER_ITER):
            p = it * PAGES_PER_ITER + u
            r0 = pl.multiple_of(p * PAGE_SIZE, PAGE_SIZE)
            c0 = pl.multiple_of(p * (PAGE_SIZE // SLABS), PAGE_SIZE // SLABS)
            for kv in range(N_KV):
                t = pages[slot, kv * LANES:(kv + 1) * LANES, pl.ds(r0, PAGE_SIZE)].T
                for v in range(PAGE_SIZE // SUBLANES):
                    rows[kv, v % SLABS, pl.ds(c0 + (v // SLABS) * SUBLANES, SUBLANES), :] = t[v * SUBLANES:(v + 1) * SUBLANES]
        return c
    lax.fori_loop(0, N_PAGES // PAGES_PER_ITER, to_rows, 0)
    new_rows = new_ref[0]
    tail = (PAST_LEN // CMP_STRIDE) * SUBLANES
    for kv in range(N_KV):
        rows[kv, 0, tail:tail + DEC_SEQ, :] = new_rows[:, kv * LANES:(kv + 1) * LANES]
        rows[kv, 0, tail + DEC_SEQ:, :] = jnp.zeros((S_CHUNKS * SUBLANES - tail - DEC_SEQ, LANES), F32)
        for slab in range(1, SLABS):
            rows[kv, slab, tail:, :] = jnp.zeros((S_CHUNKS * SUBLANES - tail, LANES), F32)

    lo = lax.broadcasted_iota(jnp.int32, (S_CHUNKS, LANES), 1) < HEAD_DIM
    for k in range(CMP_STRIDE // 2):
        for kv in range(N_KV):
            sa, sb = 2 * k, 2 * k + 1
            a = rows[kv, sa // SUBLANES, pl.ds(sa % SUBLANES, S_CHUNKS, stride=SUBLANES), :]
            b = rows[kv, sb // SUBLANES, pl.ds(sb % SUBLANES, S_CHUNKS, stride=SUBLANES), :]
            chunks[2 * kv, :, k * LANES:(k + 1) * LANES] = jnp.where(lo, a, pltpu.roll(b, HEAD_DIM, 1)).astype(BF16)
            chunks[2 * kv + 1, :, k * LANES:(k + 1) * LANES] = jnp.where(lo, pltpu.roll(a, HEAD_DIM, 1), b).astype(BF16)

    q = q_ref[0]
    qpos = PAST_LEN + lax.broadcasted_iota(jnp.int32, (DEC_SEQ, 1), 0)
    cend = lax.broadcasted_iota(jnp.int32, (DEC_SEQ, S_CHUNKS), 1) * CMP_STRIDE + CMP_BLOCK - 1
    cmask = (cend <= qpos)[None]
    imps = []
    for kv in range(N_KV):
        k_c = _compress_blocks(chunks[2 * kv], w1r_ref[0], pb_ref[0], w2_ref[0]).astype(BF16)
        v_c = _compress_blocks(chunks[2 * kv + 1], w1r_ref[1], pb_ref[1], w2_ref[1]).astype(BF16)
        qs = _sample_q_rows(q, kv).astype(BF16)
        s = (_dot_nt(qs, k_c) * ATTN_SCALE).reshape(N_G, DEC_SEQ, S_CHUNKS)
        p = _masked_probs(s, cmask)
        o = _dot(p.reshape(N_G * DEC_SEQ, S_CHUNKS).astype(BF16), v_c)
        for g in range(N_G):
            h = kv * N_G + g
            ocmp_ref[0, :, h * HEAD_DIM:(h + 1) * HEAD_DIM] = o[g * DEC_SEQ:(g + 1) * DEC_SEQ]
        imps.append(jnp.sum(p, axis=0))
    imp = jnp.concatenate(imps, axis=0)

    score = _dot_nt(imp, smat_ref[...], precision=lax.Precision.HIGHEST)
    shape = score.shape
    jj = lax.broadcasted_iota(jnp.int32, shape, 1)
    tt = PAST_LEN + lax.broadcasted_iota(jnp.int32, shape, 0) % DEC_SEQ
    cur = tt // SEL_BLOCK
    forced = (jj == 0) | (jj == cur) | (jj == cur - 1)
    sc = jnp.where(jj * SEL_BLOCK <= tt, jnp.where(forced, FORCE_SCORE, score), NEG_INF)

    sc_a = sc[:, :LANES]
    extra = jnp.sum(jnp.where(jj == LANES, sc, 0.0), axis=1, keepdims=True)
    lane = lax.broadcasted_iota(jnp.int32, sc_a.shape, 1)
    rank = jnp.where(extra > sc_a, 1.0, 0.0)
    for k in range(1, LANES):
        r = pltpu.roll(sc_a, k, 1)
        rank = rank + jnp.where(r > sc_a, 1.0, jnp.where(r == sc_a, jnp.where(lane >= k, 1.0, 0.0), 0.0))
    rank_x = jnp.sum(jnp.where(sc_a >= extra, 1.0, 0.0), axis=1, keepdims=True)
    sel_ref[0, :, :LANES] = jnp.where(rank < SEL_TOP, 1.0, 0.0)
    sel_ref[0, :, LANES:] = jnp.where((lane == 0) & (rank_x < SEL_TOP), 1.0, 0.0)


def _nsa_sample_cmp(ia, page_table, cache, new_rows, q, w1r, pb, w2):
    n = q.shape[0]
    grid_spec = pltpu.PrefetchScalarGridSpec(
        num_scalar_prefetch=1,
        grid=(n,),
        in_specs=[pl.BlockSpec((1, DEC_SEQ, KV_ROW), lambda a, pt: (a, 0, 0)),
                  pl.BlockSpec((1, DEC_SEQ, D_MODEL), lambda a, pt: (a, 0, 0)),
                  pl.BlockSpec((2, CHUNK_W, 2 * CMP_HIDDEN), lambda a, pt: (0, 0, 0)),
                  pl.BlockSpec((2, 1, CMP_HIDDEN), lambda a, pt: (0, 0, 0)),
                  pl.BlockSpec((2, CMP_HIDDEN, HEAD_DIM), lambda a, pt: (0, 0, 0)),
                  pl.BlockSpec((S_SEL, S_CHUNKS), lambda a, pt: (0, 0)),
                  pl.BlockSpec(memory_space=pl.ANY)],
        out_specs=[pl.BlockSpec((1, DEC_SEQ, D_MODEL), lambda a, pt: (a, 0, 0)),
                   pl.BlockSpec((1, N_KV * DEC_SEQ, S_SEL), lambda a, pt: (a, 0, 0))],
        scratch_shapes=[pltpu.VMEM((2, KV_ROW, PAST_LEN), F32),
                        pltpu.VMEM((N_KV, SLABS, S_CHUNKS * SUBLANES, LANES), F32),
                        pltpu.VMEM((2 * N_KV, S_CHUNKS, CHUNK_W), BF16),
                        pltpu.SemaphoreType.DMA((2,))])
    return pl.pallas_call(
        functools.partial(_nsa_sample_cmp_kernel, ia=ia),
        grid_spec=grid_spec,
        out_shape=[jax.ShapeDtypeStruct((n, DEC_SEQ, D_MODEL), F32),
                   jax.ShapeDtypeStruct((n, N_KV * DEC_SEQ, S_SEL), F32)],
        compiler_params=_cparams(("arbitrary",)),
        name="nsa_sample_cmp",
    )(page_table, new_rows, q, w1r, pb, w2, _chunk_score_matrix_t(S_SEL, S_CHUNKS), cache)


def _window_cached_attend(qbd, wt, new, window, sink):
    nq = qbd.shape[0]
    heads = nq // DEC_SEQ
    qpos = PAST_LEN + lax.broadcasted_iota(jnp.int32, (DEC_SEQ, 1), 0)
    d_p = qpos - (PAST_LEN - window + lax.broadcasted_iota(jnp.int32, (DEC_SEQ, window), 1))
    d_n = qpos - (PAST_LEN + lax.broadcasted_iota(jnp.int32, (DEC_SEQ, DEC_SEQ), 1))
    s_p = _dot(qbd, wt).reshape(heads, DEC_SEQ, window) + jnp.where((d_p >= 0) & (d_p <= window), 0.0, NEG_INF)[None]
    s_n = _dot_nt(qbd, new).reshape(heads, DEC_SEQ, DEC_SEQ) + jnp.where((d_n >= 0) & (d_n <= window), 0.0, NEG_INF)[None]
    m = jnp.maximum(jnp.max(s_p, axis=-1, keepdims=True), jnp.max(s_n, axis=-1, keepdims=True))
    if sink is not None:
        m = jnp.maximum(m, sink)
    e_p = jnp.exp(s_p - m)
    e_n = jnp.exp(s_n - m)
    den = jnp.sum(e_p, axis=-1, keepdims=True) + jnp.sum(e_n, axis=-1, keepdims=True)
    if sink is not None:
        den = den + jnp.exp(sink - m)
    o = _dot_nt(e_p.reshape(nq, window).astype(BF16), wt) + _dot(e_n.reshape(nq, DEC_SEQ).astype(BF16), new)
    return o / jnp.maximum(den.reshape(nq, 1), TINY)


def _advance_window(wt, new):
    w = wt.shape[1]
    shifted = pltpu.roll(wt, w - DEC_SEQ, 1)
    new_t = jnp.concatenate([new, jnp.zeros((LANES - DEC_SEQ, KV_ROW), F32)], axis=0).T
    new_t = pltpu.roll(new_t, LANES - DEC_SEQ, 1)
    lane = lax.broadcasted_iota(jnp.int32, (KV_ROW, LANES), 1)
    tail = jnp.where(lane >= LANES - DEC_SEQ, new_t, shifted[:, w - LANES:])
    return tail if w == LANES else jnp.concatenate([shifted[:, :w - LANES], tail], axis=1)


def _nsa_sample_slc_kernel(pt_ref, new_ref, qr_ref, sel_ref, wbuf_ref, wnew_ref, gate_ref, ocmp_ref, expand_ref,
                           cache_hbm, o_ref, wout_ref, pages, pages_bf, sem, *, ia):
    slot = _gather_pages(cache_hbm, ia, pt_ref, pages, sem)
    wout_ref[0] = _advance_window(wbuf_ref[0], wnew_ref[0])
    pages_bf[...] = pages[slot].astype(BF16)

    nq = N_KV * N_G * DEC_SEQ
    qbd = (_sample_q_blockdiag(qr_ref[0]) * ATTN_SCALE).astype(BF16)
    qpos = PAST_LEN + lax.broadcasted_iota(jnp.int32, (DEC_SEQ, 1), 0)

    past = pages_bf[...]
    new = new_ref[0].astype(BF16)
    s_p = _dot(qbd, past).reshape(N_KV, N_G, DEC_SEQ, PAST_LEN)
    s_n = _dot_nt(qbd, new).reshape(N_KV, N_G, DEC_SEQ, DEC_SEQ)
    sel = sel_ref[0]
    sel_bf = sel.astype(BF16)
    chosen = _dot(sel_bf, expand_ref[...]).reshape(N_KV, 1, DEC_SEQ, PAST_LEN)
    bias_p = jnp.where(chosen > 0.5, 0.0, NEG_INF)
    new_blk = sel[:, PAST_LEN // SEL_BLOCK:PAST_LEN // SEL_BLOCK + 1].reshape(N_KV, 1, DEC_SEQ, 1)
    npos = PAST_LEN + lax.broadcasted_iota(jnp.int32, (DEC_SEQ, DEC_SEQ), 1)
    bias_n = jnp.where((new_blk > 0.5) & (npos <= qpos)[None, None], 0.0, NEG_INF)
    s_p = s_p + bias_p
    s_n = s_n + bias_n
    m = jnp.maximum(jnp.max(s_p, axis=-1, keepdims=True), jnp.max(s_n, axis=-1, keepdims=True))
    e_p = jnp.exp(s_p - m)
    e_n = jnp.exp(s_n - m)
    den = jnp.sum(e_p, axis=-1, keepdims=True) + jnp.sum(e_n, axis=-1, keepdims=True)
    o_slc = (_dot_nt(e_p.reshape(nq, PAST_LEN).astype(BF16), past)
             + _dot(e_n.reshape(nq, DEC_SEQ).astype(BF16), new)) / jnp.maximum(den.reshape(nq, 1), TINY)

    o_win = _window_cached_attend(qbd, wbuf_ref[0].astype(BF16), wnew_ref[0].astype(BF16), A_WINDOW, None)

    gt = jax.nn.sigmoid(gate_ref[0])
    o_cmp = ocmp_ref[0]
    for kv in range(N_KV):
        v0 = kv * 2 * HEAD_DIM + HEAD_DIM
        for g in range(N_G):
            h = kv * N_G + g
            r0 = h * DEC_SEQ
            c0 = kv * LANES + g
            og = (gt[:, c0:c0 + 1] * o_cmp[:, h * HEAD_DIM:(h + 1) * HEAD_DIM]
                  + gt[:, c0 + N_G:c0 + N_G + 1] * o_slc[r0:r0 + DEC_SEQ, v0:v0 + HEAD_DIM]
                  + gt[:, c0 + 2 * N_G:c0 + 2 * N_G + 1] * o_win[r0:r0 + DEC_SEQ, v0:v0 + HEAD_DIM])
            o_ref[0, :, h * HEAD_DIM:(h + 1) * HEAD_DIM] = og


def _nsa_sample_slc(ia, page_table, cache, z, q_rot, sel, win_cache, o_cmp):
    n = z.shape[0]
    expand = (jnp.arange(PAST_LEN)[None, :] // SEL_BLOCK == jnp.arange(S_SEL)[:, None]).astype(BF16)
    row_blk = lambda base: pl.BlockSpec((1, DEC_SEQ, KV_ROW), lambda a, pt: (a, 0, base // KV_ROW))
    grid_spec = pltpu.PrefetchScalarGridSpec(
        num_scalar_prefetch=1,
        grid=(n,),
        in_specs=[row_blk(A_SLC0),
                  pl.BlockSpec((1, DEC_SEQ, D_MODEL), lambda a, pt: (a, 0, 0)),
                  pl.BlockSpec((1, N_KV * DEC_SEQ, S_SEL), lambda a, pt: (a, 0, 0)),
                  pl.BlockSpec((None, 1, KV_ROW, A_WINDOW), lambda a, pt: (ia, a, 0, 0)),
                  row_blk(A_WIN0), row_blk(A_GATE0),
                  pl.BlockSpec((1, DEC_SEQ, D_MODEL), lambda a, pt: (a, 0, 0)),
                  pl.BlockSpec((S_SEL, PAST_LEN), lambda a, pt: (0, 0)),
                  pl.BlockSpec(memory_space=pl.ANY)],
        out_specs=[pl.BlockSpec((1, DEC_SEQ, D_MODEL), lambda a, pt: (a, 0, 0)),
                   pl.BlockSpec((1, KV_ROW, A_WINDOW), lambda a, pt: (a, 0, 0))],
        scratch_shapes=[pltpu.VMEM((2, KV_ROW, PAST_LEN), F32),
                        pltpu.VMEM((KV_ROW, PAST_LEN), BF16),
                        pltpu.SemaphoreType.DMA((2,))])
    return pl.pallas_call(
        functools.partial(_nsa_sample_slc_kernel, ia=ia),
        grid_spec=grid_spec,
        out_shape=[jax.ShapeDtypeStruct((n, DEC_SEQ, D_MODEL), F32),
                   jax.ShapeDtypeStruct((n, KV_ROW, A_WINDOW), F32)],
        compiler_params=_cparams(("arbitrary",)),
        name="nsa_sample_slc",
    )(page_table, z, q_rot, sel, win_cache, z, z, o_cmp, expand, cache)


def _swa_sample_kernel(q_ref, new_ref, wbuf_ref, sink_ref, o_ref, wout_ref):
    def one(i, c):
        new = new_ref[i]
        qbd = (_sample_q_blockdiag(q_ref[i]) * ATTN_SCALE).astype(BF16)
        o = _window_cached_attend(qbd, wbuf_ref[i].astype(BF16), new.astype(BF16), B_WINDOW, sink_ref[...])
        for kv in range(N_KV):
            v0 = kv * 2 * HEAD_DIM + HEAD_DIM
            for g in range(N_G):
                h = kv * N_G + g
                o_ref[i, :, h * HEAD_DIM:(h + 1) * HEAD_DIM] = o[h * DEC_SEQ:(h + 1) * DEC_SEQ, v0:v0 + HEAD_DIM]
        wout_ref[i] = _advance_window(wbuf_ref[i], new)
        return c
    lax.fori_loop(0, q_ref.shape[0], one, 0)


def _swa_sample_attend(ib, z, win_cache, sinks):
    n = z.shape[0]
    nb = math.gcd(n, SWA_SEQS_PER_STEP)
    return pl.pallas_call(
        _swa_sample_kernel,
        grid=(n // nb,),
        in_specs=[pl.BlockSpec((nb, DEC_SEQ, D_MODEL), lambda a: (a, 0, 0)),
                  pl.BlockSpec((nb, DEC_SEQ, KV_ROW), lambda a: (a, 0, D_MODEL // KV_ROW)),
                  pl.BlockSpec((None, nb, KV_ROW, B_WINDOW), lambda a: (ib, a, 0, 0)),
                  pl.BlockSpec((N_HEADS, 1, 1), lambda a: (0, 0, 0))],
        out_specs=[pl.BlockSpec((nb, DEC_SEQ, D_MODEL), lambda a: (a, 0, 0)),
                   pl.BlockSpec((nb, KV_ROW, B_WINDOW), lambda a: (a, 0, 0))],
        out_shape=[jax.ShapeDtypeStruct((n, DEC_SEQ, D_MODEL), F32),
                   jax.ShapeDtypeStruct((n, KV_ROW, B_WINDOW), F32)],
        compiler_params=_cparams(("arbitrary",)),
        name="swa_sample_attend",
    )(z, z, win_cache, sinks.reshape(N_HEADS, 1, 1))


def _nsa_col_perm():
    perm = np.full((A_COLS,), -1, np.int64)
    perm[:D_MODEL] = np.arange(D_MODEL)
    kvw = N_KV * HEAD_DIM
    for br, base in enumerate((A_CMP0, A_SLC0, A_WIN0)):
        old0 = D_MODEL + 2 * kvw * br
        for kv in range(N_KV):
            for i in range(2):
                dst = base + kv * 2 * HEAD_DIM + i * HEAD_DIM
                src = old0 + i * kvw + kv * HEAD_DIM
                perm[dst:dst + HEAD_DIM] = np.arange(src, src + HEAD_DIM)
    old_g = D_MODEL + 6 * kvw
    for kv in range(N_KV):
        for br in range(3):
            dst = A_GATE0 + kv * LANES + br * N_G
            src = old_g + br * N_HEADS + kv * N_G
            perm[dst:dst + N_G] = np.arange(src, src + N_G)
    return perm


def _swa_col_perm():
    perm = np.zeros((B_COLS,), np.int64)
    perm[:D_MODEL] = np.arange(D_MODEL)
    kvw = N_KV * HEAD_DIM
    for kv in range(N_KV):
        for i in range(2):
            dst = D_MODEL + kv * 2 * HEAD_DIM + i * HEAD_DIM
            src = D_MODEL + i * kvw + kv * HEAD_DIM
            perm[dst:dst + HEAD_DIM] = np.arange(src, src + HEAD_DIM)
    return perm


def _permute_cols(w, perm):
    parts, start = [], 0
    for j in range(1, len(perm) + 1):
        if j == len(perm) or (perm[j] != perm[j - 1] + 1 if perm[j - 1] >= 0 else perm[j] >= 0):
            if perm[start] < 0:
                parts.append(jnp.zeros(w.shape[:-1] + (j - start,), w.dtype))
            else:
                parts.append(w[..., int(perm[start]):int(perm[start]) + j - start])
            start = j
    return jnp.concatenate(parts, axis=-1)


def _rows_view(z, c0, n, t):
    return z[..., c0:c0 + KV_ROW].reshape(n, t, N_KV, 2, HEAD_DIM)


def _rows_from_t(rows_t):
    n, _, t = rows_t.shape
    return rows_t.reshape(n, N_KV, 2, HEAD_DIM, t).transpose(0, 4, 1, 2, 3)


def _trunk(x, ada, past, p, prompt):
    g, t, _ = x.shape
    tm = min(512, t)
    if prompt:
        pos = jnp.arange(t, dtype=jnp.int32)
        n_seq, t_seq = g, t
    else:
        n_seq, t_seq = t // DEC_SEQ, DEC_SEQ
        pos = jnp.tile(PAST_LEN + jnp.arange(DEC_SEQ, dtype=jnp.int32), n_seq)
    tab = _rope_tables(pos)[None]
    cmp_out, slc_out, nwin_out, swin_out = [], [], [], []
    for l in range(DEPTH):
        sh1, sc1, g1, sh2, sc2, g2 = [(ada, l, k) for k in range(6)]
        i = l // 2
        if l % 2 == 0:
            z, q_rot, *rows_t = _proj(x, sc1, sh1, p['nsa_w_in'][i], p['zero_a'], tab, tm, (A_CMP0,),
                                      (A_SLC0, A_WIN0), True, prompt)
            if prompt:
                cmp_rows, slc_rows, win_rows = [_rows_from_t(r) for r in rows_t]
            else:
                cmp_rows, slc_rows, win_rows = [_rows_view(z, c0, n_seq, t_seq) for c0 in (A_CMP0, A_SLC0, A_WIN0)]
            w1r, pb, w2 = p['cmp_w1r'][i], p['cmp_pb'][i], p['cmp_w2'][i]
            if prompt:
                nch = t // CMP_STRIDE
                chunks = z[..., A_CMP0:A_CMP0 + KV_ROW].reshape(g, nch, CMP_STRIDE, 2 * N_KV, HEAD_DIM)
                chunks = chunks.transpose(0, 3, 1, 2, 4).reshape(g, 2 * N_KV, nch, CHUNK_W)
                cblocks = _compress_prompt(chunks, w1r, pb, w2)
                o = _nsa_prompt_attend(z, q_rot, cblocks)
                nwin_out.append(win_rows[:, -min(A_WINDOW, t):])
            else:
                cache_cmp, cache_slc, nwin_t, _, page_table = past
                zs = z.reshape(n_seq, DEC_SEQ, A_COLS)
                qs = zs[..., :D_MODEL]
                o_cmp, sel = _nsa_sample_cmp(i, page_table, cache_cmp, zs[..., A_CMP0:A_CMP0 + KV_ROW], qs, w1r, pb, w2)
                o, win_next = _nsa_sample_slc(i, page_table, cache_slc, zs, q_rot.reshape(n_seq, DEC_SEQ, D_MODEL), sel,
                                              nwin_t, o_cmp)
                o = o.reshape(g, t, D_MODEL)
                nwin_out.append(_rows_from_t(win_next))
            cmp_out.append(cmp_rows)
            slc_out.append(slc_rows)
            w_o, b_o = p['nsa_w_o'][i], p['zero_d']
        else:
            z, *rows_t = _proj(x, sc1, sh1, p['swa_w_in'][i], p['swa_b_in'][i], tab, tm, (), (D_MODEL,), False, prompt)
            rows = _rows_from_t(rows_t[0]) if prompt else _rows_view(z, D_MODEL, n_seq, t_seq)
            if prompt:
                o = _swa_prompt_attend(z, p['swa_sinks'][i])
                swin_out.append(rows[:, -min(B_WINDOW, t):])
            else:
                o, win_next = _swa_sample_attend(i, z.reshape(n_seq, DEC_SEQ, B_COLS), past[3], p['swa_sinks'][i])
                o = o.reshape(g, t, D_MODEL)
                swin_out.append(_rows_from_t(win_next))
            w_o, b_o = p['swa_w_o'][i], p['swa_b_o'][i]
        x = _outproj_ln(o, x, g1, w_o, b_o, p['ln_g'][l, 0], p['ln_b'][l, 0], tm)
        x = _moe_ln(x, sc2, sh2, g2, p['router_wt'], p['router_b'], l, p['moe_w_gate'], p['moe_w_up'],
                    p['moe_w_down'], p['ln_g'][l, 1], p['ln_b'][l, 1], min(MOE_TILE, t))
    return x, (jnp.stack(cmp_out), jnp.stack(slc_out), jnp.stack(nwin_out), jnp.stack(swin_out))


def kernel(x_prompt, x_sample, cache_nsa_cmp, cache_nsa_slc, cache_nsa_win, cache_swa_win, page_table, c_prompt, c_sample, nsa_w_in, nsa_cmp_pe, nsa_cmp_w1, nsa_cmp_b1, nsa_cmp_w2, nsa_w_o, swa_w_in, swa_b_in, swa_sinks, swa_w_o, swa_b_o, ada_w, ada_b, ln_g, ln_b, router_w, router_b, moe_w_gate, moe_w_up, moe_w_down):
    n_a = nsa_w_in.shape[0]
    batch, seq, _ = x_prompt.shape
    n_dec = x_sample.shape[0]

    pe_term = jnp.stack([_pe_term(nsa_cmp_pe[i], nsa_cmp_w1[i]) for i in range(n_a)])
    w1r = nsa_cmp_w1.reshape(n_a, 2, CMP_SHIFTS, CHUNK_W, CMP_HIDDEN).transpose(0, 1, 3, 2, 4)
    p = {
        'nsa_w_in': _permute_cols(nsa_w_in, _nsa_col_perm()).astype(BF16),
        'zero_a': jnp.zeros((1, A_COLS), F32),
        'zero_d': jnp.zeros((1, D_MODEL), F32),
        'cmp_w1r': w1r.reshape(n_a, 2, CHUNK_W, CMP_SHIFTS * CMP_HIDDEN).astype(BF16),
        'cmp_pb': (pe_term + nsa_cmp_b1).reshape(n_a, 2, 1, CMP_HIDDEN),
        'cmp_w2': nsa_cmp_w2.astype(BF16),
        'nsa_w_o': nsa_w_o.astype(BF16),
        'swa_w_in': _permute_cols(swa_w_in, _swa_col_perm()).astype(BF16),
        'swa_b_in': _permute_cols(swa_b_in, _swa_col_perm())[:, None, :],
        'swa_sinks': swa_sinks,
        'swa_w_o': swa_w_o.astype(BF16),
        'swa_b_o': swa_b_o[:, None, :],
        'ln_g': ln_g[:, :, None, :],
        'ln_b': ln_b[:, :, None, :],
        'router_wt': router_w.T,
        'router_b': router_b[:, None],
        'moe_w_gate': moe_w_gate.astype(BF16),
        'moe_w_up': moe_w_up.astype(BF16),
        'moe_w_down': moe_w_down.astype(BF16),
    }

    n_tok = n_dec * DEC_SEQ
    ada = _ada(jnp.concatenate([jnp.repeat(c_sample, DEC_SEQ, axis=0), c_prompt], axis=0), ada_w, ada_b)
    ada_s = ada[:, None]
    ada_p = ada[:, n_tok:, None, :]

    y_p, (cmp_p, slc_p, nwin_p, swin_p) = _trunk(x_prompt, ada_p, None, p, True)
    def feature_major(c):
        return c.transpose(0, 1, 3, 4, 5, 2).reshape(c.shape[0], c.shape[1], KV_ROW, c.shape[2])
    past = (feature_major(cache_nsa_cmp), feature_major(cache_nsa_slc), feature_major(cache_nsa_win),
            feature_major(cache_swa_win), page_table)
    y_s, (cmp_s, slc_s, nwin_s, swin_s) = _trunk(x_sample.reshape(1, n_dec * DEC_SEQ, D_MODEL), ada_s, past, p, False)
    return (y_p, y_s.reshape(n_dec, DEC_SEQ, D_MODEL), cmp_p, slc_p, nwin_p, swin_p, cmp_s, slc_s, nwin_s, swin_s)
```

```python
import functools
import math

import numpy as np
import jax
import jax.numpy as jnp
from jax import lax
from jax.experimental import pallas as pl
from jax.experimental.pallas import tpu as pltpu

D_MODEL = 1024
SEQ = 2048
DEPTH = 4
DEC_SEQ = 8
PAST_LEN = 8192
PAGE_SIZE = 128
N_PAGES = PAST_LEN // PAGE_SIZE

HEAD_DIM = 64
ROT_DIM = HEAD_DIM // 4
ROT_HALF = ROT_DIM // 2
ROPE_THETA = 500000.0
ATTN_SCALE = HEAD_DIM ** -0.5

N_HEADS = D_MODEL // HEAD_DIM
N_KV = 2
N_G = N_HEADS // N_KV
KV_ROW = N_KV * 2 * HEAD_DIM
CMP_BLOCK = 32
CMP_STRIDE = 16
CMP_HIDDEN = 4 * HEAD_DIM
CMP_SHIFTS = CMP_BLOCK // CMP_STRIDE
CHUNK_W = CMP_STRIDE * HEAD_DIM
SEL_BLOCK = 64
SEL_TOP = 16
CHUNKS_PER_SEL = SEL_BLOCK // CMP_STRIDE
A_WINDOW = 512
B_WINDOW = 128
FORCE_SCORE = 1e9

N_EXPERTS = 16
N_GROUPS = 4
EXPERTS_PER_GROUP = N_EXPERTS // N_GROUPS
D_FF_EXPERT = D_MODEL // 2

DN_ALPHA = (2.0 * DEPTH) ** 0.25
LN_EPS = 1e-5
NEG_INF = -1e30
TINY = 1e-30
LOG2E = math.log2(math.e)

LANES = 128
SUBLANES = 8
ROW_TILE = 512
ATT_Q_TILE = 128
ATT_KEY_CHUNK = 512
MOE_TILE = 1024
MOE_SUB_TILE = 256
SWA_SEQS_PER_STEP = 8
A_COLS = 2048
A_CMP0, A_SLC0, A_WIN0, A_GATE0 = 1024, 1280, 1536, 1792
B_COLS = D_MODEL + KV_ROW
VMEM_LIMIT = 56 * 1024 * 1024

F32 = jnp.float32
BF16 = jnp.bfloat16


def _cparams(sem):
    return pltpu.CompilerParams(dimension_semantics=sem, vmem_limit_bytes=VMEM_LIMIT)


def _dot(a, b):
    return jnp.dot(a, b, preferred_element_type=F32)


def _dot_nt(a, b, precision=None):
    return lax.dot_general(a, b, (((1,), (1,)), ((), ())), precision=precision,
                           preferred_element_type=F32)


def _masked_probs(s, mask):
    s = jnp.where(mask, s, NEG_INF)
    m = jnp.max(s, axis=-1, keepdims=True)
    e = jnp.where(mask, jnp.exp(s - m), 0.0)
    den = jnp.sum(e, axis=-1, keepdims=True)
    return e / jnp.maximum(den, TINY)


def _layer_norm(r, g, b):
    mu = jnp.mean(r, axis=-1, keepdims=True)
    rc = r - mu
    var = jnp.mean(rc * rc, axis=-1, keepdims=True)
    return rc * lax.rsqrt(var + LN_EPS) * g + b


def _ada_kernel(c_ref, w_ref, b_ref, o_ref):
    c = c_ref[...]
    a = (c * jax.nn.sigmoid(c)).astype(BF16)
    o_ref[0] = _dot(a, w_ref[0].astype(BF16)) + b_ref[0]


def _ada(c_all, ada_w, ada_b):
    m = c_all.shape[0]
    n = ada_w.shape[-1]
    tn = 1536
    return pl.pallas_call(
        _ada_kernel,
        grid=(DEPTH, n // tn),
        in_specs=[pl.BlockSpec((m, D_MODEL), lambda l, j: (0, 0)),
                  pl.BlockSpec((1, D_MODEL, tn), lambda l, j: (l, 0, j)),
                  pl.BlockSpec((1, 1, tn), lambda l, j: (l, 0, j))],
        out_specs=pl.BlockSpec((1, m, tn), lambda l, j: (l, 0, j)),
        out_shape=jax.ShapeDtypeStruct((DEPTH, m, n), F32),
        compiler_params=_cparams(("arbitrary", "arbitrary")),
        name="ada",
    )(c_all, ada_w, ada_b.reshape(DEPTH, 1, n))


def _pe_term_kernel(pe_ref, w_ref, o_ref):
    o_ref[0] = _dot(pe_ref[0].astype(BF16), w_ref[0].astype(BF16))


def _pe_term(pe, w1):
    k = CMP_BLOCK * HEAD_DIM
    pe8 = jnp.broadcast_to(pe.reshape(2, 1, k), (2, 8, k))
    out = pl.pallas_call(
        _pe_term_kernel,
        grid=(2,),
        in_specs=[pl.BlockSpec((1, 8, k), lambda i: (i, 0, 0)),
                  pl.BlockSpec((1, k, CMP_HIDDEN), lambda i: (i, 0, 0))],
        out_specs=pl.BlockSpec((1, 8, CMP_HIDDEN), lambda i: (i, 0, 0)),
        out_shape=jax.ShapeDtypeStruct((2, 8, CMP_HIDDEN), F32),
        compiler_params=_cparams(("arbitrary",)),
        name="pe_term",
    )(pe8, w1)
    return out[:, 0]


def _rope_tables(pos):
    inv = jnp.exp(-math.log(ROPE_THETA) * jnp.arange(ROT_HALF, dtype=F32) / ROT_HALF)
    ang = pos.astype(F32)[:, None] * inv[None, :]
    cos, sin = jnp.cos(ang), jnp.sin(ang)
    t = pos.shape[0]
    one = jnp.ones((t, HEAD_DIM - ROT_DIM), F32)
    zero = jnp.zeros((t, HEAD_DIM - ROT_DIM), F32)
    zh = jnp.zeros((t, ROT_HALF), F32)
    cos_h = jnp.concatenate([cos, cos, one], axis=1)
    sa_h = jnp.concatenate([-sin, zh, zero], axis=1)
    sb_h = jnp.concatenate([zh, sin, zero], axis=1)
    id_h = jnp.ones((t, HEAD_DIM), F32)
    z_h = jnp.zeros((t, HEAD_DIM), F32)
    return jnp.concatenate([cos_h, cos_h, sa_h, sa_h, sb_h, sb_h,
                            cos_h, id_h, sa_h, z_h, sb_h, z_h], axis=1)


def _rope_apply(seg, cos, sa, sb):
    w = seg.shape[1]
    reps = w // LANES
    c = jnp.tile(cos, (1, reps))
    a = jnp.tile(sa, (1, reps))
    b = jnp.tile(sb, (1, reps))
    return seg * c + pltpu.roll(seg, w - ROT_HALF, 1) * a + pltpu.roll(seg, ROT_HALF, 1) * b


def _mod_operand(mod, tm, n_grid):
    ada, l, k = mod
    per_token = ada.shape[2] > 1
    if n_grid == 2:
        imap = (lambda a, i: (l, a, i, k)) if per_token else (lambda a, i: (l, a, 0, k))
    else:
        imap = (lambda a, i, e: (l, a, i, k)) if per_token else (lambda a, i, e: (l, a, 0, k))
    return ada, pl.BlockSpec((None, 1, tm if per_token else 1, D_MODEL), imap)


def _proj_kernel(x_ref, sc_ref, sh_ref, w_ref, b_ref, tab_ref, z_ref, *more_refs, plain_segs, kv_segs, sep_qrot,
                 rows_t):
    more_refs = list(more_refs)
    x = x_ref[0]
    h = x * (1.0 + sc_ref[0]) + sh_ref[0]
    z = _dot(h.astype(BF16), w_ref[...]) + b_ref[...]
    tab = tab_ref[0]
    tq = [tab[:, i * LANES:(i + 1) * LANES] for i in range(3)]
    tk = [tab[:, i * LANES:(i + 1) * LANES] for i in range(3, 6)]
    q_rot = _rope_apply(z[:, :D_MODEL], *tq)
    z_ref[0] = z
    if sep_qrot:
        more_refs.pop(0)[0] = q_rot
    else:
        z_ref[0, :, :D_MODEL] = q_rot
    for c0 in plain_segs:
        if rows_t:
            more_refs.pop(0)[0] = z[:, c0:c0 + KV_ROW].T
    for c0 in kv_segs:
        seg = _rope_apply(z[:, c0:c0 + KV_ROW], *tk)
        z_ref[0, :, c0:c0 + KV_ROW] = seg
        if rows_t:
            more_refs.pop(0)[0] = seg.T


def _proj(x, sc, sh, w, b, tab, tm, plain_segs, kv_segs, sep_qrot, rows_t):
    g, t, _ = x.shape
    n = w.shape[1]
    sc, sc_spec = _mod_operand(sc, tm, 2)
    sh, sh_spec = _mod_operand(sh, tm, 2)
    out_shape = [jax.ShapeDtypeStruct((g, t, n), F32)]
    out_specs = [pl.BlockSpec((1, tm, n), lambda a, i: (a, i, 0))]
    if sep_qrot:
        out_shape.append(jax.ShapeDtypeStruct((g, t, D_MODEL), F32))
        out_specs.append(pl.BlockSpec((1, tm, D_MODEL), lambda a, i: (a, i, 0)))
    if rows_t:
        for _ in plain_segs + kv_segs:
            out_shape.append(jax.ShapeDtypeStruct((g, KV_ROW, t), F32))
            out_specs.append(pl.BlockSpec((1, KV_ROW, tm), lambda a, i: (a, 0, i)))
    return pl.pallas_call(
        functools.partial(_proj_kernel, plain_segs=plain_segs, kv_segs=kv_segs, sep_qrot=sep_qrot, rows_t=rows_t),
        grid=(g, t // tm),
        in_specs=[pl.BlockSpec((1, tm, D_MODEL), lambda a, i: (a, i, 0)),
                  sc_spec, sh_spec,
                  pl.BlockSpec((D_MODEL, n), lambda a, i: (0, 0)),
                  pl.BlockSpec((1, n), lambda a, i: (0, 0)),
                  pl.BlockSpec((1, tm, 6 * LANES), lambda a, i: (0, i, 0))],
        out_specs=out_specs,
        out_shape=out_shape,
        compiler_params=_cparams(("arbitrary", "arbitrary")),
        name="proj",
    )(x, sc, sh, w, b, tab)


def _outproj_kernel(o_ref, x_ref, g_ref, w_ref, b_ref, lg_ref, lb_ref, out_ref):
    y = _dot(o_ref[0].astype(BF16), w_ref[...]) + b_ref[...]
    r = DN_ALPHA * x_ref[0] + g_ref[0] * y
    out_ref[0] = _layer_norm(r, lg_ref[...], lb_ref[...])


def _outproj_ln(o, x, gate, w, b, lg, lb, tm):
    g, t, _ = x.shape
    gate, mod_spec = _mod_operand(gate, tm, 2)
    row = pl.BlockSpec((1, tm, D_MODEL), lambda a, i: (a, i, 0))
    vec = pl.BlockSpec((1, D_MODEL), lambda a, i: (0, 0))
    return pl.pallas_call(
        _outproj_kernel,
        grid=(g, t // tm),
        in_specs=[row, row, mod_spec,
                  pl.BlockSpec((D_MODEL, D_MODEL), lambda a, i: (0, 0)), vec, vec, vec],
        out_specs=row,
        out_shape=jax.ShapeDtypeStruct((g, t, D_MODEL), F32),
        compiler_params=_cparams(("arbitrary", "arbitrary")),
        name="outproj_ln",
    )(o, x, gate, w, b, lg, lb)


def _router_gates_t(h, rwt_ref, rb_ref):
    logits = _dot_nt(rwt_ref[...], h, precision=lax.Precision.HIGHEST)
    m = jnp.max(logits, axis=0, keepdims=True)
    ex = jnp.exp(logits - m)
    probs = ex / jnp.sum(ex, axis=0, keepdims=True)
    biased = probs + rb_ref[...]
    eidx = lax.broadcasted_iota(jnp.int32, biased.shape, 0)
    best = None
    g_sel = None
    for g in range(N_GROUPS):
        rows = [biased[g * EXPERTS_PER_GROUP + i:g * EXPERTS_PER_GROUP + i + 1] for i in range(EXPERTS_PER_GROUP)]
        gs = None
        for i in range(EXPERTS_PER_GROUP):
            for j in range(i + 1, EXPERTS_PER_GROUP):
                pair = rows[i] + rows[j]
                gs = pair if gs is None else jnp.maximum(gs, pair)
        if best is None:
            best, g_sel = gs, jnp.zeros(gs.shape, jnp.int32)
        else:
            upd = gs > best
            g_sel = jnp.where(upd, g, g_sel)
            best = jnp.maximum(best, gs)
    v = jnp.where(eidx // EXPERTS_PER_GROUP == g_sel, biased, NEG_INF)
    m1 = jnp.max(v, axis=0, keepdims=True)
    i1 = jnp.min(jnp.where(v == m1, eidx, N_EXPERTS), axis=0, keepdims=True)
    v2 = jnp.where(eidx == i1, -jnp.inf, v)
    m2 = jnp.max(v2, axis=0, keepdims=True)
    i2 = jnp.min(jnp.where(v2 == m2, eidx, N_EXPERTS), axis=0, keepdims=True)
    w1 = jnp.sum(jnp.where(eidx == i1, probs, 0.0), axis=0, keepdims=True)
    w2 = jnp.sum(jnp.where(eidx == i2, probs, 0.0), axis=0, keepdims=True)
    tot = w1 + w2
    return jnp.where(eidx == i1, w1 / tot, 0.0) + jnp.where(eidx == i2, w2 / tot, 0.0), g_sel


def _moe_kernel(x_ref, sc_ref, sh_ref, g_ref, rwt_ref, rb_ref, earlier_ref, wg_ref, wu_ref, wd_ref, lg_ref, lb_ref,
                out_ref, hs_scr, gate_scr, ys_scr, unperm_scr, seg_smem, *, sub):
    e = pl.program_id(2)
    tm = hs_scr.shape[0]

    @pl.when(e == 0)
    def _():
        h = x_ref[0] * (1.0 + sc_ref[0]) + sh_ref[0]
        gates_t, g_sel = _router_gates_t(h, rwt_ref, rb_ref)
        grp = lax.broadcasted_iota(jnp.int32, (SUBLANES, tm), 0)
        onehot = jnp.where(grp == g_sel, 1.0, 0.0)
        src = lax.broadcasted_iota(jnp.int32, (tm, tm), 0)
        cum = _dot(onehot.astype(BF16), earlier_ref[...])
        pos = jnp.zeros((1, tm), F32)
        off = jnp.int32(0)
        for r in range(N_GROUPS):
            cnt = jnp.sum(onehot[r:r + 1, :]).astype(jnp.int32)
            seg_smem[r] = off
            seg_smem[N_GROUPS + r] = cnt
            pos = pos + onehot[r:r + 1, :] * (cum[r:r + 1, :] + off.astype(F32))
            off = off + cnt
        perm = jnp.where(src == pos.astype(jnp.int32), 1.0, 0.0)
        hs_scr[...] = _dot(perm.astype(BF16), h.astype(BF16)).astype(BF16)
        gs_t = _dot_nt(gates_t, perm, precision=lax.Precision.HIGHEST)
        pad = jnp.zeros((LANES - N_EXPERTS, tm), F32)
        gate_scr[...] = jnp.concatenate([gs_t, pad], axis=0).T
        unperm_scr[...] = perm.T.astype(BF16)
        ys_scr[...] = jnp.zeros_like(ys_scr)

    grp_e = e // EXPERTS_PER_GROUP
    off = seg_smem[grp_e]
    cnt = seg_smem[N_GROUPS + grp_e]
    first = off // sub
    last = (off + cnt - 1) // sub
    for j in range(tm // sub):
        @pl.when((cnt > 0) & (first <= j) & (j <= last))
        def _():
            rows = hs_scr[j * sub:(j + 1) * sub, :]
            a = _dot(rows, wg_ref[0])
            u = _dot(rows, wu_ref[0])
            gates = gate_scr[j * sub:(j + 1) * sub, :]
            lane = lax.broadcasted_iota(jnp.int32, gates.shape, 1)
            gcol = jnp.sum(jnp.where(lane == e, gates, 0.0), axis=1, keepdims=True)
            act = (a * jax.nn.sigmoid(a)) * u * gcol
            ys_scr[j * sub:(j + 1) * sub, :] += _dot(act.astype(BF16), wd_ref[0])

    @pl.when(e == N_EXPERTS - 1)
    def _():
        ys = ys_scr[...]
        hi = ys.astype(BF16)
        lo = (ys - hi.astype(F32)).astype(BF16)
        unperm = unperm_scr[...]
        y = _dot(unperm, hi) + _dot(unperm, lo)
        r = DN_ALPHA * x_ref[0] + g_ref[0] * y
        out_ref[0] = _layer_norm(r, lg_ref[...], lb_ref[...])


def _moe_ln(x, sc, sh, gate, rwt, rb, layer, wg, wu, wd, lg, lb, tm):
    g, t, _ = x.shape
    sub = min(MOE_SUB_TILE, tm)
    sc, sc_spec = _mod_operand(sc, tm, 3)
    sh, sh_spec = _mod_operand(sh, tm, 3)
    gate, gate_spec = _mod_operand(gate, tm, 3)
    earlier = (jnp.arange(tm)[:, None] < jnp.arange(tm)[None, :]).astype(BF16)
    row = pl.BlockSpec((1, tm, D_MODEL), lambda a, i, e: (a, i, 0))
    vec = pl.BlockSpec((1, D_MODEL), lambda a, i, e: (0, 0))
    return pl.pallas_call(
        functools.partial(_moe_kernel, sub=sub),
        grid=(g, t // tm, N_EXPERTS),
        in_specs=[row, sc_spec, sh_spec, gate_spec,
                  pl.BlockSpec((N_EXPERTS, D_MODEL), lambda a, i, e: (0, 0)),
                  pl.BlockSpec((N_EXPERTS, 1), lambda a, i, e: (0, 0)),
                  pl.BlockSpec((tm, tm), lambda a, i, e: (0, 0)),
                  pl.BlockSpec((None, 1, D_MODEL, D_FF_EXPERT), lambda a, i, e: (layer, e, 0, 0)),
                  pl.BlockSpec((None, 1, D_MODEL, D_FF_EXPERT), lambda a, i, e: (layer, e, 0, 0)),
                  pl.BlockSpec((None, 1, D_FF_EXPERT, D_MODEL), lambda a, i, e: (layer, e, 0, 0)),
                  vec, vec],
        out_specs=row,
        out_shape=jax.ShapeDtypeStruct((g, t, D_MODEL), F32),
        scratch_shapes=[pltpu.VMEM((tm, D_MODEL), BF16),
                        pltpu.VMEM((tm, LANES), F32),
                        pltpu.VMEM((tm, D_MODEL), F32),
                        pltpu.VMEM((tm, tm), BF16),
                        pltpu.SMEM((2 * N_GROUPS,), jnp.int32)],
        compiler_params=_cparams(("arbitrary", "arbitrary", "arbitrary")),
        name="moe_ln",
    )(x, sc, sh, gate, rwt, rb, earlier, wg, wu, wd, lg, lb)


def _compress_blocks(chunks, w1r, pb, w2):
    nch = chunks.shape[0]
    part = _dot(chunks, w1r)
    pre = part[:, :CMP_HIDDEN] + pltpu.roll(part[:, CMP_HIDDEN:], nch - 1, 0)
    hid = jax.nn.gelu(pre + pb)
    return _dot(hid.astype(BF16), w2)


def _compress_kernel(ch_ref, w1r_ref, pb_ref, w2_ref, o_ref):
    for j in range(2 * N_KV):
        i = j % 2
        o_ref[0, j] = _compress_blocks(ch_ref[0, j].astype(BF16), w1r_ref[i], pb_ref[i], w2_ref[i])


def _compress_prompt(chunks, w1r, pb, w2):
    b, _, nch, _ = chunks.shape
    return pl.pallas_call(
        _compress_kernel,
        grid=(b,),
        in_specs=[pl.BlockSpec((1, 2 * N_KV, nch, CHUNK_W), lambda a: (a, 0, 0, 0)),
                  pl.BlockSpec((2, CHUNK_W, 2 * CMP_HIDDEN), lambda a: (0, 0, 0)),
                  pl.BlockSpec((2, 1, CMP_HIDDEN), lambda a: (0, 0, 0)),
                  pl.BlockSpec((2, CMP_HIDDEN, HEAD_DIM), lambda a: (0, 0, 0))],
        out_specs=pl.BlockSpec((1, 2 * N_KV, nch, HEAD_DIM), lambda a: (a, 0, 0, 0)),
        out_shape=jax.ShapeDtypeStruct((b, 2 * N_KV, nch, HEAD_DIM), F32),
        compiler_params=_cparams(("arbitrary",)),
        name="compress_prompt",
    )(chunks, w1r, pb, w2)


def _stack_heads(q):
    return jnp.concatenate([q[:, g * HEAD_DIM:(g + 1) * HEAD_DIM] for g in range(N_G)], axis=0)


def _chunk_score_matrix_t(n_sel, n_cmp):
    j = lax.broadcasted_iota(jnp.int32, (n_sel, n_cmp), 0)
    c = lax.broadcasted_iota(jnp.int32, (n_sel, n_cmp), 1)
    lo = j * CHUNKS_PER_SEL
    hi = lo + CHUNKS_PER_SEL - 1
    m = jnp.zeros((n_sel, n_cmp), F32)
    for r in range(CMP_SHIFTS):
        m = m + jnp.where((c + r >= lo) & (c + r <= hi), 1.0, 0.0)
    return m


def _nsa_prompt_kernel(q_ref, qr_ref, cb_ref, slc_ref, win_ref, gate_ref, o_ref, m_scr, acc_scr, *, tq, kc):
    qi = pl.program_id(2)
    t0 = qi * tq
    n_cmp = cb_ref.shape[2]
    n_sel = SEQ // SEL_BLOCK
    tpos = t0 + lax.broadcasted_iota(jnp.int32, (tq, 1), 0)

    qs = _stack_heads(q_ref[0]).astype(BF16)
    k_c = cb_ref[0, 0].astype(BF16)
    v_c = cb_ref[0, 1].astype(BF16)
    s = (_dot_nt(qs, k_c) * ATTN_SCALE).reshape(N_G, tq, n_cmp)
    cend = lax.broadcasted_iota(jnp.int32, (tq, n_cmp), 1) * CMP_STRIDE + CMP_BLOCK - 1
    p = _masked_probs(s, (cend <= tpos)[None])
    o_cmp = _dot(p.reshape(N_G * tq, n_cmp).astype(BF16), v_c)
    imp = jnp.sum(p, axis=0)

    score_t = _dot_nt(_chunk_score_matrix_t(n_sel, n_cmp), imp, precision=lax.Precision.HIGHEST)
    jj = lax.broadcasted_iota(jnp.int32, (n_sel, tq), 0)
    tt = t0 + lax.broadcasted_iota(jnp.int32, (n_sel, tq), 1)
    cur = tt // SEL_BLOCK
    forced = (jj == 0) | (jj == cur) | (jj == cur - 1)
    sc = jnp.where(jj * SEL_BLOCK <= tt, jnp.where(forced, FORCE_SCORE, score_t), NEG_INF)
    rank = jnp.zeros((n_sel, tq), F32)
    for i in range(n_sel):
        ri = sc[i:i + 1]
        rank = rank + jnp.where(ri > sc, 1.0, jnp.where((ri == sc) & (jj > i), 1.0, 0.0))
    sel_t = jnp.where(rank < SEL_TOP, 1.0, 0.0)
    sel = jnp.concatenate([sel_t, jnp.zeros((LANES - n_sel, tq), F32)], axis=0).T.astype(BF16)

    qrs = (_stack_heads(qr_ref[0]) * (ATTN_SCALE * LOG2E)).astype(BF16)

    m_scr[...] = jnp.full(m_scr.shape, NEG_INF, F32)
    acc_scr[...] = jnp.zeros(acc_scr.shape, F32)

    def body(c, carry):
        k0 = pl.multiple_of(c * kc, kc)
        rows = slc_ref[0, pl.ds(k0, kc), :]
        k = rows[:, :HEAD_DIM].astype(BF16)
        sk = _dot_nt(qrs, k).reshape(N_G, tq, kc)
        kpos = k0 + lax.broadcasted_iota(jnp.int32, (LANES, kc), 1)
        expand = jnp.where(kpos // SEL_BLOCK == lax.broadcasted_iota(jnp.int32, (LANES, kc), 0), 1.0, 0.0)
        chosen = _dot(sel, expand.astype(BF16))
        sm = sk + jnp.where((chosen > 0.5) & (kpos[:1] <= tpos), 0.0, NEG_INF)[None]
        m = m_scr[...]
        m_new = jnp.maximum(m, jnp.max(sm, axis=-1, keepdims=True))
        e = jnp.exp2(sm - m_new)
        pv = _dot(e.reshape(N_G * tq, kc).astype(BF16), _values_and_ones(rows)).reshape(N_G, tq, LANES)
        acc_scr[...] = jnp.exp2(m - m_new) * acc_scr[...] + pv
        m_scr[...] = m_new
        return carry

    lax.fori_loop(0, (t0 + tq + kc - 1) // kc, body, 0)
    acc = acc_scr[...]
    o_slc = acc[:, :, :HEAD_DIM] / jnp.maximum(acc[:, :, HEAD_DIM:HEAD_DIM + 1], TINY)

    o_win = _band_attend(qrs, win_ref, t0, tq, A_WINDOW, None)

    gt = jax.nn.sigmoid(gate_ref[0])
    o_cmp = o_cmp.reshape(N_G, tq, HEAD_DIM)
    for g in range(N_G):
        og = (gt[:, g:g + 1] * o_cmp[g] + gt[:, N_G + g:N_G + g + 1] * o_slc[g]
              + gt[:, 2 * N_G + g:2 * N_G + g + 1] * o_win[g])
        o_ref[0, :, g * HEAD_DIM:(g + 1) * HEAD_DIM] = og


def _band_attend(qrs, rows_ref, t0, tq, window, sinks):
    span = window + tq
    start = pl.multiple_of(jnp.maximum(t0 - window, 0), tq)
    rows = rows_ref[0, pl.ds(start, span), :]
    k = rows[:, :HEAD_DIM].astype(BF16)
    d = (t0 + lax.broadcasted_iota(jnp.int32, (tq, span), 0)) - (start + lax.broadcasted_iota(jnp.int32, (tq, span), 1))
    s = _dot_nt(qrs, k).reshape(N_G, tq, span) + jnp.where((d >= 0) & (d <= window), 0.0, NEG_INF)[None]
    m = jnp.max(s, axis=-1, keepdims=True)
    if sinks is not None:
        sink = jnp.concatenate([jnp.full((1, tq, 1), sk, F32) for sk in sinks], axis=0)
        m = jnp.maximum(m, sink)
    e = jnp.exp2(s - m)
    den = jnp.sum(e, axis=-1, keepdims=True)
    if sinks is not None:
        den = den + jnp.exp2(sink - m)
    o = _dot(e.reshape(N_G * tq, span).astype(BF16), rows[:, HEAD_DIM:].astype(BF16)).reshape(N_G, tq, HEAD_DIM)
    return o / jnp.maximum(den, TINY)


def _values_and_ones(rows):
    lane = lax.broadcasted_iota(jnp.int32, rows.shape, 1)
    return jnp.where(lane < HEAD_DIM, pltpu.roll(rows, HEAD_DIM, 1), 1.0).astype(BF16)


def _nsa_prompt_attend(z, q_rot, cblocks, tq=ATT_Q_TILE, kc=ATT_KEY_CHUNK):
    b, t, _ = z.shape
    n_cmp = cblocks.shape[2]
    hw = N_G * HEAD_DIM
    blk = lambda base: pl.BlockSpec((1, t, LANES), lambda a, k, i: (a, 0, base // LANES + k))
    return pl.pallas_call(
        functools.partial(_nsa_prompt_kernel, tq=tq, kc=kc),
        grid=(b, N_KV, t // tq),
        in_specs=[pl.BlockSpec((1, tq, hw), lambda a, k, i: (a, i, k)),
                  pl.BlockSpec((1, tq, hw), lambda a, k, i: (a, i, k)),
                  pl.BlockSpec((1, 2, n_cmp, HEAD_DIM), lambda a, k, i: (a, k, 0, 0)),
                  blk(A_SLC0), blk(A_WIN0),
                  pl.BlockSpec((1, tq, LANES), lambda a, k, i: (a, i, A_GATE0 // LANES + k))],
        out_specs=pl.BlockSpec((1, tq, hw), lambda a, k, i: (a, i, k)),
        out_shape=jax.ShapeDtypeStruct((b, t, D_MODEL), F32),
        scratch_shapes=[pltpu.VMEM((N_G, tq, 1), F32), pltpu.VMEM((N_G, tq, LANES), F32)],
        compiler_params=_cparams(("arbitrary", "arbitrary", "arbitrary")),
        name="nsa_prompt_attend",
    )(z, q_rot, cblocks, z, z, z)


def _swa_prompt_kernel(sink_ref, q_ref, rows_ref, o_ref, *, tq):
    kv = pl.program_id(1)
    t0 = pl.program_id(2) * tq
    qrs = (_stack_heads(q_ref[0]) * (ATTN_SCALE * LOG2E)).astype(BF16)
    sinks = [sink_ref[kv, g] * LOG2E for g in range(N_G)]
    o = _band_attend(qrs, rows_ref, t0, tq, B_WINDOW, sinks)
    for g in range(N_G):
        o_ref[0, :, g * HEAD_DIM:(g + 1) * HEAD_DIM] = o[g]


def _swa_prompt_attend(z, sinks, tq=ATT_Q_TILE):
    b, t, _ = z.shape
    hw = N_G * HEAD_DIM
    return pl.pallas_call(
        functools.partial(_swa_prompt_kernel, tq=tq),
        grid=(b, N_KV, t // tq),
        in_specs=[pl.BlockSpec(memory_space=pltpu.SMEM),
                  pl.BlockSpec((1, tq, hw), lambda a, k, i: (a, i, k)),
                  pl.BlockSpec((1, t, LANES), lambda a, k, i: (a, 0, D_MODEL // LANES + k))],
        out_specs=pl.BlockSpec((1, tq, hw), lambda a, k, i: (a, i, k)),
        out_shape=jax.ShapeDtypeStruct((b, t, D_MODEL), F32),
        compiler_params=_cparams(("arbitrary", "arbitrary", "arbitrary")),
        name="swa_prompt_attend",
    )(sinks.reshape(N_KV, N_G), z, z)


S_CHUNKS = 520
S_SEL = 256
N_SEL_SAMPLE = PAST_LEN // SEL_BLOCK + 1
SLABS = CMP_STRIDE // SUBLANES
PAGES_PER_ITER = 8
assert N_SEL_SAMPLE == LANES + 1 and DEC_SEQ == SUBLANES and N_PAGES % PAGES_PER_ITER == 0


def _page_copy(cache_hbm, ia, pt_ref, n, p, buf, slot, sem):
    return pltpu.make_async_copy(cache_hbm.at[ia, pt_ref[n, p]],
                                 buf.at[slot, :, pl.ds(pl.multiple_of(p * PAGE_SIZE, PAGE_SIZE), PAGE_SIZE)],
                                 sem.at[slot])


def _gather_pages(cache_hbm, ia, pt_ref, buf, sem):
    n = pl.program_id(0)
    nn = pl.num_programs(0)
    slot = n % 2

    def start(seq, sl):
        def go(p, c):
            _page_copy(cache_hbm, ia, pt_ref, seq, p, buf, sl, sem).start()
            return c
        lax.fori_loop(0, N_PAGES, go, 0)

    @pl.when(n == 0)
    def _():
        start(0, 0)

    @pl.when(n + 1 < nn)
    def _():
        start(n + 1, 1 - slot)

    def wait(p, c):
        _page_copy(cache_hbm, ia, pt_ref, n, p, buf, slot, sem).wait()
        return c
    lax.fori_loop(0, N_PAGES, wait, 0)
    return slot


def _sample_q_rows(q, kv):
    return jnp.concatenate([q[:, (kv * N_G + g) * HEAD_DIM:(kv * N_G + g + 1) * HEAD_DIM] for g in range(N_G)], axis=0)


def _sample_q_blockdiag(q):
    parts = []
    for kv in range(N_KV):
        qs = _sample_q_rows(q, kv)
        left = jnp.zeros((qs.shape[0], kv * 2 * HEAD_DIM), F32)
        right = jnp.zeros((qs.shape[0], KV_ROW - kv * 2 * HEAD_DIM - HEAD_DIM), F32)
        parts.append(jnp.concatenate([left, qs, right], axis=1) if kv else jnp.concatenate([qs, right], axis=1))
    return jnp.concatenate(parts, axis=0)


def _nsa_sample_cmp_kernel(pt_ref, new_ref, q_ref, w1r_ref, pb_ref, w2_ref, smat_ref, cache_hbm,
                           ocmp_ref, sel_ref, pages, rows, chunks, sem, *, ia):
    slot = _gather_pages(cache_hbm, ia, pt_ref, pages, sem)

    def to_rows(it, c):
        for u in range(PAGES_PER_ITER):
            p = it * PAGES_PER_ITER + u
            r0 = pl.multiple_of(p * PAGE_SIZE, PAGE_SIZE)
            c0 = pl.multiple_of(p * (PAGE_SIZE // SLABS), PAGE_SIZE // SLABS)
            for kv in range(N_KV):
                t = pages[slot, kv * LANES:(kv + 1) * LANES, pl.ds(r0, PAGE_SIZE)].T
                for v in range(PAGE_SIZE // SUBLANES):
                    rows[kv, v % SLABS, pl.ds(c0 + (v // SLABS) * SUBLANES, SUBLANES), :] = t[v * SUBLANES:(v + 1) * SUBLANES]
        return c
    lax.fori_loop(0, N_PAGES // PAGES_PER_ITER, to_rows, 0)
    new_rows = new_ref[0]
    tail = (PAST_LEN // CMP_STRIDE) * SUBLANES
    for kv in range(N_KV):
        rows[kv, 0, tail:tail + DEC_SEQ, :] = new_rows[:, kv * LANES:(kv + 1) * LANES]
        rows[kv, 0, tail + DEC_SEQ:, :] = jnp.zeros((S_CHUNKS * SUBLANES - tail - DEC_SEQ, LANES), F32)
        for slab in range(1, SLABS):
            rows[kv, slab, tail:, :] = jnp.zeros((S_CHUNKS * SUBLANES - tail, LANES), F32)

    lo = lax.broadcasted_iota(jnp.int32, (S_CHUNKS, LANES), 1) < HEAD_DIM
    for k in range(CMP_STRIDE // 2):
        for kv in range(N_KV):
            sa, sb = 2 * k, 2 * k + 1
            a = rows[kv, sa // SUBLANES, pl.ds(sa % SUBLANES, S_CHUNKS, stride=SUBLANES), :]
            b = rows[kv, sb // SUBLANES, pl.ds(sb % SUBLANES, S_CHUNKS, stride=SUBLANES), :]
            chunks[2 * kv, :, k * LANES:(k + 1) * LANES] = jnp.where(lo, a, pltpu.roll(b, HEAD_DIM, 1)).astype(BF16)
            chunks[2 * kv + 1, :, k * LANES:(k + 1) * LANES] = jnp.where(lo, pltpu.roll(a, HEAD_DIM, 1), b).astype(BF16)

    q = q_ref[0]
    qpos = PAST_LEN + lax.broadcasted_iota(jnp.int32, (DEC_SEQ, 1), 0)
    cend = lax.broadcasted_iota(jnp.int32, (DEC_SEQ, S_CHUNKS), 1) * CMP_STRIDE + CMP_BLOCK - 1
    cmask = (cend <= qpos)[None]
    imps = []
    for kv in range(N_KV):
        k_c = _compress_blocks(chunks[2 * kv], w1r_ref[0], pb_ref[0], w2_ref[0]).astype(BF16)
        v_c = _compress_blocks(chunks[2 * kv + 1], w1r_ref[1], pb_ref[1], w2_ref[1]).astype(BF16)
        qs = _sample_q_rows(q, kv).astype(BF16)
        s = (_dot_nt(qs, k_c) * ATTN_SCALE).reshape(N_G, DEC_SEQ, S_CHUNKS)
        p = _masked_probs(s, cmask)
        o = _dot(p.reshape(N_G * DEC_SEQ, S_CHUNKS).astype(BF16), v_c)
        for g in range(N_G):
            h = kv * N_G + g
            ocmp_ref[0, :, h * HEAD_DIM:(h + 1) * HEAD_DIM] = o[g * DEC_SEQ:(g + 1) * DEC_SEQ]
        imps.append(jnp.sum(p, axis=0))
    imp = jnp.concatenate(imps, axis=0)

    score = _dot_nt(imp, smat_ref[...], precision=lax.Precision.HIGHEST)
    shape = score.shape
    jj = lax.broadcasted_iota(jnp.int32, shape, 1)
    tt = PAST_LEN + lax.broadcasted_iota(jnp.int32, shape, 0) % DEC_SEQ
    cur = tt // SEL_BLOCK
    forced = (jj == 0) | (jj == cur) | (jj == cur - 1)
    sc = jnp.where(jj * SEL_BLOCK <= tt, jnp.where(forced, FORCE_SCORE, score), NEG_INF)

    sc_a = sc[:, :LANES]
    extra = jnp.sum(jnp.where(jj == LANES, sc, 0.0), axis=1, keepdims=True)
    lane = lax.broadcasted_iota(jnp.int32, sc_a.shape, 1)
    rank = jnp.where(extra > sc_a, 1.0, 0.0)
    for k in range(1, LANES):
        r = pltpu.roll(sc_a, k, 1)
        rank = rank + jnp.where(r > sc_a, 1.0, jnp.where(r == sc_a, jnp.where(lane >= k, 1.0, 0.0), 0.0))
    rank_x = jnp.sum(jnp.where(sc_a >= extra, 1.0, 0.0), axis=1, keepdims=True)
    sel_ref[0, :, :LANES] = jnp.where(rank < SEL_TOP, 1.0, 0.0)
    sel_ref[0, :, LANES:] = jnp.where((lane == 0) & (rank_x < SEL_TOP), 1.0, 0.0)


def _nsa_sample_cmp(ia, page_table, cache, new_rows, q, w1r, pb, w2):
    n = q.shape[0]
    grid_spec = pltpu.PrefetchScalarGridSpec(
        num_scalar_prefetch=1,
        grid=(n,),
        in_specs=[pl.BlockSpec((1, DEC_SEQ, KV_ROW), lambda a, pt: (a, 0, 0)),
                  pl.BlockSpec((1, DEC_SEQ, D_MODEL), lambda a, pt: (a, 0, 0)),
                  pl.BlockSpec((2, CHUNK_W, 2 * CMP_HIDDEN), lambda a, pt: (0, 0, 0)),
                  pl.BlockSpec((2, 1, CMP_HIDDEN), lambda a, pt: (0, 0, 0)),
                  pl.BlockSpec((2, CMP_HIDDEN, HEAD_DIM), lambda a, pt: (0, 0, 0)),
                  pl.BlockSpec((S_SEL, S_CHUNKS), lambda a, pt: (0, 0)),
                  pl.BlockSpec(memory_space=pl.ANY)],
        out_specs=[pl.BlockSpec((1, DEC_SEQ, D_MODEL), lambda a, pt: (a, 0, 0)),
                   pl.BlockSpec((1, N_KV * DEC_SEQ, S_SEL), lambda a, pt: (a, 0, 0))],
        scratch_shapes=[pltpu.VMEM((2, KV_ROW, PAST_LEN), F32),
                        pltpu.VMEM((N_KV, SLABS, S_CHUNKS * SUBLANES, LANES), F32),
                        pltpu.VMEM((2 * N_KV, S_CHUNKS, CHUNK_W), BF16),
                        pltpu.SemaphoreType.DMA((2,))])
    return pl.pallas_call(
        functools.partial(_nsa_sample_cmp_kernel, ia=ia),
        grid_spec=grid_spec,
        out_shape=[jax.ShapeDtypeStruct((n, DEC_SEQ, D_MODEL), F32),
                   jax.ShapeDtypeStruct((n, N_KV * DEC_SEQ, S_SEL), F32)],
        compiler_params=_cparams(("arbitrary",)),
        name="nsa_sample_cmp",
    )(page_table, new_rows, q, w1r, pb, w2, _chunk_score_matrix_t(S_SEL, S_CHUNKS), cache)


def _window_cached_attend(qbd, wt, new, window, sink):
    nq = qbd.shape[0]
    heads = nq // DEC_SEQ
    qpos = PAST_LEN + lax.broadcasted_iota(jnp.int32, (DEC_SEQ, 1), 0)
    d_p = qpos - (PAST_LEN - window + lax.broadcasted_iota(jnp.int32, (DEC_SEQ, window), 1))
    d_n = qpos - (PAST_LEN + lax.broadcasted_iota(jnp.int32, (DEC_SEQ, DEC_SEQ), 1))
    s_p = _dot(qbd, wt).reshape(heads, DEC_SEQ, window) + jnp.where((d_p >= 0) & (d_p <= window), 0.0, NEG_INF)[None]
    s_n = _dot_nt(qbd, new).reshape(heads, DEC_SEQ, DEC_SEQ) + jnp.where((d_n >= 0) & (d_n <= window), 0.0, NEG_INF)[None]
    m = jnp.maximum(jnp.max(s_p, axis=-1, keepdims=True), jnp.max(s_n, axis=-1, keepdims=True))
    if sink is not None:
        m = jnp.maximum(m, sink)
    e_p = jnp.exp(s_p - m)
    e_n = jnp.exp(s_n - m)
    den = jnp.sum(e_p, axis=-1, keepdims=True) + jnp.sum(e_n, axis=-1, keepdims=True)
    if sink is not None:
        den = den + jnp.exp(sink - m)
    o = _dot_nt(e_p.reshape(nq, window).astype(BF16), wt) + _dot(e_n.reshape(nq, DEC_SEQ).astype(BF16), new)
    return o / jnp.maximum(den.reshape(nq, 1), TINY)


def _advance_window(wt, new):
    w = wt.shape[1]
    shifted = pltpu.roll(wt, w - DEC_SEQ, 1)
    new_t = jnp.concatenate([new, jnp.zeros((LANES - DEC_SEQ, KV_ROW), F32)], axis=0).T
    new_t = pltpu.roll(new_t, LANES - DEC_SEQ, 1)
    lane = lax.broadcasted_iota(jnp.int32, (KV_ROW, LANES), 1)
    tail = jnp.where(lane >= LANES - DEC_SEQ, new_t, shifted[:, w - LANES:])
    return tail if w == LANES else jnp.concatenate([shifted[:, :w - LANES], tail], axis=1)


def _nsa_sample_slc_kernel(pt_ref, new_ref, qr_ref, sel_ref, wbuf_ref, wnew_ref, gate_ref, ocmp_ref, expand_ref,
                           cache_hbm, o_ref, wout_ref, pages, pages_bf, sem, *, ia):
    slot = _gather_pages(cache_hbm, ia, pt_ref, pages, sem)
    wout_ref[0] = _advance_window(wbuf_ref[0], wnew_ref[0])
    pages_bf[...] = pages[slot].astype(BF16)

    nq = N_KV * N_G * DEC_SEQ
    qbd = (_sample_q_blockdiag(qr_ref[0]) * ATTN_SCALE).astype(BF16)
    qpos = PAST_LEN + lax.broadcasted_iota(jnp.int32, (DEC_SEQ, 1), 0)

    past = pages_bf[...]
    new = new_ref[0].astype(BF16)
    s_p = _dot(qbd, past).reshape(N_KV, N_G, DEC_SEQ, PAST_LEN)
    s_n = _dot_nt(qbd, new).reshape(N_KV, N_G, DEC_SEQ, DEC_SEQ)
    sel = sel_ref[0]
    sel_bf = sel.astype(BF16)
    chosen = _dot(sel_bf, expand_ref[...]).reshape(N_KV, 1, DEC_SEQ, PAST_LEN)
    bias_p = jnp.where(chosen > 0.5, 0.0, NEG_INF)
    new_blk = sel[:, PAST_LEN // SEL_BLOCK:PAST_LEN // SEL_BLOCK + 1].reshape(N_KV, 1, DEC_SEQ, 1)
    npos = PAST_LEN + lax.broadcasted_iota(jnp.int32, (DEC_SEQ, DEC_SEQ), 1)
    bias_n = jnp.where((new_blk > 0.5) & (npos <= qpos)[None, None], 0.0, NEG_INF)
    s_p = s_p + bias_p
    s_n = s_n + bias_n
    m = jnp.maximum(jnp.max(s_p, axis=-1, keepdims=True), jnp.max(s_n, axis=-1, keepdims=True))
    e_p = jnp.exp(s_p - m)
    e_n = jnp.exp(s_n - m)
    den = jnp.sum(e_p, axis=-1, keepdims=True) + jnp.sum(e_n, axis=-1, keepdims=True)
    o_slc = (_dot_nt(e_p.reshape(nq, PAST_LEN).astype(BF16), past)
             + _dot(e_n.reshape(nq, DEC_SEQ).astype(BF16), new)) / jnp.maximum(den.reshape(nq, 1), TINY)

    o_win = _window_cached_attend(qbd, wbuf_ref[0].astype(BF16), wnew_ref[0].astype(BF16), A_WINDOW, None)

    gt = jax.nn.sigmoid(gate_ref[0])
    o_cmp = ocmp_ref[0]
    for kv in range(N_KV):
        v0 = kv * 2 * HEAD_DIM + HEAD_DIM
        for g in range(N_G):
            h = kv * N_G + g
            r0 = h * DEC_SEQ
            c0 = kv * LANES + g
            og = (gt[:, c0:c0 + 1] * o_cmp[:, h * HEAD_DIM:(h + 1) * HEAD_DIM]
                  + gt[:, c0 + N_G:c0 + N_G + 1] * o_slc[r0:r0 + DEC_SEQ, v0:v0 + HEAD_DIM]
                  + gt[:, c0 + 2 * N_G:c0 + 2 * N_G + 1] * o_win[r0:r0 + DEC_SEQ, v0:v0 + HEAD_DIM])
            o_ref[0, :, h * HEAD_DIM:(h + 1) * HEAD_DIM] = og


def _nsa_sample_slc(ia, page_table, cache, z, q_rot, sel, win_cache, o_cmp):
    n = z.shape[0]
    expand = (jnp.arange(PAST_LEN)[None, :] // SEL_BLOCK == jnp.arange(S_SEL)[:, None]).astype(BF16)
    row_blk = lambda base: pl.BlockSpec((1, DEC_SEQ, KV_ROW), lambda a, pt: (a, 0, base // KV_ROW))
    grid_spec = pltpu.PrefetchScalarGridSpec(
        num_scalar_prefetch=1,
        grid=(n,),
        in_specs=[row_blk(A_SLC0),
                  pl.BlockSpec((1, DEC_SEQ, D_MODEL), lambda a, pt: (a, 0, 0)),
                  pl.BlockSpec((1, N_KV * DEC_SEQ, S_SEL), lambda a, pt: (a, 0, 0)),
                  pl.BlockSpec((None, 1, KV_ROW, A_WINDOW), lambda a, pt: (ia, a, 0, 0)),
                  row_blk(A_WIN0), row_blk(A_GATE0),
                  pl.BlockSpec((1, DEC_SEQ, D_MODEL), lambda a, pt: (a, 0, 0)),
                  pl.BlockSpec((S_SEL, PAST_LEN), lambda a, pt: (0, 0)),
                  pl.BlockSpec(memory_space=pl.ANY)],
        out_specs=[pl.BlockSpec((1, DEC_SEQ, D_MODEL), lambda a, pt: (a, 0, 0)),
                   pl.BlockSpec((1, KV_ROW, A_WINDOW), lambda a, pt: (a, 0, 0))],
        scratch_shapes=[pltpu.VMEM((2, KV_ROW, PAST_LEN), F32),
                        pltpu.VMEM((KV_ROW, PAST_LEN), BF16),
                        pltpu.SemaphoreType.DMA((2,))])
    return pl.pallas_call(
        functools.partial(_nsa_sample_slc_kernel, ia=ia),
        grid_spec=grid_spec,
        out_shape=[jax.ShapeDtypeStruct((n, DEC_SEQ, D_MODEL), F32),
                   jax.ShapeDtypeStruct((n, KV_ROW, A_WINDOW), F32)],
        compiler_params=_cparams(("arbitrary",)),
        name="nsa_sample_slc",
    )(page_table, z, q_rot, sel, win_cache, z, z, o_cmp, expand, cache)


def _swa_sample_kernel(q_ref, new_ref, wbuf_ref, sink_ref, o_ref, wout_ref):
    def one(i, c):
        new = new_ref[i]
        qbd = (_sample_q_blockdiag(q_ref[i]) * ATTN_SCALE).astype(BF16)
        o = _window_cached_attend(qbd, wbuf_ref[i].astype(BF16), new.astype(BF16), B_WINDOW, sink_ref[...])
        for kv in range(N_KV):
            v0 = kv * 2 * HEAD_DIM + HEAD_DIM
            for g in range(N_G):
                h = kv * N_G + g
                o_ref[i, :, h * HEAD_DIM:(h + 1) * HEAD_DIM] = o[h * DEC_SEQ:(h + 1) * DEC_SEQ, v0:v0 + HEAD_DIM]
        wout_ref[i] = _advance_window(wbuf_ref[i], new)
        return c
    lax.fori_loop(0, q_ref.shape[0], one, 0)


def _swa_sample_attend(ib, z, win_cache, sinks):
    n = z.shape[0]
    nb = math.gcd(n, SWA_SEQS_PER_STEP)
    return pl.pallas_call(
        _swa_sample_kernel,
        grid=(n // nb,),
        in_specs=[pl.BlockSpec((nb, DEC_SEQ, D_MODEL), lambda a: (a, 0, 0)),
                  pl.BlockSpec((nb, DEC_SEQ, KV_ROW), lambda a: (a, 0, D_MODEL // KV_ROW)),
                  pl.BlockSpec((None, nb, KV_ROW, B_WINDOW), lambda a: (ib, a, 0, 0)),
                  pl.BlockSpec((N_HEADS, 1, 1), lambda a: (0, 0, 0))],
        out_specs=[pl.BlockSpec((nb, DEC_SEQ, D_MODEL), lambda a: (a, 0, 0)),
                   pl.BlockSpec((nb, KV_ROW, B_WINDOW), lambda a: (a, 0, 0))],
        out_shape=[jax.ShapeDtypeStruct((n, DEC_SEQ, D_MODEL), F32),
                   jax.ShapeDtypeStruct((n, KV_ROW, B_WINDOW), F32)],
        compiler_params=_cparams(("arbitrary",)),
        name="swa_sample_attend",
    )(z, z, win_cache, sinks.reshape(N_HEADS, 1, 1))


def _nsa_col_perm():
    perm = np.full((A_COLS,), -1, np.int64)
    perm[:D_MODEL] = np.arange(D_MODEL)
    kvw = N_KV * HEAD_DIM
    for br, base in enumerate((A_CMP0, A_SLC0, A_WIN0)):
        old0 = D_MODEL + 2 * kvw * br
        for kv in range(N_KV):
            for i in range(2):
                dst = base + kv * 2 * HEAD_DIM + i * HEAD_DIM
                src = old0 + i * kvw + kv * HEAD_DIM
                perm[dst:dst + HEAD_DIM] = np.arange(src, src + HEAD_DIM)
    old_g = D_MODEL + 6 * kvw
    for kv in range(N_KV):
        for br in range(3):
            dst = A_GATE0 + kv * LANES + br * N_G
            src = old_g + br * N_HEADS + kv * N_G
            perm[dst:dst + N_G] = np.arange(src, src + N_G)
    return perm


def _swa_col_perm():
    perm = np.zeros((B_COLS,), np.int64)
    perm[:D_MODEL] = np.arange(D_MODEL)
    kvw = N_KV * HEAD_DIM
    for kv in range(N_KV):
        for i in range(2):
            dst = D_MODEL + kv * 2 * HEAD_DIM + i * HEAD_DIM
            src = D_MODEL + i * kvw + kv * HEAD_DIM
            perm[dst:dst + HEAD_DIM] = np.arange(src, src + HEAD_DIM)
    return perm


def _permute_cols(w, perm):
    parts, start = [], 0
    for j in range(1, len(perm) + 1):
        if j == len(perm) or (perm[j] != perm[j - 1] + 1 if perm[j - 1] >= 0 else perm[j] >= 0):
            if perm[start] < 0:
                parts.append(jnp.zeros(w.shape[:-1] + (j - start,), w.dtype))
            else:
                parts.append(w[..., int(perm[start]):int(perm[start]) + j - start])
            start = j
    return jnp.concatenate(parts, axis=-1)


def _rows_view(z, c0, n, t):
    return z[..., c0:c0 + KV_ROW].reshape(n, t, N_KV, 2, HEAD_DIM)


def _rows_from_t(rows_t):
    n, _, t = rows_t.shape
    return rows_t.reshape(n, N_KV, 2, HEAD_DIM, t).transpose(0, 4, 1, 2, 3)


def _trunk(x, ada, past, p, prompt):
    g, t, _ = x.shape
    tm = min(ROW_TILE, t)
    if prompt:
        pos = jnp.arange(t, dtype=jnp.int32)
        n_seq, t_seq = g, t
    else:
        n_seq, t_seq = t // DEC_SEQ, DEC_SEQ
        pos = jnp.tile(PAST_LEN + jnp.arange(DEC_SEQ, dtype=jnp.int32), n_seq)
    tab = _rope_tables(pos)[None]
    cmp_out, slc_out, nwin_out, swin_out = [], [], [], []
    for l in range(DEPTH):
        sh1, sc1, g1, sh2, sc2, g2 = [(ada, l, k) for k in range(6)]
        i = l // 2
        if l % 2 == 0:
            z, q_rot, *rows_t = _proj(x, sc1, sh1, p['nsa_w_in'][i], p['zero_a'], tab, tm, (A_CMP0,),
                                      (A_SLC0, A_WIN0), True, prompt)
            if prompt:
                cmp_rows, slc_rows, win_rows = [_rows_from_t(r) for r in rows_t]
            else:
                cmp_rows, slc_rows, win_rows = [_rows_view(z, c0, n_seq, t_seq) for c0 in (A_CMP0, A_SLC0, A_WIN0)]
            w1r, pb, w2 = p['cmp_w1r'][i], p['cmp_pb'][i], p['cmp_w2'][i]
            if prompt:
                nch = t // CMP_STRIDE
                chunks = z[..., A_CMP0:A_CMP0 + KV_ROW].reshape(g, nch, CMP_STRIDE, 2 * N_KV, HEAD_DIM)
                chunks = chunks.transpose(0, 3, 1, 2, 4).reshape(g, 2 * N_KV, nch, CHUNK_W)
                cblocks = _compress_prompt(chunks, w1r, pb, w2)
                o = _nsa_prompt_attend(z, q_rot, cblocks)
                nwin_out.append(win_rows[:, -min(A_WINDOW, t):])
            else:
                cache_cmp, cache_slc, nwin_t, _, page_table = past
                zs = z.reshape(n_seq, DEC_SEQ, A_COLS)
                qs = zs[..., :D_MODEL]
                o_cmp, sel = _nsa_sample_cmp(i, page_table, cache_cmp, zs[..., A_CMP0:A_CMP0 + KV_ROW], qs, w1r, pb, w2)
                o, win_next = _nsa_sample_slc(i, page_table, cache_slc, zs, q_rot.reshape(n_seq, DEC_SEQ, D_MODEL), sel,
                                              nwin_t, o_cmp)
                o = o.reshape(g, t, D_MODEL)
                nwin_out.append(_rows_from_t(win_next))
            cmp_out.append(cmp_rows)
            slc_out.append(slc_rows)
            w_o, b_o = p['nsa_w_o'][i], p['zero_d']
        else:
            z, *rows_t = _proj(x, sc1, sh1, p['swa_w_in'][i], p['swa_b_in'][i], tab, tm, (), (D_MODEL,), False, prompt)
            rows = _rows_from_t(rows_t[0]) if prompt else _rows_view(z, D_MODEL, n_seq, t_seq)
            if prompt:
                o = _swa_prompt_attend(z, p['swa_sinks'][i])
                swin_out.append(rows[:, -min(B_WINDOW, t):])
            else:
                o, win_next = _swa_sample_attend(i, z.reshape(n_seq, DEC_SEQ, B_COLS), past[3], p['swa_sinks'][i])
                o = o.reshape(g, t, D_MODEL)
                swin_out.append(_rows_from_t(win_next))
            w_o, b_o = p['swa_w_o'][i], p['swa_b_o'][i]
        x = _outproj_ln(o, x, g1, w_o, b_o, p['ln_g'][l, 0], p['ln_b'][l, 0], tm)
        x = _moe_ln(x, sc2, sh2, g2, p['router_wt'], p['router_b'], l, p['moe_w_gate'], p['moe_w_up'],
                    p['moe_w_down'], p['ln_g'][l, 1], p['ln_b'][l, 1], min(MOE_TILE, t))
    return x, (jnp.stack(cmp_out), jnp.stack(slc_out), jnp.stack(nwin_out), jnp.stack(swin_out))


def kernel(x_prompt, x_sample, cache_nsa_cmp, cache_nsa_slc, cache_nsa_win, cache_swa_win, page_table, c_prompt, c_sample, nsa_w_in, nsa_cmp_pe, nsa_cmp_w1, nsa_cmp_b1, nsa_cmp_w2, nsa_w_o, swa_w_in, swa_b_in, swa_sinks, swa_w_o, swa_b_o, ada_w, ada_b, ln_g, ln_b, router_w, router_b, moe_w_gate, moe_w_up, moe_w_down):
    n_a = nsa_w_in.shape[0]
    batch, seq, _ = x_prompt.shape
    n_dec = x_sample.shape[0]

    pe_term = jnp.stack([_pe_term(nsa_cmp_pe[i], nsa_cmp_w1[i]) for i in range(n_a)])
    w1r = nsa_cmp_w1.reshape(n_a, 2, CMP_SHIFTS, CHUNK_W, CMP_HIDDEN).transpose(0, 1, 3, 2, 4)
    p = {
        'nsa_w_in': _permute_cols(nsa_w_in, _nsa_col_perm()).astype(BF16),
        'zero_a': jnp.zeros((1, A_COLS), F32),
        'zero_d': jnp.zeros((1, D_MODEL), F32),
        'cmp_w1r': w1r.reshape(n_a, 2, CHUNK_W, CMP_SHIFTS * CMP_HIDDEN).astype(BF16),
        'cmp_pb': (pe_term + nsa_cmp_b1).reshape(n_a, 2, 1, CMP_HIDDEN),
        'cmp_w2': nsa_cmp_w2.astype(BF16),
        'nsa_w_o': nsa_w_o.astype(BF16),
        'swa_w_in': _permute_cols(swa_w_in, _swa_col_perm()).astype(BF16),
        'swa_b_in': _permute_cols(swa_b_in, _swa_col_perm())[:, None, :],
        'swa_sinks': swa_sinks,
        'swa_w_o': swa_w_o.astype(BF16),
        'swa_b_o': swa_b_o[:, None, :],
        'ln_g': ln_g[:, :, None, :],
        'ln_b': ln_b[:, :, None, :],
        'router_wt': router_w.T,
        'router_b': router_b[:, None],
        'moe_w_gate': moe_w_gate.astype(BF16),
        'moe_w_up': moe_w_up.astype(BF16),
        'moe_w_down': moe_w_down.astype(BF16),
    }

    n_tok = n_dec * DEC_SEQ
    ada = _ada(jnp.concatenate([jnp.repeat(c_sample, DEC_SEQ, axis=0), c_prompt], axis=0), ada_w, ada_b)
    ada_s = ada[:, None]
    ada_p = ada[:, n_tok:, None, :]

    y_p, (cmp_p, slc_p, nwin_p, swin_p) = _trunk(x_prompt, ada_p, None, p, True)
    def feature_major(c):
        return c.transpose(0, 1, 3, 4, 5, 2).reshape(c.shape[0], c.shape[1], KV_ROW, c.shape[2])
    past = (feature_major(cache_nsa_cmp), feature_major(cache_nsa_slc), feature_major(cache_nsa_win),
            feature_major(cache_swa_win), page_table)
    y_s, (cmp_s, slc_s, nwin_s, swin_s) = _trunk(x_sample.reshape(1, n_dec * DEC_SEQ, D_MODEL), ada_s, past, p, False)
    return (y_p, y_s.reshape(n_dec, DEC_SEQ, D_MODEL), cmp_p, slc_p, nwin_p, swin_p, cmp_s, slc_s, nwin_s, swin_s)
```

```python
import functools
import math

import numpy as np
import jax
import jax.numpy as jnp
from jax import lax
from jax.experimental import pallas as pl
from jax.experimental.pallas import tpu as pltpu

D_MODEL = 1024
SEQ = 2048
DEPTH = 4
DEC_SEQ = 8
PAST_LEN = 8192
PAGE_SIZE = 128
N_PAGES = PAST_LEN // PAGE_SIZE

HEAD_DIM = 64
ROT_DIM = HEAD_DIM // 4
ROT_HALF = ROT_DIM // 2
ROPE_THETA = 500000.0
ATTN_SCALE = HEAD_DIM ** -0.5

N_HEADS = D_MODEL // HEAD_DIM
N_KV = 2
N_G = N_HEADS // N_KV
KV_ROW = N_KV * 2 * HEAD_DIM
CMP_BLOCK = 32
CMP_STRIDE = 16
CMP_HIDDEN = 4 * HEAD_DIM
CMP_SHIFTS = CMP_BLOCK // CMP_STRIDE
CHUNK_W = CMP_STRIDE * HEAD_DIM
SEL_BLOCK = 64
SEL_TOP = 16
CHUNKS_PER_SEL = SEL_BLOCK // CMP_STRIDE
A_WINDOW = 512
B_WINDOW = 128
FORCE_SCORE = 1e9

N_EXPERTS = 16
N_GROUPS = 4
EXPERTS_PER_GROUP = N_EXPERTS // N_GROUPS
D_FF_EXPERT = D_MODEL // 2

DN_ALPHA = (2.0 * DEPTH) ** 0.25
LN_EPS = 1e-5
NEG_INF = -1e30
TINY = 1e-30
LOG2E = math.log2(math.e)

LANES = 128
SUBLANES = 8
ROW_TILE = 512
ATT_Q_TILE = 128
ATT_KEY_CHUNK = 512
MOE_TILE = 1024
MOE_SUB_TILE = 256
SWA_SEQS_PER_STEP = 8
A_COLS = 2048
A_CMP0, A_SLC0, A_WIN0, A_GATE0 = 1024, 1280, 1536, 1792
B_COLS = D_MODEL + KV_ROW
VMEM_LIMIT = 56 * 1024 * 1024

F32 = jnp.float32
BF16 = jnp.bfloat16


def _cparams(sem):
    return pltpu.CompilerParams(dimension_semantics=sem, vmem_limit_bytes=VMEM_LIMIT)


def _dot(a, b):
    return jnp.dot(a, b, preferred_element_type=F32)


def _dot_nt(a, b, precision=None):
    return lax.dot_general(a, b, (((1,), (1,)), ((), ())), precision=precision,
                           preferred_element_type=F32)


def _split_bf16(x, terms):
    parts = []
    for _ in range(terms):
        p = x.astype(BF16)
        parts.append(p)
        x = x - p.astype(F32)
    return parts


def _masked_probs(s, mask):
    s = jnp.where(mask, s, NEG_INF)
    m = jnp.max(s, axis=-1, keepdims=True)
    e = jnp.where(mask, jnp.exp(s - m), 0.0)
    den = jnp.sum(e, axis=-1, keepdims=True)
    return e / jnp.maximum(den, TINY)


def _layer_norm(r, g, b):
    mu = jnp.mean(r, axis=-1, keepdims=True)
    rc = r - mu
    var = jnp.mean(rc * rc, axis=-1, keepdims=True)
    return rc * lax.rsqrt(var + LN_EPS) * g + b


def _ada_kernel(c_ref, w_ref, b_ref, o_ref):
    c = c_ref[...]
    a = (c * jax.nn.sigmoid(c)).astype(BF16)
    o_ref[0] = _dot(a, w_ref[0].astype(BF16)) + b_ref[0]


def _ada(c_all, ada_w, ada_b):
    m = c_all.shape[0]
    n = ada_w.shape[-1]
    tn = 1536
    return pl.pallas_call(
        _ada_kernel,
        grid=(DEPTH, n // tn),
        in_specs=[pl.BlockSpec((m, D_MODEL), lambda l, j: (0, 0)),
                  pl.BlockSpec((1, D_MODEL, tn), lambda l, j: (l, 0, j)),
                  pl.BlockSpec((1, 1, tn), lambda l, j: (l, 0, j))],
        out_specs=pl.BlockSpec((1, m, tn), lambda l, j: (l, 0, j)),
        out_shape=jax.ShapeDtypeStruct((DEPTH, m, n), F32),
        compiler_params=_cparams(("arbitrary", "arbitrary")),
        name="ada",
    )(c_all, ada_w, ada_b.reshape(DEPTH, 1, n))


def _pe_term_kernel(pe_ref, w_ref, o_ref):
    o_ref[0] = _dot(pe_ref[0].astype(BF16), w_ref[0].astype(BF16))


def _pe_term(pe, w1):
    k = CMP_BLOCK * HEAD_DIM
    pe8 = jnp.broadcast_to(pe.reshape(2, 1, k), (2, 8, k))
    out = pl.pallas_call(
        _pe_term_kernel,
        grid=(2,),
        in_specs=[pl.BlockSpec((1, 8, k), lambda i: (i, 0, 0)),
                  pl.BlockSpec((1, k, CMP_HIDDEN), lambda i: (i, 0, 0))],
        out_specs=pl.BlockSpec((1, 8, CMP_HIDDEN), lambda i: (i, 0, 0)),
        out_shape=jax.ShapeDtypeStruct((2, 8, CMP_HIDDEN), F32),
        compiler_params=_cparams(("arbitrary",)),
        name="pe_term",
    )(pe8, w1)
    return out[:, 0]


def _rope_tables(pos):
    inv = jnp.exp(-math.log(ROPE_THETA) * jnp.arange(ROT_HALF, dtype=F32) / ROT_HALF)
    ang = pos.astype(F32)[:, None] * inv[None, :]
    cos, sin = jnp.cos(ang), jnp.sin(ang)
    t = pos.shape[0]
    one = jnp.ones((t, HEAD_DIM - ROT_DIM), F32)
    zero = jnp.zeros((t, HEAD_DIM - ROT_DIM), F32)
    zh = jnp.zeros((t, ROT_HALF), F32)
    cos_h = jnp.concatenate([cos, cos, one], axis=1)
    sa_h = jnp.concatenate([-sin, zh, zero], axis=1)
    sb_h = jnp.concatenate([zh, sin, zero], axis=1)
    id_h = jnp.ones((t, HEAD_DIM), F32)
    z_h = jnp.zeros((t, HEAD_DIM), F32)
    return jnp.concatenate([cos_h, cos_h, sa_h, sa_h, sb_h, sb_h,
                            cos_h, id_h, sa_h, z_h, sb_h, z_h], axis=1)


def _rope_apply(seg, cos, sa, sb):
    w = seg.shape[1]
    reps = w // LANES
    c = jnp.tile(cos, (1, reps))
    a = jnp.tile(sa, (1, reps))
    b = jnp.tile(sb, (1, reps))
    return seg * c + pltpu.roll(seg, w - ROT_HALF, 1) * a + pltpu.roll(seg, ROT_HALF, 1) * b


def _mod_operand(mod, tm, n_grid):
    ada, l, k = mod
    per_token = ada.shape[2] > 1
    if n_grid == 2:
        imap = (lambda a, i: (l, a, i, k)) if per_token else (lambda a, i: (l, a, 0, k))
    else:
        imap = (lambda a, i, e: (l, a, i, k)) if per_token else (lambda a, i, e: (l, a, 0, k))
    return ada, pl.BlockSpec((None, 1, tm if per_token else 1, D_MODEL), imap)


def _proj_kernel(x_ref, sc_ref, sh_ref, w_ref, b_ref, tab_ref, z_ref, *more_refs, plain_segs, kv_segs, sep_qrot,
                 rows_t):
    more_refs = list(more_refs)
    x = x_ref[0]
    h = x * (1.0 + sc_ref[0]) + sh_ref[0]
    z = _dot(h.astype(BF16), w_ref[...]) + b_ref[...]
    tab = tab_ref[0]
    tq = [tab[:, i * LANES:(i + 1) * LANES] for i in range(3)]
    tk = [tab[:, i * LANES:(i + 1) * LANES] for i in range(3, 6)]
    q_rot = _rope_apply(z[:, :D_MODEL], *tq)
    z_ref[0] = z
    if sep_qrot:
        more_refs.pop(0)[0] = q_rot
    else:
        z_ref[0, :, :D_MODEL] = q_rot
    for c0 in plain_segs:
        if rows_t:
            more_refs.pop(0)[0] = z[:, c0:c0 + KV_ROW].T
    for c0 in kv_segs:
        seg = _rope_apply(z[:, c0:c0 + KV_ROW], *tk)
        z_ref[0, :, c0:c0 + KV_ROW] = seg
        if rows_t:
            more_refs.pop(0)[0] = seg.T


def _proj(x, sc, sh, w, b, tab, tm, plain_segs, kv_segs, sep_qrot, rows_t):
    g, t, _ = x.shape
    n = w.shape[1]
    sc, sc_spec = _mod_operand(sc, tm, 2)
    sh, sh_spec = _mod_operand(sh, tm, 2)
    out_shape = [jax.ShapeDtypeStruct((g, t, n), F32)]
    out_specs = [pl.BlockSpec((1, tm, n), lambda a, i: (a, i, 0))]
    if sep_qrot:
        out_shape.append(jax.ShapeDtypeStruct((g, t, D_MODEL), F32))
        out_specs.append(pl.BlockSpec((1, tm, D_MODEL), lambda a, i: (a, i, 0)))
    if rows_t:
        for _ in plain_segs + kv_segs:
            out_shape.append(jax.ShapeDtypeStruct((g, KV_ROW, t), F32))
            out_specs.append(pl.BlockSpec((1, KV_ROW, tm), lambda a, i: (a, 0, i)))
    return pl.pallas_call(
        functools.partial(_proj_kernel, plain_segs=plain_segs, kv_segs=kv_segs, sep_qrot=sep_qrot, rows_t=rows_t),
        grid=(g, t // tm),
        in_specs=[pl.BlockSpec((1, tm, D_MODEL), lambda a, i: (a, i, 0)),
                  sc_spec, sh_spec,
                  pl.BlockSpec((D_MODEL, n), lambda a, i: (0, 0)),
                  pl.BlockSpec((1, n), lambda a, i: (0, 0)),
                  pl.BlockSpec((1, tm, 6 * LANES), lambda a, i: (0, i, 0))],
        out_specs=out_specs,
        out_shape=out_shape,
        compiler_params=_cparams(("arbitrary", "arbitrary")),
        name="proj",
    )(x, sc, sh, w, b, tab)


def _outproj_kernel(o_ref, x_ref, g_ref, w_ref, b_ref, lg_ref, lb_ref, out_ref):
    y = _dot(o_ref[0].astype(BF16), w_ref[...]) + b_ref[...]
    r = DN_ALPHA * x_ref[0] + g_ref[0] * y
    out_ref[0] = _layer_norm(r, lg_ref[...], lb_ref[...])


def _outproj_ln(o, x, gate, w, b, lg, lb, tm):
    g, t, _ = x.shape
    gate, mod_spec = _mod_operand(gate, tm, 2)
    row = pl.BlockSpec((1, tm, D_MODEL), lambda a, i: (a, i, 0))
    vec = pl.BlockSpec((1, D_MODEL), lambda a, i: (0, 0))
    return pl.pallas_call(
        _outproj_kernel,
        grid=(g, t // tm),
        in_specs=[row, row, mod_spec,
                  pl.BlockSpec((D_MODEL, D_MODEL), lambda a, i: (0, 0)), vec, vec, vec],
        out_specs=row,
        out_shape=jax.ShapeDtypeStruct((g, t, D_MODEL), F32),
        compiler_params=_cparams(("arbitrary", "arbitrary")),
        name="outproj_ln",
    )(o, x, gate, w, b, lg, lb)


def _router_gates_t(h, rwt_ref, rb_ref):
    w_hi, w_lo = _split_bf16(rwt_ref[...], 2)
    h_hi, h_lo = _split_bf16(h, 2)
    logits = _dot_nt(w_hi, h_hi) + _dot_nt(w_hi, h_lo) + _dot_nt(w_lo, h_hi)
    m = jnp.max(logits, axis=0, keepdims=True)
    ex = jnp.exp(logits - m)
    probs = ex / jnp.sum(ex, axis=0, keepdims=True)
    biased = probs + rb_ref[...]
    eidx = lax.broadcasted_iota(jnp.int32, biased.shape, 0)
    best = None
    g_sel = None
    for g in range(N_GROUPS):
        rows = [biased[g * EXPERTS_PER_GROUP + i:g * EXPERTS_PER_GROUP + i + 1] for i in range(EXPERTS_PER_GROUP)]
        gs = None
        for i in range(EXPERTS_PER_GROUP):
            for j in range(i + 1, EXPERTS_PER_GROUP):
                pair = rows[i] + rows[j]
                gs = pair if gs is None else jnp.maximum(gs, pair)
        if best is None:
            best, g_sel = gs, jnp.zeros(gs.shape, jnp.int32)
        else:
            upd = gs > best
            g_sel = jnp.where(upd, g, g_sel)
            best = jnp.maximum(best, gs)
    v = jnp.where(eidx // EXPERTS_PER_GROUP == g_sel, biased, NEG_INF)
    m1 = jnp.max(v, axis=0, keepdims=True)
    i1 = jnp.min(jnp.where(v == m1, eidx, N_EXPERTS), axis=0, keepdims=True)
    v2 = jnp.where(eidx == i1, -jnp.inf, v)
    m2 = jnp.max(v2, axis=0, keepdims=True)
    i2 = jnp.min(jnp.where(v2 == m2, eidx, N_EXPERTS), axis=0, keepdims=True)
    w1 = jnp.sum(jnp.where(eidx == i1, probs, 0.0), axis=0, keepdims=True)
    w2 = jnp.sum(jnp.where(eidx == i2, probs, 0.0), axis=0, keepdims=True)
    tot = w1 + w2
    return jnp.where(eidx == i1, w1 / tot, 0.0) + jnp.where(eidx == i2, w2 / tot, 0.0), g_sel


def _moe_kernel(x_ref, sc_ref, sh_ref, g_ref, rwt_ref, rb_ref, earlier_ref, wg_ref, wu_ref, wd_ref, lg_ref, lb_ref,
                out_ref, hs_scr, gate_scr, ys_scr, unperm_scr, seg_smem, *, sub):
    e = pl.program_id(2)
    tm = hs_scr.shape[0]

    @pl.when(e == 0)
    def _():
        h = x_ref[0] * (1.0 + sc_ref[0]) + sh_ref[0]
        gates_t, g_sel = _router_gates_t(h, rwt_ref, rb_ref)
        grp = lax.broadcasted_iota(jnp.int32, (SUBLANES, tm), 0)
        onehot = jnp.where(grp == g_sel, 1.0, 0.0)
        src = lax.broadcasted_iota(jnp.int32, (tm, tm), 0)
        cum = _dot(onehot.astype(BF16), earlier_ref[...])
        pos = jnp.zeros((1, tm), F32)
        off = jnp.int32(0)
        for r in range(N_GROUPS):
            cnt = jnp.sum(onehot[r:r + 1, :]).astype(jnp.int32)
            seg_smem[r] = off
            seg_smem[N_GROUPS + r] = cnt
            pos = pos + onehot[r:r + 1, :] * (cum[r:r + 1, :] + off.astype(F32))
            off = off + cnt
        perm = jnp.where(src == pos.astype(jnp.int32), 1.0, 0.0)
        hs_scr[...] = _dot(perm.astype(BF16), h.astype(BF16)).astype(BF16)
        perm_bf = perm.astype(BF16)
        gs_t = sum(_dot_nt(piece, perm_bf) for piece in _split_bf16(gates_t, 3))
        pad = jnp.zeros((LANES - N_EXPERTS, tm), F32)
        gate_scr[...] = jnp.concatenate([gs_t, pad], axis=0).T
        unperm_scr[...] = perm.T.astype(BF16)
        ys_scr[...] = jnp.zeros_like(ys_scr)

    grp_e = e // EXPERTS_PER_GROUP
    off = seg_smem[grp_e]
    cnt = seg_smem[N_GROUPS + grp_e]
    first = off // sub
    last = (off + cnt - 1) // sub
    for j in range(tm // sub):
        @pl.when((cnt > 0) & (first <= j) & (j <= last))
        def _():
            rows = hs_scr[j * sub:(j + 1) * sub, :]
            a = _dot(rows, wg_ref[0])
            u = _dot(rows, wu_ref[0])
            gates = gate_scr[j * sub:(j + 1) * sub, :]
            lane = lax.broadcasted_iota(jnp.int32, gates.shape, 1)
            gcol = jnp.sum(jnp.where(lane == e, gates, 0.0), axis=1, keepdims=True)
            act = (a * jax.nn.sigmoid(a)) * u * gcol
            ys_scr[j * sub:(j + 1) * sub, :] += _dot(act.astype(BF16), wd_ref[0])

    @pl.when(e == N_EXPERTS - 1)
    def _():
        ys = ys_scr[...]
        hi = ys.astype(BF16)
        lo = (ys - hi.astype(F32)).astype(BF16)
        unperm = unperm_scr[...]
        y = _dot(unperm, hi) + _dot(unperm, lo)
        r = DN_ALPHA * x_ref[0] + g_ref[0] * y
        out_ref[0] = _layer_norm(r, lg_ref[...], lb_ref[...])


def _moe_ln(x, sc, sh, gate, rwt, rb, layer, wg, wu, wd, lg, lb, tm):
    g, t, _ = x.shape
    sub = min(MOE_SUB_TILE, tm)
    sc, sc_spec = _mod_operand(sc, tm, 3)
    sh, sh_spec = _mod_operand(sh, tm, 3)
    gate, gate_spec = _mod_operand(gate, tm, 3)
    earlier = (jnp.arange(tm)[:, None] < jnp.arange(tm)[None, :]).astype(BF16)
    row = pl.BlockSpec((1, tm, D_MODEL), lambda a, i, e: (a, i, 0))
    vec = pl.BlockSpec((1, D_MODEL), lambda a, i, e: (0, 0))
    return pl.pallas_call(
        functools.partial(_moe_kernel, sub=sub),
        grid=(g, t // tm, N_EXPERTS),
        in_specs=[row, sc_spec, sh_spec, gate_spec,
                  pl.BlockSpec((N_EXPERTS, D_MODEL), lambda a, i, e: (0, 0)),
                  pl.BlockSpec((N_EXPERTS, 1), lambda a, i, e: (0, 0)),
                  pl.BlockSpec((tm, tm), lambda a, i, e: (0, 0)),
                  pl.BlockSpec((None, 1, D_MODEL, D_FF_EXPERT), lambda a, i, e: (layer, e, 0, 0)),
                  pl.BlockSpec((None, 1, D_MODEL, D_FF_EXPERT), lambda a, i, e: (layer, e, 0, 0)),
                  pl.BlockSpec((None, 1, D_FF_EXPERT, D_MODEL), lambda a, i, e: (layer, e, 0, 0)),
                  vec, vec],
        out_specs=row,
        out_shape=jax.ShapeDtypeStruct((g, t, D_MODEL), F32),
        scratch_shapes=[pltpu.VMEM((tm, D_MODEL), BF16),
                        pltpu.VMEM((tm, LANES), F32),
                        pltpu.VMEM((tm, D_MODEL), F32),
                        pltpu.VMEM((tm, tm), BF16),
                        pltpu.SMEM((2 * N_GROUPS,), jnp.int32)],
        compiler_params=_cparams(("arbitrary", "arbitrary", "arbitrary")),
        name="moe_ln",
    )(x, sc, sh, gate, rwt, rb, earlier, wg, wu, wd, lg, lb)


def _compress_blocks(chunks, w1r, pb, w2):
    nch = chunks.shape[0]
    part = _dot(chunks, w1r)
    pre = part[:, :CMP_HIDDEN] + pltpu.roll(part[:, CMP_HIDDEN:], nch - 1, 0)
    hid = jax.nn.gelu(pre + pb)
    return _dot(hid.astype(BF16), w2)


def _compress_kernel(ch_ref, w1r_ref, pb_ref, w2_ref, o_ref):
    for j in range(2 * N_KV):
        i = j % 2
        o_ref[0, j] = _compress_blocks(ch_ref[0, j].astype(BF16), w1r_ref[i], pb_ref[i], w2_ref[i])


def _compress_prompt(chunks, w1r, pb, w2):
    b, _, nch, _ = chunks.shape
    return pl.pallas_call(
        _compress_kernel,
        grid=(b,),
        in_specs=[pl.BlockSpec((1, 2 * N_KV, nch, CHUNK_W), lambda a: (a, 0, 0, 0)),
                  pl.BlockSpec((2, CHUNK_W, 2 * CMP_HIDDEN), lambda a: (0, 0, 0)),
                  pl.BlockSpec((2, 1, CMP_HIDDEN), lambda a: (0, 0, 0)),
                  pl.BlockSpec((2, CMP_HIDDEN, HEAD_DIM), lambda a: (0, 0, 0))],
        out_specs=pl.BlockSpec((1, 2 * N_KV, nch, HEAD_DIM), lambda a: (a, 0, 0, 0)),
        out_shape=jax.ShapeDtypeStruct((b, 2 * N_KV, nch, HEAD_DIM), F32),
        compiler_params=_cparams(("arbitrary",)),
        name="compress_prompt",
    )(chunks, w1r, pb, w2)


def _stack_heads(q):
    return jnp.concatenate([q[:, g * HEAD_DIM:(g + 1) * HEAD_DIM] for g in range(N_G)], axis=0)


def _chunk_score_matrix_t(n_sel, n_cmp):
    j = lax.broadcasted_iota(jnp.int32, (n_sel, n_cmp), 0)
    c = lax.broadcasted_iota(jnp.int32, (n_sel, n_cmp), 1)
    lo = j * CHUNKS_PER_SEL
    hi = lo + CHUNKS_PER_SEL - 1
    m = jnp.zeros((n_sel, n_cmp), F32)
    for r in range(CMP_SHIFTS):
        m = m + jnp.where((c + r >= lo) & (c + r <= hi), 1.0, 0.0)
    return m


def _nsa_prompt_kernel(q_ref, qr_ref, cb_ref, slc_ref, win_ref, gate_ref, o_ref, m_scr, acc_scr, *, tq, kc):
    qi = pl.program_id(2)
    t0 = qi * tq
    n_cmp = cb_ref.shape[2]
    n_sel = SEQ // SEL_BLOCK
    tpos = t0 + lax.broadcasted_iota(jnp.int32, (tq, 1), 0)

    qs = _stack_heads(q_ref[0]).astype(BF16)
    k_c = cb_ref[0, 0].astype(BF16)
    v_c = cb_ref[0, 1].astype(BF16)
    s = (_dot_nt(qs, k_c) * ATTN_SCALE).reshape(N_G, tq, n_cmp)
    cend = lax.broadcasted_iota(jnp.int32, (tq, n_cmp), 1) * CMP_STRIDE + CMP_BLOCK - 1
    p = _masked_probs(s, (cend <= tpos)[None])
    o_cmp = _dot(p.reshape(N_G * tq, n_cmp).astype(BF16), v_c)
    imp = jnp.sum(p, axis=0)

    score_t = _dot_nt(_chunk_score_matrix_t(n_sel, n_cmp), imp, precision=lax.Precision.HIGHEST)
    jj = lax.broadcasted_iota(jnp.int32, (n_sel, tq), 0)
    tt = t0 + lax.broadcasted_iota(jnp.int32, (n_sel, tq), 1)
    cur = tt // SEL_BLOCK
    forced = (jj == 0) | (jj == cur) | (jj == cur - 1)
    sc = jnp.where(jj * SEL_BLOCK <= tt, jnp.where(forced, FORCE_SCORE, score_t), NEG_INF)
    rank = jnp.zeros((n_sel, tq), F32)
    for i in range(n_sel):
        ri = sc[i:i + 1]
        rank = rank + jnp.where(ri > sc, 1.0, jnp.where((ri == sc) & (jj > i), 1.0, 0.0))
    sel_t = jnp.where(rank < SEL_TOP, 1.0, 0.0)
    sel = jnp.concatenate([sel_t, jnp.zeros((LANES - n_sel, tq), F32)], axis=0).T.astype(BF16)

    qrs = (_stack_heads(qr_ref[0]) * (ATTN_SCALE * LOG2E)).astype(BF16)

    m_scr[...] = jnp.full(m_scr.shape, NEG_INF, F32)
    acc_scr[...] = jnp.zeros(acc_scr.shape, F32)

    def body(c, carry):
        k0 = pl.multiple_of(c * kc, kc)
        rows = slc_ref[0, pl.ds(k0, kc), :]
        k = rows[:, :HEAD_DIM].astype(BF16)
        sk = _dot_nt(qrs, k).reshape(N_G, tq, kc)
        kpos = k0 + lax.broadcasted_iota(jnp.int32, (LANES, kc), 1)
        expand = jnp.where(kpos // SEL_BLOCK == lax.broadcasted_iota(jnp.int32, (LANES, kc), 0), 1.0, 0.0)
        chosen = _dot(sel, expand.astype(BF16))
        sm = sk + jnp.where((chosen > 0.5) & (kpos[:1] <= tpos), 0.0, NEG_INF)[None]
        m = m_scr[...]
        m_new = jnp.maximum(m, jnp.max(sm, axis=-1, keepdims=True))
        e = jnp.exp2(sm - m_new)
        pv = _dot(e.reshape(N_G * tq, kc).astype(BF16), _values_and_ones(rows)).reshape(N_G, tq, LANES)
        acc_scr[...] = jnp.exp2(m - m_new) * acc_scr[...] + pv
        m_scr[...] = m_new
        return carry

    lax.fori_loop(0, (t0 + tq + kc - 1) // kc, body, 0)
    acc = acc_scr[...]
    o_slc = acc[:, :, :HEAD_DIM] / jnp.maximum(acc[:, :, HEAD_DIM:HEAD_DIM + 1], TINY)

    o_win = _band_attend(qrs, win_ref, t0, tq, A_WINDOW, None)

    gt = jax.nn.sigmoid(gate_ref[0])
    o_cmp = o_cmp.reshape(N_G, tq, HEAD_DIM)
    for g in range(N_G):
        og = (gt[:, g:g + 1] * o_cmp[g] + gt[:, N_G + g:N_G + g + 1] * o_slc[g]
              + gt[:, 2 * N_G + g:2 * N_G + g + 1] * o_win[g])
        o_ref[0, :, g * HEAD_DIM:(g + 1) * HEAD_DIM] = og


def _band_attend(qrs, rows_ref, t0, tq, window, sinks):
    span = window + tq
    start = pl.multiple_of(jnp.maximum(t0 - window, 0), tq)
    rows = rows_ref[0, pl.ds(start, span), :]
    k = rows[:, :HEAD_DIM].astype(BF16)
    d = (t0 + lax.broadcasted_iota(jnp.int32, (tq, span), 0)) - (start + lax.broadcasted_iota(jnp.int32, (tq, span), 1))
    s = _dot_nt(qrs, k).reshape(N_G, tq, span) + jnp.where((d >= 0) & (d <= window), 0.0, NEG_INF)[None]
    m = jnp.max(s, axis=-1, keepdims=True)
    if sinks is not None:
        sink = jnp.concatenate([jnp.full((1, tq, 1), sk, F32) for sk in sinks], axis=0)
        m = jnp.maximum(m, sink)
    e = jnp.exp2(s - m)
    den = jnp.sum(e, axis=-1, keepdims=True)
    if sinks is not None:
        den = den + jnp.exp2(sink - m)
    o = _dot(e.reshape(N_G * tq, span).astype(BF16), rows[:, HEAD_DIM:].astype(BF16)).reshape(N_G, tq, HEAD_DIM)
    return o / jnp.maximum(den, TINY)


def _values_and_ones(rows):
    lane = lax.broadcasted_iota(jnp.int32, rows.shape, 1)
    return jnp.where(lane < HEAD_DIM, pltpu.roll(rows, HEAD_DIM, 1), 1.0).astype(BF16)


def _nsa_prompt_attend(z, q_rot, cblocks, tq=ATT_Q_TILE, kc=ATT_KEY_CHUNK):
    b, t, _ = z.shape
    n_cmp = cblocks.shape[2]
    hw = N_G * HEAD_DIM
    blk = lambda base: pl.BlockSpec((1, t, LANES), lambda a, k, i: (a, 0, base // LANES + k))
    return pl.pallas_call(
        functools.partial(_nsa_prompt_kernel, tq=tq, kc=kc),
        grid=(b, N_KV, t // tq),
        in_specs=[pl.BlockSpec((1, tq, hw), lambda a, k, i: (a, i, k)),
                  pl.BlockSpec((1, tq, hw), lambda a, k, i: (a, i, k)),
                  pl.BlockSpec((1, 2, n_cmp, HEAD_DIM), lambda a, k, i: (a, k, 0, 0)),
                  blk(A_SLC0), blk(A_WIN0),
                  pl.BlockSpec((1, tq, LANES), lambda a, k, i: (a, i, A_GATE0 // LANES + k))],
        out_specs=pl.BlockSpec((1, tq, hw), lambda a, k, i: (a, i, k)),
        out_shape=jax.ShapeDtypeStruct((b, t, D_MODEL), F32),
        scratch_shapes=[pltpu.VMEM((N_G, tq, 1), F32), pltpu.VMEM((N_G, tq, LANES), F32)],
        compiler_params=_cparams(("arbitrary", "arbitrary", "arbitrary")),
        name="nsa_prompt_attend",
    )(z, q_rot, cblocks, z, z, z)


def _swa_prompt_kernel(sink_ref, q_ref, rows_ref, o_ref, *, tq):
    kv = pl.program_id(1)
    t0 = pl.program_id(2) * tq
    qrs = (_stack_heads(q_ref[0]) * (ATTN_SCALE * LOG2E)).astype(BF16)
    sinks = [sink_ref[kv, g] * LOG2E for g in range(N_G)]
    o = _band_attend(qrs, rows_ref, t0, tq, B_WINDOW, sinks)
    for g in range(N_G):
        o_ref[0, :, g * HEAD_DIM:(g + 1) * HEAD_DIM] = o[g]


def _swa_prompt_attend(z, sinks, tq=ATT_Q_TILE):
    b, t, _ = z.shape
    hw = N_G * HEAD_DIM
    return pl.pallas_call(
        functools.partial(_swa_prompt_kernel, tq=tq),
        grid=(b, N_KV, t // tq),
        in_specs=[pl.BlockSpec(memory_space=pltpu.SMEM),
                  pl.BlockSpec((1, tq, hw), lambda a, k, i: (a, i, k)),
                  pl.BlockSpec((1, t, LANES), lambda a, k, i: (a, 0, D_MODEL // LANES + k))],
        out_specs=pl.BlockSpec((1, tq, hw), lambda a, k, i: (a, i, k)),
        out_shape=jax.ShapeDtypeStruct((b, t, D_MODEL), F32),
        compiler_params=_cparams(("arbitrary", "arbitrary", "arbitrary")),
        name="swa_prompt_attend",
    )(sinks.reshape(N_KV, N_G), z, z)


S_CHUNKS = 520
S_SEL = 256
N_SEL_SAMPLE = PAST_LEN // SEL_BLOCK + 1
SLABS = CMP_STRIDE // SUBLANES
PAGES_PER_ITER = 16
assert N_SEL_SAMPLE == LANES + 1 and DEC_SEQ == SUBLANES and N_PAGES % PAGES_PER_ITER == 0


def _page_copy(cache_hbm, ia, pt_ref, n, p, buf, slot, sem):
    return pltpu.make_async_copy(cache_hbm.at[ia, pt_ref[n, p]],
                                 buf.at[slot, :, pl.ds(pl.multiple_of(p * PAGE_SIZE, PAGE_SIZE), PAGE_SIZE)],
                                 sem.at[slot])


def _gather_pages(cache_hbm, ia, pt_ref, buf, sem):
    n = pl.program_id(0)
    nn = pl.num_programs(0)
    slot = n % 2

    def start(seq, sl):
        def go(p, c):
            _page_copy(cache_hbm, ia, pt_ref, seq, p, buf, sl, sem).start()
            return c
        lax.fori_loop(0, N_PAGES, go, 0)

    @pl.when(n == 0)
    def _():
        start(0, 0)

    @pl.when(n + 1 < nn)
    def _():
        start(n + 1, 1 - slot)

    def wait(p, c):
        _page_copy(cache_hbm, ia, pt_ref, n, p, buf, slot, sem).wait()
        return c
    lax.fori_loop(0, N_PAGES, wait, 0)
    return slot


def _sample_q_rows(q, kv):
    return jnp.concatenate([q[:, (kv * N_G + g) * HEAD_DIM:(kv * N_G + g + 1) * HEAD_DIM] for g in range(N_G)], axis=0)


def _sample_q_blockdiag(q):
    parts = []
    for kv in range(N_KV):
        qs = _sample_q_rows(q, kv)
        left = jnp.zeros((qs.shape[0], kv * 2 * HEAD_DIM), F32)
        right = jnp.zeros((qs.shape[0], KV_ROW - kv * 2 * HEAD_DIM - HEAD_DIM), F32)
        parts.append(jnp.concatenate([left, qs, right], axis=1) if kv else jnp.concatenate([qs, right], axis=1))
    return jnp.concatenate(parts, axis=0)


def _nsa_sample_cmp_kernel(pt_ref, new_ref, q_ref, w1r_ref, pb_ref, w2_ref, smat_ref, cache_hbm,
                           ocmp_ref, sel_ref, pages, rows, chunks, sem, *, ia):
    slot = _gather_pages(cache_hbm, ia, pt_ref, pages, sem)

    def to_rows(it, c):
        for u in range(PAGES_PER_ITER):
            p = it * PAGES_PER_ITER + u
            r0 = pl.multiple_of(p * PAGE_SIZE, PAGE_SIZE)
            c0 = pl.multiple_of(p * (PAGE_SIZE // SLABS), PAGE_SIZE // SLABS)
            for kv in range(N_KV):
                t = pages[slot, kv * LANES:(kv + 1) * LANES, pl.ds(r0, PAGE_SIZE)].T
                for v in range(PAGE_SIZE // SUBLANES):
                    rows[kv, v % SLABS, pl.ds(c0 + (v // SLABS) * SUBLANES, SUBLANES), :] = t[v * SUBLANES:(v + 1) * SUBLANES]
        return c
    lax.fori_loop(0, N_PAGES // PAGES_PER_ITER, to_rows, 0)
    new_rows = new_ref[0]
    tail = (PAST_LEN // CMP_STRIDE) * SUBLANES
    for kv in range(N_KV):
        rows[kv, 0, tail:tail + DEC_SEQ, :] = new_rows[:, kv * LANES:(kv + 1) * LANES]
        rows[kv, 0, tail + DEC_SEQ:, :] = jnp.zeros((S_CHUNKS * SUBLANES - tail - DEC_SEQ, LANES), F32)
        for slab in range(1, SLABS):
            rows[kv, slab, tail:, :] = jnp.zeros((S_CHUNKS * SUBLANES - tail, LANES), F32)

    lo = lax.broadcasted_iota(jnp.int32, (S_CHUNKS, LANES), 1) < HEAD_DIM
    for k in range(CMP_STRIDE // 2):
        for kv in range(N_KV):
            sa, sb = 2 * k, 2 * k + 1
            a = rows[kv, sa // SUBLANES, pl.ds(sa % SUBLANES, S_CHUNKS, stride=SUBLANES), :]
            b = rows[kv, sb // SUBLANES, pl.ds(sb % SUBLANES, S_CHUNKS, stride=SUBLANES), :]
            chunks[2 * kv, :, k * LANES:(k + 1) * LANES] = jnp.where(lo, a, pltpu.roll(b, HEAD_DIM, 1)).astype(BF16)
            chunks[2 * kv + 1, :, k * LANES:(k + 1) * LANES] = jnp.where(lo, pltpu.roll(a, HEAD_DIM, 1), b).astype(BF16)

    q = q_ref[0]
    qpos = PAST_LEN + lax.broadcasted_iota(jnp.int32, (DEC_SEQ, 1), 0)
    cend = lax.broadcasted_iota(jnp.int32, (DEC_SEQ, S_CHUNKS), 1) * CMP_STRIDE + CMP_BLOCK - 1
    cmask = (cend <= qpos)[None]
    imps = []
    for kv in range(N_KV):
        k_c = _compress_blocks(chunks[2 * kv], w1r_ref[0], pb_ref[0], w2_ref[0]).astype(BF16)
        v_c = _compress_blocks(chunks[2 * kv + 1], w1r_ref[1], pb_ref[1], w2_ref[1]).astype(BF16)
        qs = _sample_q_rows(q, kv).astype(BF16)
        s = (_dot_nt(qs, k_c) * ATTN_SCALE).reshape(N_G, DEC_SEQ, S_CHUNKS)
        p = _masked_probs(s, cmask)
        o = _dot(p.reshape(N_G * DEC_SEQ, S_CHUNKS).astype(BF16), v_c)
        for g in range(N_G):
            h = kv * N_G + g
            ocmp_ref[0, :, h * HEAD_DIM:(h + 1) * HEAD_DIM] = o[g * DEC_SEQ:(g + 1) * DEC_SEQ]
        imps.append(jnp.sum(p, axis=0))
    imp = jnp.concatenate(imps, axis=0)

    score = _dot_nt(imp, smat_ref[...], precision=lax.Precision.HIGHEST)
    shape = score.shape
    jj = lax.broadcasted_iota(jnp.int32, shape, 1)
    tt = PAST_LEN + lax.broadcasted_iota(jnp.int32, shape, 0) % DEC_SEQ
    cur = tt // SEL_BLOCK
    forced = (jj == 0) | (jj == cur) | (jj == cur - 1)
    sc = jnp.where(jj * SEL_BLOCK <= tt, jnp.where(forced, FORCE_SCORE, score), NEG_INF)

    sc_a = sc[:, :LANES]
    extra = jnp.sum(jnp.where(jj == LANES, sc, 0.0), axis=1, keepdims=True)
    lane = lax.broadcasted_iota(jnp.int32, sc_a.shape, 1)
    rank = jnp.where(extra > sc_a, 1.0, 0.0)
    for k in range(1, LANES):
        r = pltpu.roll(sc_a, k, 1)
        rank = rank + jnp.where(r > sc_a, 1.0, jnp.where(r == sc_a, jnp.where(lane >= k, 1.0, 0.0), 0.0))
    rank_x = jnp.sum(jnp.where(sc_a >= extra, 1.0, 0.0), axis=1, keepdims=True)
    sel_ref[0, :, :LANES] = jnp.where(rank < SEL_TOP, 1.0, 0.0)
    sel_ref[0, :, LANES:] = jnp.where((lane == 0) & (rank_x < SEL_TOP), 1.0, 0.0)


def _nsa_sample_cmp(ia, page_table, cache, new_rows, q, w1r, pb, w2):
    n = q.shape[0]
    grid_spec = pltpu.PrefetchScalarGridSpec(
        num_scalar_prefetch=1,
        grid=(n,),
        in_specs=[pl.BlockSpec((1, DEC_SEQ, KV_ROW), lambda a, pt: (a, 0, 0)),
                  pl.BlockSpec((1, DEC_SEQ, D_MODEL), lambda a, pt: (a, 0, 0)),
                  pl.BlockSpec((2, CHUNK_W, 2 * CMP_HIDDEN), lambda a, pt: (0, 0, 0)),
                  pl.BlockSpec((2, 1, CMP_HIDDEN), lambda a, pt: (0, 0, 0)),
                  pl.BlockSpec((2, CMP_HIDDEN, HEAD_DIM), lambda a, pt: (0, 0, 0)),
                  pl.BlockSpec((S_SEL, S_CHUNKS), lambda a, pt: (0, 0)),
                  pl.BlockSpec(memory_space=pl.ANY)],
        out_specs=[pl.BlockSpec((1, DEC_SEQ, D_MODEL), lambda a, pt: (a, 0, 0)),
                   pl.BlockSpec((1, N_KV * DEC_SEQ, S_SEL), lambda a, pt: (a, 0, 0))],
        scratch_shapes=[pltpu.VMEM((2, KV_ROW, PAST_LEN), F32),
                        pltpu.VMEM((N_KV, SLABS, S_CHUNKS * SUBLANES, LANES), F32),
                        pltpu.VMEM((2 * N_KV, S_CHUNKS, CHUNK_W), BF16),
                        pltpu.SemaphoreType.DMA((2,))])
    return pl.pallas_call(
        functools.partial(_nsa_sample_cmp_kernel, ia=ia),
        grid_spec=grid_spec,
        out_shape=[jax.ShapeDtypeStruct((n, DEC_SEQ, D_MODEL), F32),
                   jax.ShapeDtypeStruct((n, N_KV * DEC_SEQ, S_SEL), F32)],
        compiler_params=_cparams(("arbitrary",)),
        name="nsa_sample_cmp",
    )(page_table, new_rows, q, w1r, pb, w2, _chunk_score_matrix_t(S_SEL, S_CHUNKS), cache)


def _window_cached_attend(qbd, wt, new, window, sink):
    nq = qbd.shape[0]
    heads = nq // DEC_SEQ
    qpos = PAST_LEN + lax.broadcasted_iota(jnp.int32, (DEC_SEQ, 1), 0)
    d_p = qpos - (PAST_LEN - window + lax.broadcasted_iota(jnp.int32, (DEC_SEQ, window), 1))
    d_n = qpos - (PAST_LEN + lax.broadcasted_iota(jnp.int32, (DEC_SEQ, DEC_SEQ), 1))
    s_p = _dot(qbd, wt).reshape(heads, DEC_SEQ, window) + jnp.where((d_p >= 0) & (d_p <= window), 0.0, NEG_INF)[None]
    s_n = _dot_nt(qbd, new).reshape(heads, DEC_SEQ, DEC_SEQ) + jnp.where((d_n >= 0) & (d_n <= window), 0.0, NEG_INF)[None]
    m = jnp.maximum(jnp.max(s_p, axis=-1, keepdims=True), jnp.max(s_n, axis=-1, keepdims=True))
    if sink is not None:
        m = jnp.maximum(m, sink)
    e_p = jnp.exp(s_p - m)
    e_n = jnp.exp(s_n - m)
    den = jnp.sum(e_p, axis=-1, keepdims=True) + jnp.sum(e_n, axis=-1, keepdims=True)
    if sink is not None:
        den = den + jnp.exp(sink - m)
    o = _dot_nt(e_p.reshape(nq, window).astype(BF16), wt) + _dot(e_n.reshape(nq, DEC_SEQ).astype(BF16), new)
    return o / jnp.maximum(den.reshape(nq, 1), TINY)


def _advance_window(wt, new):
    w = wt.shape[1]
    shifted = pltpu.roll(wt, w - DEC_SEQ, 1)
    new_t = jnp.concatenate([new, jnp.zeros((LANES - DEC_SEQ, KV_ROW), F32)], axis=0).T
    new_t = pltpu.roll(new_t, LANES - DEC_SEQ, 1)
    lane = lax.broadcasted_iota(jnp.int32, (KV_ROW, LANES), 1)
    tail = jnp.where(lane >= LANES - DEC_SEQ, new_t, shifted[:, w - LANES:])
    return tail if w == LANES else jnp.concatenate([shifted[:, :w - LANES], tail], axis=1)


def _nsa_sample_slc_kernel(pt_ref, new_ref, qr_ref, sel_ref, wbuf_ref, wnew_ref, gate_ref, ocmp_ref, expand_ref,
                           cache_hbm, o_ref, wout_ref, pages, pages_bf, sem, *, ia):
    slot = _gather_pages(cache_hbm, ia, pt_ref, pages, sem)
    wout_ref[0] = _advance_window(wbuf_ref[0], wnew_ref[0])
    pages_bf[...] = pages[slot].astype(BF16)

    nq = N_KV * N_G * DEC_SEQ
    qbd = (_sample_q_blockdiag(qr_ref[0]) * ATTN_SCALE).astype(BF16)
    qpos = PAST_LEN + lax.broadcasted_iota(jnp.int32, (DEC_SEQ, 1), 0)

    past = pages_bf[...]
    new = new_ref[0].astype(BF16)
    s_p = _dot(qbd, past).reshape(N_KV, N_G, DEC_SEQ, PAST_LEN)
    s_n = _dot_nt(qbd, new).reshape(N_KV, N_G, DEC_SEQ, DEC_SEQ)
    sel = sel_ref[0]
    sel_bf = sel.astype(BF16)
    chosen = _dot(sel_bf, expand_ref[...]).reshape(N_KV, 1, DEC_SEQ, PAST_LEN)
    bias_p = jnp.where(chosen > 0.5, 0.0, NEG_INF)
    new_blk = sel[:, PAST_LEN // SEL_BLOCK:PAST_LEN // SEL_BLOCK + 1].reshape(N_KV, 1, DEC_SEQ, 1)
    npos = PAST_LEN + lax.broadcasted_iota(jnp.int32, (DEC_SEQ, DEC_SEQ), 1)
    bias_n = jnp.where((new_blk > 0.5) & (npos <= qpos)[None, None], 0.0, NEG_INF)
    s_p = s_p + bias_p
    s_n = s_n + bias_n
    m = jnp.maximum(jnp.max(s_p, axis=-1, keepdims=True), jnp.max(s_n, axis=-1, keepdims=True))
    e_p = jnp.exp(s_p - m)
    e_n = jnp.exp(s_n - m)
    den = jnp.sum(e_p, axis=-1, keepdims=True) + jnp.sum(e_n, axis=-1, keepdims=True)
    o_slc = (_dot_nt(e_p.reshape(nq, PAST_LEN).astype(BF16), past)
             + _dot(e_n.reshape(nq, DEC_SEQ).astype(BF16), new)) / jnp.maximum(den.reshape(nq, 1), TINY)

    o_win = _window_cached_attend(qbd, wbuf_ref[0].astype(BF16), wnew_ref[0].astype(BF16), A_WINDOW, None)

    gt = jax.nn.sigmoid(gate_ref[0])
    o_cmp = ocmp_ref[0]
    for kv in range(N_KV):
        v0 = kv * 2 * HEAD_DIM + HEAD_DIM
        for g in range(N_G):
            h = kv * N_G + g
            r0 = h * DEC_SEQ
            c0 = kv * LANES + g
            og = (gt[:, c0:c0 + 1] * o_cmp[:, h * HEAD_DIM:(h + 1) * HEAD_DIM]
                  + gt[:, c0 + N_G:c0 + N_G + 1] * o_slc[r0:r0 + DEC_SEQ, v0:v0 + HEAD_DIM]
                  + gt[:, c0 + 2 * N_G:c0 + 2 * N_G + 1] * o_win[r0:r0 + DEC_SEQ, v0:v0 + HEAD_DIM])
            o_ref[0, :, h * HEAD_DIM:(h + 1) * HEAD_DIM] = og


def _nsa_sample_slc(ia, page_table, cache, z, q_rot, sel, win_cache, o_cmp):
    n = z.shape[0]
    expand = (jnp.arange(PAST_LEN)[None, :] // SEL_BLOCK == jnp.arange(S_SEL)[:, None]).astype(BF16)
    row_blk = lambda base: pl.BlockSpec((1, DEC_SEQ, KV_ROW), lambda a, pt: (a, 0, base // KV_ROW))
    grid_spec = pltpu.PrefetchScalarGridSpec(
        num_scalar_prefetch=1,
        grid=(n,),
        in_specs=[row_blk(A_SLC0),
                  pl.BlockSpec((1, DEC_SEQ, D_MODEL), lambda a, pt: (a, 0, 0)),
                  pl.BlockSpec((1, N_KV * DEC_SEQ, S_SEL), lambda a, pt: (a, 0, 0)),
                  pl.BlockSpec((None, 1, KV_ROW, A_WINDOW), lambda a, pt: (ia, a, 0, 0)),
                  row_blk(A_WIN0), row_blk(A_GATE0),
                  pl.BlockSpec((1, DEC_SEQ, D_MODEL), lambda a, pt: (a, 0, 0)),
                  pl.BlockSpec((S_SEL, PAST_LEN), lambda a, pt: (0, 0)),
                  pl.BlockSpec(memory_space=pl.ANY)],
        out_specs=[pl.BlockSpec((1, DEC_SEQ, D_MODEL), lambda a, pt: (a, 0, 0)),
                   pl.BlockSpec((1, KV_ROW, A_WINDOW), lambda a, pt: (a, 0, 0))],
        scratch_shapes=[pltpu.VMEM((2, KV_ROW, PAST_LEN), F32),
                        pltpu.VMEM((KV_ROW, PAST_LEN), BF16),
                        pltpu.SemaphoreType.DMA((2,))])
    return pl.pallas_call(
        functools.partial(_nsa_sample_slc_kernel, ia=ia),
        grid_spec=grid_spec,
        out_shape=[jax.ShapeDtypeStruct((n, DEC_SEQ, D_MODEL), F32),
                   jax.ShapeDtypeStruct((n, KV_ROW, A_WINDOW), F32)],
        compiler_params=_cparams(("arbitrary",)),
        name="nsa_sample_slc",
    )(page_table, z, q_rot, sel, win_cache, z, z, o_cmp, expand, cache)


def _swa_sample_kernel(q_ref, new_ref, wbuf_ref, sink_ref, o_ref, wout_ref):
    def one(i, c):
        new = new_ref[i]
        qbd = (_sample_q_blockdiag(q_ref[i]) * ATTN_SCALE).astype(BF16)
        o = _window_cached_attend(qbd, wbuf_ref[i].astype(BF16), new.astype(BF16), B_WINDOW, sink_ref[...])
        for kv in range(N_KV):
            v0 = kv * 2 * HEAD_DIM + HEAD_DIM
            for g in range(N_G):
                h = kv * N_G + g
                o_ref[i, :, h * HEAD_DIM:(h + 1) * HEAD_DIM] = o[h * DEC_SEQ:(h + 1) * DEC_SEQ, v0:v0 + HEAD_DIM]
        wout_ref[i] = _advance_window(wbuf_ref[i], new)
        return c
    lax.fori_loop(0, q_ref.shape[0], one, 0)


def _swa_sample_attend(ib, z, win_cache, sinks):
    n = z.shape[0]
    nb = math.gcd(n, SWA_SEQS_PER_STEP)
    return pl.pallas_call(
        _swa_sample_kernel,
        grid=(n // nb,),
        in_specs=[pl.BlockSpec((nb, DEC_SEQ, D_MODEL), lambda a: (a, 0, 0)),
                  pl.BlockSpec((nb, DEC_SEQ, KV_ROW), lambda a: (a, 0, D_MODEL // KV_ROW)),
                  pl.BlockSpec((None, nb, KV_ROW, B_WINDOW), lambda a: (ib, a, 0, 0)),
                  pl.BlockSpec((N_HEADS, 1, 1), lambda a: (0, 0, 0))],
        out_specs=[pl.BlockSpec((nb, DEC_SEQ, D_MODEL), lambda a: (a, 0, 0)),
                   pl.BlockSpec((nb, KV_ROW, B_WINDOW), lambda a: (a, 0, 0))],
        out_shape=[jax.ShapeDtypeStruct((n, DEC_SEQ, D_MODEL), F32),
                   jax.ShapeDtypeStruct((n, KV_ROW, B_WINDOW), F32)],
        compiler_params=_cparams(("arbitrary",)),
        name="swa_sample_attend",
    )(z, z, win_cache, sinks.reshape(N_HEADS, 1, 1))


def _nsa_col_perm():
    perm = np.full((A_COLS,), -1, np.int64)
    perm[:D_MODEL] = np.arange(D_MODEL)
    kvw = N_KV * HEAD_DIM
    for br, base in enumerate((A_CMP0, A_SLC0, A_WIN0)):
        old0 = D_MODEL + 2 * kvw * br
        for kv in range(N_KV):
            for i in range(2):
                dst = base + kv * 2 * HEAD_DIM + i * HEAD_DIM
                src = old0 + i * kvw + kv * HEAD_DIM
                perm[dst:dst + HEAD_DIM] = np.arange(src, src + HEAD_DIM)
    old_g = D_MODEL + 6 * kvw
    for kv in range(N_KV):
        for br in range(3):
            dst = A_GATE0 + kv * LANES + br * N_G
            src = old_g + br * N_HEADS + kv * N_G
            perm[dst:dst + N_G] = np.arange(src, src + N_G)
    return perm


def _swa_col_perm():
    perm = np.zeros((B_COLS,), np.int64)
    perm[:D_MODEL] = np.arange(D_MODEL)
    kvw = N_KV * HEAD_DIM
    for kv in range(N_KV):
        for i in range(2):
            dst = D_MODEL + kv * 2 * HEAD_DIM + i * HEAD_DIM
            src = D_MODEL + i * kvw + kv * HEAD_DIM
            perm[dst:dst + HEAD_DIM] = np.arange(src, src + HEAD_DIM)
    return perm


def _permute_cols(w, perm):
    parts, start = [], 0
    for j in range(1, len(perm) + 1):
        if j == len(perm) or (perm[j] != perm[j - 1] + 1 if perm[j - 1] >= 0 else perm[j] >= 0):
            if perm[start] < 0:
                parts.append(jnp.zeros(w.shape[:-1] + (j - start,), w.dtype))
            else:
                parts.append(w[..., int(perm[start]):int(perm[start]) + j - start])
            start = j
    return jnp.concatenate(parts, axis=-1)


def _rows_view(z, c0, n, t):
    return z[..., c0:c0 + KV_ROW].reshape(n, t, N_KV, 2, HEAD_DIM)


def _rows_from_t(rows_t):
    n, _, t = rows_t.shape
    return rows_t.reshape(n, N_KV, 2, HEAD_DIM, t).transpose(0, 4, 1, 2, 3)


def _trunk(x, ada, past, p, prompt):
    g, t, _ = x.shape
    tm = min(ROW_TILE, t)
    if prompt:
        pos = jnp.arange(t, dtype=jnp.int32)
        n_seq, t_seq = g, t
    else:
        n_seq, t_seq = t // DEC_SEQ, DEC_SEQ
        pos = jnp.tile(PAST_LEN + jnp.arange(DEC_SEQ, dtype=jnp.int32), n_seq)
    tab = _rope_tables(pos)[None]
    cmp_out, slc_out, nwin_out, swin_out = [], [], [], []
    for l in range(DEPTH):
        sh1, sc1, g1, sh2, sc2, g2 = [(ada, l, k) for k in range(6)]
        i = l // 2
        if l % 2 == 0:
            z, q_rot, *rows_t = _proj(x, sc1, sh1, p['nsa_w_in'][i], p['zero_a'], tab, tm, (A_CMP0,),
                                      (A_SLC0, A_WIN0), True, prompt)
            if prompt:
                cmp_rows, slc_rows, win_rows = [_rows_from_t(r) for r in rows_t]
            else:
                cmp_rows, slc_rows, win_rows = [_rows_view(z, c0, n_seq, t_seq) for c0 in (A_CMP0, A_SLC0, A_WIN0)]
            w1r, pb, w2 = p['cmp_w1r'][i], p['cmp_pb'][i], p['cmp_w2'][i]
            if prompt:
                nch = t // CMP_STRIDE
                chunks = z[..., A_CMP0:A_CMP0 + KV_ROW].reshape(g, nch, CMP_STRIDE, 2 * N_KV, HEAD_DIM)
                chunks = chunks.transpose(0, 3, 1, 2, 4).reshape(g, 2 * N_KV, nch, CHUNK_W)
                cblocks = _compress_prompt(chunks, w1r, pb, w2)
                o = _nsa_prompt_attend(z, q_rot, cblocks)
                nwin_out.append(win_rows[:, -min(A_WINDOW, t):])
            else:
                cache_cmp, cache_slc, nwin_t, _, page_table = past
                zs = z.reshape(n_seq, DEC_SEQ, A_COLS)
                qs = zs[..., :D_MODEL]
                o_cmp, sel = _nsa_sample_cmp(i, page_table, cache_cmp, zs[..., A_CMP0:A_CMP0 + KV_ROW], qs, w1r, pb, w2)
                o, win_next = _nsa_sample_slc(i, page_table, cache_slc, zs, q_rot.reshape(n_seq, DEC_SEQ, D_MODEL), sel,
                                              nwin_t, o_cmp)
                o = o.reshape(g, t, D_MODEL)
                nwin_out.append(_rows_from_t(win_next))
            cmp_out.append(cmp_rows)
            slc_out.append(slc_rows)
            w_o, b_o = p['nsa_w_o'][i], p['zero_d']
        else:
            z, *rows_t = _proj(x, sc1, sh1, p['swa_w_in'][i], p['swa_b_in'][i], tab, tm, (), (D_MODEL,), False, prompt)
            rows = _rows_from_t(rows_t[0]) if prompt else _rows_view(z, D_MODEL, n_seq, t_seq)
            if prompt:
                o = _swa_prompt_attend(z, p['swa_sinks'][i])
                swin_out.append(rows[:, -min(B_WINDOW, t):])
            else:
                o, win_next = _swa_sample_attend(i, z.reshape(n_seq, DEC_SEQ, B_COLS), past[3], p['swa_sinks'][i])
                o = o.reshape(g, t, D_MODEL)
                swin_out.append(_rows_from_t(win_next))
            w_o, b_o = p['swa_w_o'][i], p['swa_b_o'][i]
        x = _outproj_ln(o, x, g1, w_o, b_o, p['ln_g'][l, 0], p['ln_b'][l, 0], tm)
        x = _moe_ln(x, sc2, sh2, g2, p['router_wt'], p['router_b'], l, p['moe_w_gate'], p['moe_w_up'],
                    p['moe_w_down'], p['ln_g'][l, 1], p['ln_b'][l, 1], min(MOE_TILE, t))
    return x, (jnp.stack(cmp_out), jnp.stack(slc_out), jnp.stack(nwin_out), jnp.stack(swin_out))


def kernel(x_prompt, x_sample, cache_nsa_cmp, cache_nsa_slc, cache_nsa_win, cache_swa_win, page_table, c_prompt, c_sample, nsa_w_in, nsa_cmp_pe, nsa_cmp_w1, nsa_cmp_b1, nsa_cmp_w2, nsa_w_o, swa_w_in, swa_b_in, swa_sinks, swa_w_o, swa_b_o, ada_w, ada_b, ln_g, ln_b, router_w, router_b, moe_w_gate, moe_w_up, moe_w_down):
    n_a = nsa_w_in.shape[0]
    batch, seq, _ = x_prompt.shape
    n_dec = x_sample.shape[0]

    pe_term = jnp.stack([_pe_term(nsa_cmp_pe[i], nsa_cmp_w1[i]) for i in range(n_a)])
    w1r = nsa_cmp_w1.reshape(n_a, 2, CMP_SHIFTS, CHUNK_W, CMP_HIDDEN).transpose(0, 1, 3, 2, 4)
    p = {
        'nsa_w_in': _permute_cols(nsa_w_in, _nsa_col_perm()).astype(BF16),
        'zero_a': jnp.zeros((1, A_COLS), F32),
        'zero_d': jnp.zeros((1, D_MODEL), F32),
        'cmp_w1r': w1r.reshape(n_a, 2, CHUNK_W, CMP_SHIFTS * CMP_HIDDEN).astype(BF16),
        'cmp_pb': (pe_term + nsa_cmp_b1).reshape(n_a, 2, 1, CMP_HIDDEN),
        'cmp_w2': nsa_cmp_w2.astype(BF16),
        'nsa_w_o': nsa_w_o.astype(BF16),
        'swa_w_in': _permute_cols(swa_w_in, _swa_col_perm()).astype(BF16),
        'swa_b_in': _permute_cols(swa_b_in, _swa_col_perm())[:, None, :],
        'swa_sinks': swa_sinks,
        'swa_w_o': swa_w_o.astype(BF16),
        'swa_b_o': swa_b_o[:, None, :],
        'ln_g': ln_g[:, :, None, :],
        'ln_b': ln_b[:, :, None, :],
        'router_wt': router_w.T,
        'router_b': router_b[:, None],
        'moe_w_gate': moe_w_gate.astype(BF16),
        'moe_w_up': moe_w_up.astype(BF16),
        'moe_w_down': moe_w_down.astype(BF16),
    }

    n_tok = n_dec * DEC_SEQ
    ada = _ada(jnp.concatenate([jnp.repeat(c_sample, DEC_SEQ, axis=0), c_prompt], axis=0), ada_w, ada_b)
    ada_s = ada[:, None]
    ada_p = ada[:, n_tok:, None, :]

    y_p, (cmp_p, slc_p, nwin_p, swin_p) = _trunk(x_prompt, ada_p, None, p, True)
    def feature_major(c):
        return c.transpose(0, 1, 3, 4, 5, 2).reshape(c.shape[0], c.shape[1], KV_ROW, c.shape[2])
    past = (feature_major(cache_nsa_cmp), feature_major(cache_nsa_slc), feature_major(cache_nsa_win),
            feature_major(cache_swa_win), page_table)
    y_s, (cmp_s, slc_s, nwin_s, swin_s) = _trunk(x_sample.reshape(1, n_dec * DEC_SEQ, D_MODEL), ada_s, past, p, False)
    return (y_p, y_s.reshape(n_dec, DEC_SEQ, D_MODEL), cmp_p, slc_p, nwin_p, swin_p, cmp_s, slc_s, nwin_s, swin_s)
```
